```python
import jax, jax.numpy as jnp
from jax import lax
import numpy as np

D_MODEL = 1024
BATCH = 8
SEQ = 2048
DEPTH = 1

CHUNK = 64
N_META = 16
Q_BLOCK = 128
ROPE_THETA = 10000.0
RMS_EPS = 1e-6
D_MIX = D_MODEL
SB_HEAD_DIM = 64
SB_HEADS = (D_MIX // 2) // SB_HEAD_DIM
SB_WIDTH = SB_HEADS * SB_HEAD_DIM
DIFF_HEAD_DIM = 64
DIFF_V_DIM = 2 * DIFF_HEAD_DIM
DIFF_HEADS = (D_MIX // 2) // DIFF_V_DIM
DIFF_QK_WIDTH = DIFF_HEADS * 2 * DIFF_HEAD_DIM
DIFF_WIDTH = DIFF_HEADS * DIFF_V_DIM
SPLIT_SIZES = (SB_WIDTH, SB_WIDTH, SB_WIDTH, SB_WIDTH,
               DIFF_QK_WIDTH, DIFF_QK_WIDTH, DIFF_WIDTH, DIFF_WIDTH)
IN_COLS = int(sum(SPLIT_SIZES))

kernel_name = "hymba_stickbreak_diffattn_chunk_causal"


def rmsnorm(x, g):
    xf = x.astype(jnp.float32)
    y = xf * lax.rsqrt(jnp.mean(xf * xf, axis=-1, keepdims=True) + RMS_EPS)
    return (y * g.astype(jnp.float32)).astype(x.dtype)


def rope_tables(length, dim):
    inv = 1.0 / (ROPE_THETA ** (jnp.arange(0, dim, 2, dtype=jnp.float32) / dim))
    ang = jnp.arange(length, dtype=jnp.float32)[:, None] * inv[None, :]
    return jnp.cos(ang), jnp.sin(ang)


def apply_rope(x, cos, sin):
    xf = x.astype(jnp.float32)
    x1, x2 = jnp.split(xf, 2, axis=-1)
    c = cos[None, :, None, :]
    s = sin[None, :, None, :]
    return jnp.concatenate([x1 * c - x2 * s, x2 * c + x1 * s], axis=-1).astype(x.dtype)


def chunk_ids(length):
    p = jnp.arange(length)
    return jnp.where(p < N_META, 0, (p - N_META) // CHUNK + 1)


def chunk_end(pos):
    if pos < N_META:
        return N_META
    return N_META + CHUNK * ((pos - N_META) // CHUNK + 1)


def stick_breaking_attention(q, k, v):
    Lp = q.shape[1]
    d = q.shape[-1]
    scale = d ** -0.5
    outs = []
    for t0 in range(0, Lp, Q_BLOCK):
        t1 = t0 + Q_BLOCK
        z = jnp.einsum('bqhd,bkhd->bhqk', q[:, t0:t1].astype(jnp.float32),
                       k[:, :t1].astype(jnp.float32)) * scale
        strict = jnp.arange(t1)[None, :] < jnp.arange(t0, t1)[:, None]
        log_beta = jax.nn.log_sigmoid(z)
        log_keep = jnp.where(strict, jax.nn.log_sigmoid(-z), 0.0)
        stick = jnp.pad(lax.cumsum(log_keep[..., 1:], axis=3, reverse=True),
                        ((0, 0), (0, 0), (0, 0), (0, 1)))
        w = jnp.where(strict, jnp.exp(log_beta + stick), 0.0)
        o = jnp.einsum('bhqk,bkhd->bqhd', w, v[:, :t1].astype(jnp.float32))
        outs.append(o.astype(v.dtype))
    return jnp.concatenate(outs, axis=1)


def differential_attention(q, k, v, lam, chunk):
    Lp = q.shape[1]
    d = q.shape[-1]
    scale = d ** -0.5
    outs = []
    for t0 in range(0, Lp, Q_BLOCK):
        t1 = t0 + Q_BLOCK
        kl = min(chunk_end(t1 - 1), Lp)
        s = jnp.einsum('bqhcd,bkhcd->bhcqk', q[:, t0:t1].astype(jnp.float32),
                       k[:, :kl].astype(jnp.float32)) * scale
        mask = chunk[t0:t1][:, None] >= chunk[:kl][None, :]
        s = jnp.where(mask, s, -1e30)
        p = jax.nn.softmax(s, axis=-1)
        a = p[:, :, 0] - lam * p[:, :, 1]
        o = jnp.einsum('bhqk,bkhe->bqhe', a, v[:, :kl].astype(jnp.float32))
        outs.append(o.astype(v.dtype))
    return jnp.concatenate(outs, axis=1)


def hybrid_layer(h, norm_g, w_in, w_out, lq1, lk1, lq2, lk2, subln_g,
                 layer_idx, cos, sin, chunk, Lp):
    B, L, _ = h.shape
    u = rmsnorm(h, norm_g)
    proj = jnp.einsum('bld,dc->blc', u, w_in.astype(u.dtype))
    proj = jnp.pad(proj, ((0, 0), (0, Lp - L), (0, 0)))
    idx = [int(i) for i in np.cumsum(SPLIT_SIZES)[:-1]]
    sb_q, sb_k, sb_v, sb_g, df_q, df_k, df_v, df_g = jnp.split(proj, idx, axis=-1)

    shp_sb = (B, Lp, SB_HEADS, SB_HEAD_DIM)
    sb_o = stick_breaking_attention(sb_q.reshape(shp_sb), sb_k.reshape(shp_sb),
                                    sb_v.reshape(shp_sb)).reshape(B, Lp, SB_WIDTH)

    shp_qk = (B, Lp, 2 * DIFF_HEADS, DIFF_HEAD_DIM)
    dq = apply_rope(df_q.reshape(shp_qk), cos, sin).reshape(B, Lp, DIFF_HEADS, 2, DIFF_HEAD_DIM)
    dk = apply_rope(df_k.reshape(shp_qk), cos, sin).reshape(B, Lp, DIFF_HEADS, 2, DIFF_HEAD_DIM)
    dv = df_v.reshape(B, Lp, DIFF_HEADS, DIFF_V_DIM)
    lambda_init = 0.8 - 0.6 * float(np.exp(-0.3 * layer_idx))
    lam = (jnp.exp(jnp.sum(lq1.astype(jnp.float32) * lk1.astype(jnp.float32)))
           - jnp.exp(jnp.sum(lq2.astype(jnp.float32) * lk2.astype(jnp.float32)))
           + lambda_init)
    df_o = differential_attention(dq, dk, dv, lam, chunk)
    df_o = (rmsnorm(df_o, subln_g) * (1.0 - lambda_init)).reshape(B, Lp, DIFF_WIDTH)

    mix = jnp.concatenate([sb_o * jax.nn.silu(sb_g), df_o * jax.nn.silu(df_g)], axis=-1)[:, :L]
    return h + jnp.einsum('blc,cd->bld', mix, w_out.astype(mix.dtype))


def setup_inputs(seed: int = 0) -> dict:
    key = jax.random.key(seed)
    ks = jax.random.split(key, 11)
    f32 = jnp.float32
    x = jax.random.normal(ks[0], (BATCH, SEQ, D_MODEL), f32)
    meta_tokens = jax.random.normal(ks[1], (N_META, D_MODEL), f32)
    norm_gain = 1.0 + 0.02 * jax.random.normal(ks[2], (DEPTH, D_MODEL), f32)
    w_in = jax.random.normal(ks[3], (DEPTH, D_MODEL, IN_COLS), f32) * D_MODEL ** -0.5
    w_out = jax.random.normal(ks[4], (DEPTH, D_MIX, D_MODEL), f32) * D_MIX ** -0.5
    lambda_q1 = 0.1 * jax.random.normal(ks[5], (DEPTH, DIFF_HEAD_DIM), f32)
    lambda_k1 = 0.1 * jax.random.normal(ks[6], (DEPTH, DIFF_HEAD_DIM), f32)
    lambda_q2 = 0.1 * jax.random.normal(ks[7], (DEPTH, DIFF_HEAD_DIM), f32)
    lambda_k2 = 0.1 * jax.random.normal(ks[8], (DEPTH, DIFF_HEAD_DIM), f32)
    subln_gain = 1.0 + 0.02 * jax.random.normal(ks[9], (DEPTH, DIFF_V_DIM), f32)
    final_norm_gain = 1.0 + 0.02 * jax.random.normal(ks[10], (D_MODEL,), f32)
    return {"x": x, "meta_tokens": meta_tokens, "norm_gain": norm_gain,
            "w_in": w_in, "w_out": w_out,
            "lambda_q1": lambda_q1, "lambda_k1": lambda_k1,
            "lambda_q2": lambda_q2, "lambda_k2": lambda_k2,
            "subln_gain": subln_gain, "final_norm_gain": final_norm_gain}


def reference(x, meta_tokens, norm_gain, w_in, w_out, lambda_q1, lambda_k1,
              lambda_q2, lambda_k2, subln_gain, final_norm_gain):
    B = x.shape[0]
    meta = jnp.broadcast_to(meta_tokens[None].astype(x.dtype), (B, N_META, D_MODEL))
    h = jnp.concatenate([meta, x], axis=1)
    L = h.shape[1]
    Lp = -(-L // Q_BLOCK) * Q_BLOCK
    cos, sin = rope_tables(Lp, DIFF_HEAD_DIM)
    chunk = chunk_ids(Lp)
    for i in range(DEPTH):
        h = hybrid_layer(h, norm_gain[i], w_in[i], w_out[i],
                         lambda_q1[i], lambda_k1[i], lambda_q2[i], lambda_k2[i],
                         subln_gain[i], i, cos, sin, chunk, Lp)
    y = rmsnorm(h, final_norm_gain)
    return y[:, N_META:]
```

```python
import functools

import numpy as np
import jax
import jax.numpy as jnp
from jax import lax
from jax.experimental import pallas as pl
from jax.experimental.pallas import tpu as pltpu

LANES_V7X = 128
VMEM_LIMIT_BYTES_V7X = 56 * 1024 * 1024

N_META = 16
CHUNK = 64
ROPE_THETA = 10000.0
RMS_EPS = 1e-6
HEAD_DIM = 64
GROUP = 512
META_PAD = LANES_V7X
NEG_BIG = -1e30
LAMBDA_INIT = 0.8 - 0.6 * float(np.exp(-0.3 * 0))


def _rms(x, g):
    return x * lax.rsqrt(jnp.mean(x * x, axis=-1, keepdims=True) + RMS_EPS) * g


def _proj_kernel(x_ref, g_ref, w_ref, cos_ref, sin_ref,
                 sbq_ref, sbk_ref, sbv_ref, sbg_ref, dq_ref, dk_ref, dv_ref, dg_ref):
    u = _rms(x_ref[...], g_ref[...]).astype(jnp.bfloat16)

    def col(c):
        return jnp.dot(u, w_ref[:, c * GROUP:(c + 1) * GROUP], preferred_element_type=jnp.float32)

    def silu(g):
        return g / (1.0 + jnp.exp(-g))

    def rope(t):
        cos = jnp.concatenate([cos_ref[...]] * (GROUP // LANES_V7X), axis=1)
        sin = jnp.concatenate([sin_ref[...]] * (GROUP // LANES_V7X), axis=1)
        parts = [pltpu.roll(t[:, j * LANES_V7X:(j + 1) * LANES_V7X], LANES_V7X // 2, 1)
                 for j in range(GROUP // LANES_V7X)]
        return t * cos + jnp.concatenate(parts, axis=1) * sin

    scale = HEAD_DIM ** -0.5
    sbq_ref[...] = (col(0) * scale).astype(jnp.bfloat16)
    sbk_ref[...] = col(1).astype(jnp.bfloat16)
    sbv_ref[...] = col(2).astype(jnp.bfloat16)
    sbg_ref[...] = silu(col(3))
    dq_ref[...] = (rope(col(4)) * scale).astype(jnp.bfloat16)
    dk_ref[...] = rope(col(5)).astype(jnp.bfloat16)
    dv_ref[...] = col(6).astype(jnp.bfloat16)
    dg_ref[...] = silu(col(7))


def _project(x2d, gain, wb, cos, sin, tm):
    rows, d = x2d.shape
    n_pos_tiles = cos.shape[0] // tm
    row_spec = lambda width: pl.BlockSpec((tm, width), lambda i: (i, 0))
    tab_spec = pl.BlockSpec((tm, LANES_V7X), lambda i: (i % n_pos_tiles, 0))
    bf, f32 = jnp.bfloat16, jnp.float32
    out_dtypes = (bf, bf, bf, f32, bf, bf, bf, f32)
    return pl.pallas_call(
        _proj_kernel,
        grid=(rows // tm,),
        in_specs=[row_spec(d),
                  pl.BlockSpec((1, d), lambda i: (0, 0)),
                  pl.BlockSpec(wb.shape, lambda i: (0, 0)),
                  tab_spec, tab_spec],
        out_specs=[row_spec(GROUP)] * 8,
        out_shape=[jax.ShapeDtypeStruct((rows, GROUP), dt) for dt in out_dtypes],
        compiler_params=pltpu.CompilerParams(
            dimension_semantics=("parallel",), vmem_limit_bytes=VMEM_LIMIT_BYTES_V7X),
        name="proj",
    )(x2d, gain, wb, cos, sin)


def _sb_kernel(q_ref, k_ref, v_ref, km_ref, vm_ref, g_ref, tri_ref, o_ref, *, qb):
    i = pl.program_id(2)
    lane = lax.broadcasted_iota(jnp.int32, (1, LANES_V7X), 1)
    row = lax.broadcasted_iota(jnp.int32, (qb, qb), 0)
    colk = lax.broadcasted_iota(jnp.int32, (qb, qb), 1)
    strict = colk < row
    meta_valid = lax.broadcasted_iota(jnp.int32, (qb, META_PAD), 1) < N_META
    q = q_ref[0].astype(jnp.float32)
    nt = (((1,), (1,)), ((), ()))

    def block(qm, kblk, vblk, tri, carry, acc, mask):
        z = lax.dot_general(qm, kblk, nt, preferred_element_type=jnp.float32)
        soft = jnp.log1p(jnp.exp(-jnp.abs(z)))
        log_beta = jnp.minimum(z, 0.0) - soft
        log_keep = log_beta - z
        if mask is not None:
            log_keep = jnp.where(mask, log_keep, 0.0)
        hi = log_keep.astype(jnp.bfloat16)
        lo = (log_keep - hi.astype(jnp.float32)).astype(jnp.bfloat16)
        stick = (jnp.dot(hi, tri, preferred_element_type=jnp.float32)
                 + jnp.dot(lo, tri, preferred_element_type=jnp.float32))
        w = jnp.exp(log_beta + stick + carry)
        if mask is not None:
            w = jnp.where(mask, w, 0.0)
        acc = acc + jnp.dot(w.astype(jnp.bfloat16), vblk, preferred_element_type=jnp.float32)
        carry = carry + jnp.sum(log_keep, axis=-1, keepdims=True)
        return carry, acc

    outs = []
    for hh in range(2):
        in_head = (lane // HEAD_DIM) == hh
        qm = jnp.where(in_head, q, 0.0).astype(jnp.bfloat16)
        carry = jnp.zeros((qb, 1), jnp.float32)
        acc = jnp.zeros((qb, LANES_V7X), jnp.float32)
        start = pl.multiple_of(i * qb, qb)
        carry, acc = block(qm, k_ref[0, pl.ds(start, qb), :], v_ref[0, pl.ds(start, qb), :],
                           tri_ref[...], carry, acc, strict)

        def body(t, c):
            s = pl.multiple_of((i - 1 - t) * qb, qb)
            return block(qm, k_ref[0, pl.ds(s, qb), :], v_ref[0, pl.ds(s, qb), :],
                         tri_ref[...], c[0], c[1], None)

        carry, acc = lax.fori_loop(0, i, body, (carry, acc))
        carry, acc = block(qm, km_ref[...], vm_ref[...], tri_ref[:META_PAD, :META_PAD],
                           carry, acc, meta_valid)
        outs.append(acc)
    o = jnp.where(lane < HEAD_DIM, outs[0], outs[1])
    o_ref[0] = (o * g_ref[0]).astype(o_ref.dtype)


def _sb_attention(q, k, v, km, vm, g, tri, qb):
    b, s, width = q.shape
    pairs = width // LANES_V7X
    qspec = pl.BlockSpec((1, qb, LANES_V7X), lambda bi, p, i: (bi, i, p))
    kspec = pl.BlockSpec((1, s, LANES_V7X), lambda bi, p, i: (bi, 0, p))
    mspec = pl.BlockSpec((META_PAD, LANES_V7X), lambda bi, p, i: (0, p))
    return pl.pallas_call(
        functools.partial(_sb_kernel, qb=qb),
        grid=(b, pairs, s // qb),
        in_specs=[qspec, kspec, kspec, mspec, mspec, qspec,
                  pl.BlockSpec(tri.shape, lambda bi, p, i: (0, 0))],
        out_specs=qspec,
        out_shape=jax.ShapeDtypeStruct((b, s, width), jnp.bfloat16),
        compiler_params=pltpu.CompilerParams(
            dimension_semantics=("parallel", "parallel", "arbitrary"),
            vmem_limit_bytes=VMEM_LIMIT_BYTES_V7X),
        name="sb_attn",
    )(q, k, v, km, vm, g, tri)


def _diff_kernel(q_ref, k_ref, v_ref, km_ref, vm_ref, g_ref,
                 lq1_ref, lk1_ref, lq2_ref, lk2_ref, sub_ref, o_ref, *, qb):
    i = pl.program_id(2)
    lane = lax.broadcasted_iota(jnp.int32, (1, LANES_V7X), 1)
    row = lax.broadcasted_iota(jnp.int32, (qb, qb), 0)
    colk = lax.broadcasted_iota(jnp.int32, (qb, qb), 1)
    chunk_ok = (row // CHUNK) >= (colk // CHUNK)
    meta_valid = lax.broadcasted_iota(jnp.int32, (qb, META_PAD), 1) < N_META
    q = q_ref[0].astype(jnp.float32)
    nt = (((1,), (1,)), ((), ()))

    def block(qm, kblk, vblk, m, l, acc, mask):
        s = lax.dot_general(qm, kblk, nt, preferred_element_type=jnp.float32)
        if mask is not None:
            s = jnp.where(mask, s, NEG_BIG)
        m_new = jnp.maximum(m, jnp.max(s, axis=-1, keepdims=True))
        alpha = jnp.exp(m - m_new)
        p = jnp.exp(s - m_new)
        l = l * alpha + jnp.sum(p, axis=-1, keepdims=True)
        acc = acc * alpha + jnp.dot(p.astype(jnp.bfloat16), vblk,
                                    preferred_element_type=jnp.float32)
        return m_new, l, acc

    halves = []
    for c in range(2):
        in_half = ((lane // (HEAD_DIM // 2)) % 2) == c
        qm = jnp.where(in_half, q, 0.0).astype(jnp.bfloat16)
        m = jnp.full((qb, 1), NEG_BIG, jnp.float32)
        l = jnp.zeros((qb, 1), jnp.float32)
        acc = jnp.zeros((qb, LANES_V7X), jnp.float32)
        start = pl.multiple_of(i * qb, qb)
        m, l, acc = block(qm, k_ref[0, pl.ds(start, qb), :], v_ref[0, pl.ds(start, qb), :],
                          m, l, acc, chunk_ok)

        def body(t, st):
            s0 = pl.multiple_of(t * qb, qb)
            return block(qm, k_ref[0, pl.ds(s0, qb), :], v_ref[0, pl.ds(s0, qb), :],
                         st[0], st[1], st[2], None)

        m, l, acc = lax.fori_loop(0, i, body, (m, l, acc))
        m, l, acc = block(qm, km_ref[...], vm_ref[...], m, l, acc, meta_valid)
        halves.append(acc / l)

    lam = (jnp.exp(jnp.sum(lq1_ref[...] * lk1_ref[...], axis=-1, keepdims=True))
           - jnp.exp(jnp.sum(lq2_ref[...] * lk2_ref[...], axis=-1, keepdims=True))
           + LAMBDA_INIT)
    o = halves[0] - lam * halves[1]
    o = _rms(o, sub_ref[...]) * (1.0 - LAMBDA_INIT)
    o_ref[0] = (o * g_ref[0]).astype(o_ref.dtype)


def _diff_attention(q, k, v, km, vm, g, lq1, lk1, lq2, lk2, sub, qb):
    b, s, width = q.shape
    heads = width // LANES_V7X
    qspec = pl.BlockSpec((1, qb, LANES_V7X), lambda bi, h, i: (bi, i, h))
    kspec = pl.BlockSpec((1, s, LANES_V7X), lambda bi, h, i: (bi, 0, h))
    mspec = pl.BlockSpec((META_PAD, LANES_V7X), lambda bi, h, i: (0, h))
    full = lambda a: pl.BlockSpec(a.shape, lambda bi, h, i: (0, 0))
    return pl.pallas_call(
        functools.partial(_diff_kernel, qb=qb),
        grid=(b, heads, s // qb),
        in_specs=[qspec, kspec, kspec, mspec, mspec, qspec,
                  full(lq1), full(lk1), full(lq2), full(lk2), full(sub)],
        out_specs=qspec,
        out_shape=jax.ShapeDtypeStruct((b, s, width), jnp.bfloat16),
        compiler_params=pltpu.CompilerParams(
            dimension_semantics=("parallel", "parallel", "arbitrary"),
            vmem_limit_bytes=VMEM_LIMIT_BYTES_V7X),
        name="diff_attn",
    )(q, k, v, km, vm, g, lq1, lk1, lq2, lk2, sub)


def _out_kernel(a_ref, b_ref, w_ref, x_ref, g_ref, o_ref):
    half = a_ref.shape[1]
    h = (x_ref[...]
         + jnp.dot(a_ref[...], w_ref[:half, :], preferred_element_type=jnp.float32)
         + jnp.dot(b_ref[...], w_ref[half:, :], preferred_element_type=jnp.float32))
    o_ref[...] = _rms(h, g_ref[...])


def _out_project(mix_a, mix_b, wo, x2d, gain, tm):
    rows, d = x2d.shape
    return pl.pallas_call(
        _out_kernel,
        grid=(rows // tm,),
        in_specs=[pl.BlockSpec((tm, mix_a.shape[1]), lambda i: (i, 0)),
                  pl.BlockSpec((tm, mix_b.shape[1]), lambda i: (i, 0)),
                  pl.BlockSpec(wo.shape, lambda i: (0, 0)),
                  pl.BlockSpec((tm, d), lambda i: (i, 0)),
                  pl.BlockSpec((1, d), lambda i: (0, 0))],
        out_specs=pl.BlockSpec((tm, d), lambda i: (i, 0)),
        out_shape=jax.ShapeDtypeStruct((rows, d), jnp.float32),
        compiler_params=pltpu.CompilerParams(
            dimension_semantics=("parallel",), vmem_limit_bytes=VMEM_LIMIT_BYTES_V7X),
        name="out_proj",
    )(mix_a, mix_b, wo, x2d, gain)


def _rope_column_order():
    h = HEAD_DIM // 2
    one = np.concatenate([np.arange(0, h), np.arange(2 * h, 3 * h),
                          np.arange(h, 2 * h), np.arange(3 * h, 4 * h)])
    return np.concatenate([one + LANES_V7X * j for j in range(GROUP // LANES_V7X)])


def _rope_tables(length):
    inv = 1.0 / (ROPE_THETA ** (jnp.arange(0, HEAD_DIM, 2, dtype=jnp.float32) / HEAD_DIM))
    ang = jnp.arange(length, dtype=jnp.float32)[:, None] * inv[None, :]
    cos, sin = jnp.cos(ang), jnp.sin(ang)
    return (jnp.concatenate([cos, cos, cos, cos], axis=1),
            jnp.concatenate([-sin, -sin, sin, sin], axis=1))


def _forward(x, meta_tokens, norm_gain, w_in, w_out, lambda_q1, lambda_k1, lambda_q2, lambda_k2,
             subln_gain, final_norm_gain, *, qb, tm):
    b, s, d = x.shape
    order = _rope_column_order()
    w = w_in[0]
    perm = np.concatenate([np.arange(4 * GROUP), 4 * GROUP + order, 5 * GROUP + order,
                           np.arange(6 * GROUP, 8 * GROUP)])
    wb = w[:, perm].astype(jnp.bfloat16)
    wo = w_out[0].astype(jnp.bfloat16)
    cos, sin = _rope_tables(N_META + s)
    gain = norm_gain[0][None, :]
    x2d = x.reshape(b * s, d)

    px = _project(x2d, gain, wb, cos[N_META:], sin[N_META:], tm)
    pm = _project(meta_tokens, gain, wb, cos[:N_META], sin[:N_META], N_META)
    sbq, sbk, sbv, sbg, dq, dk, dv, dg = [a.reshape(b, s, GROUP) for a in px]
    pad = lambda a: jnp.pad(a, ((0, META_PAD - N_META), (0, 0)))
    sbk_m, sbv_m, dk_m, dv_m = pad(pm[1]), pad(pm[2]), pad(pm[5]), pad(pm[6])

    idx = np.arange(qb)
    tri = jnp.asarray(idx[:, None] > idx[None, :], dtype=jnp.bfloat16)

    mix_sb = _sb_attention(sbq, sbk, sbv, sbk_m, sbv_m, sbg, tri, qb)
    mix_df = _diff_attention(dq, dk, dv, dk_m, dv_m, dg,
                             lambda_q1, lambda_k1, lambda_q2, lambda_k2, subln_gain, qb)
    y = _out_project(mix_sb.reshape(b * s, GROUP), mix_df.reshape(b * s, GROUP), wo, x2d,
                     final_norm_gain[None, :], tm)
    return y.reshape(b, s, d)


def kernel(x, meta_tokens, norm_gain, w_in, w_out, lambda_q1, lambda_k1, lambda_q2, lambda_k2,
           subln_gain, final_norm_gain):
    return _forward(x, meta_tokens, norm_gain, w_in, w_out, lambda_q1, lambda_k1, lambda_q2,
                    lambda_k2, subln_gain, final_norm_gain, qb=256, tm=512)
```

```python
import functools

import numpy as np
import jax
import jax.numpy as jnp
from jax import lax
from jax.experimental import pallas as pl
from jax.experimental.pallas import tpu as pltpu

LANES_V7X = 128
VMEM_LIMIT_BYTES_V7X = 56 * 1024 * 1024

N_META = 16
CHUNK = 64
ROPE_THETA = 10000.0
RMS_EPS = 1e-6
HEAD_DIM = 64
GROUP = 512
META_PAD = LANES_V7X
NEG_BIG = -1e30
LOG2_E = 1.4426950408889634
LAMBDA_INIT = 0.8 - 0.6 * float(np.exp(-0.3 * 0))


def _rms(x, g):
    return x * lax.rsqrt(jnp.mean(x * x, axis=-1, keepdims=True) + RMS_EPS) * g


def _proj_kernel(x_ref, g_ref, w_ref, cos_ref, sin_ref,
                 sbq_ref, sbk_ref, sbv_ref, sbg_ref, dq_ref, dk_ref, dv_ref, dg_ref):
    u = _rms(x_ref[...], g_ref[...]).astype(jnp.bfloat16)

    def col(c):
        return jnp.dot(u, w_ref[:, c * GROUP:(c + 1) * GROUP], preferred_element_type=jnp.float32)

    def silu(g):
        return g / (1.0 + jnp.exp(-g))

    def rope(t):
        cos = jnp.concatenate([cos_ref[...]] * (GROUP // LANES_V7X), axis=1)
        sin = jnp.concatenate([sin_ref[...]] * (GROUP // LANES_V7X), axis=1)
        parts = [pltpu.roll(t[:, j * LANES_V7X:(j + 1) * LANES_V7X], LANES_V7X // 2, 1)
                 for j in range(GROUP // LANES_V7X)]
        return t * cos + jnp.concatenate(parts, axis=1) * sin

    scale = HEAD_DIM ** -0.5
    sbq_ref[...] = (col(0) * scale).astype(jnp.bfloat16)
    sbk_ref[...] = col(1).astype(jnp.bfloat16)
    sbv_ref[...] = col(2).astype(jnp.bfloat16)
    sbg_ref[...] = silu(col(3))
    dq_ref[...] = (rope(col(4)) * scale).astype(jnp.bfloat16)
    dk_ref[...] = rope(col(5)).astype(jnp.bfloat16)
    dv_ref[...] = col(6).astype(jnp.bfloat16)
    dg_ref[...] = silu(col(7))


def _project(x2d, gain, wb, cos, sin, tm):
    rows, d = x2d.shape
    n_pos_tiles = cos.shape[0] // tm
    row_spec = lambda width: pl.BlockSpec((tm, width), lambda i: (i, 0))
    tab_spec = pl.BlockSpec((tm, LANES_V7X), lambda i: (i % n_pos_tiles, 0))
    bf, f32 = jnp.bfloat16, jnp.float32
    out_dtypes = (bf, bf, bf, f32, bf, bf, bf, f32)
    return pl.pallas_call(
        _proj_kernel,
        grid=(rows // tm,),
        in_specs=[row_spec(d),
                  pl.BlockSpec((1, d), lambda i: (0, 0)),
                  pl.BlockSpec(wb.shape, lambda i: (0, 0)),
                  tab_spec, tab_spec],
        out_specs=[row_spec(GROUP)] * 8,
        out_shape=[jax.ShapeDtypeStruct((rows, GROUP), dt) for dt in out_dtypes],
        compiler_params=pltpu.CompilerParams(
            dimension_semantics=("parallel",), vmem_limit_bytes=VMEM_LIMIT_BYTES_V7X),
        name="proj",
    )(x2d, gain, wb, cos, sin)


def _lane_tile(ref, p):
    return ref[0, :, p * LANES_V7X:(p + 1) * LANES_V7X]


def _widen(c, width):
    return c if width == LANES_V7X else jnp.concatenate([c] * (width // LANES_V7X), axis=1)


def _sb_kernel(q_ref, k_ref, v_ref, km_ref, vm_ref, g_ref, tri_ref, o_ref, acc_ref, car_ref,
               *, qb, tiles):
    i = pl.program_id(2)
    lane = lax.broadcasted_iota(jnp.int32, (1, LANES_V7X), 1)
    row = lax.broadcasted_iota(jnp.int32, (qb, qb), 0)
    colk = lax.broadcasted_iota(jnp.int32, (qb, qb), 1)
    strict = colk < row
    meta_valid = lax.broadcasted_iota(jnp.int32, (qb, META_PAD), 1) < N_META
    nt = (((1,), (1,)), ((), ()))
    chains = [(p, hh) for p in range(tiles) for hh in range(2)]
    qms = []
    for p, hh in chains:
        q = _lane_tile(q_ref, p).astype(jnp.float32)
        qms.append(jnp.where((lane // HEAD_DIM) == hh, q, 0.0).astype(jnp.bfloat16))

    def step(kblk, vblk, tri, mask, first):
        zs = [lax.dot_general(qms[n], kblk(p), nt, preferred_element_type=jnp.float32)
              for n, (p, hh) in enumerate(chains)]
        log_betas, sticks, tots = [], [], []
        for z in zs:
            log_beta = jnp.minimum(z, 0.0) - jnp.log(1.0 + jnp.exp2(jnp.abs(z) * (-LOG2_E)))
            log_keep = log_beta - z
            if mask is not None:
                log_keep = jnp.where(mask, log_keep, 0.0)
            hi = log_keep.astype(jnp.bfloat16)
            lo = (log_keep - hi.astype(jnp.float32)).astype(jnp.bfloat16)
            sticks.append(jnp.dot(jnp.concatenate([hi, lo], axis=1), tri,
                                  preferred_element_type=jnp.float32))
            log_betas.append(log_beta)
            tots.append(jnp.broadcast_to(jnp.sum(log_keep, axis=-1, keepdims=True),
                                         (qb, LANES_V7X)))
        for n, (p, hh) in enumerate(chains):
            t = log_betas[n] + sticks[n]
            if not first:
                t = t + _widen(car_ref[n], t.shape[1])
            w = jnp.exp(t)
            if mask is not None:
                w = jnp.where(mask, w, 0.0)
            pv = jnp.dot(w.astype(jnp.bfloat16), vblk(p), preferred_element_type=jnp.float32)
            if first:
                acc_ref[n] = pv
                car_ref[n] = tots[n]
            else:
                acc_ref[n] += pv
                car_ref[n] += tots[n]

    start = pl.multiple_of(i * qb, qb)
    step(lambda p: k_ref[0, pl.ds(start, qb), p * LANES_V7X:(p + 1) * LANES_V7X],
         lambda p: v_ref[0, pl.ds(start, qb), p * LANES_V7X:(p + 1) * LANES_V7X],
         tri_ref[...], strict, True)

    def body(t, c):
        s = pl.multiple_of((i - 1 - t) * qb, qb)
        step(lambda p: k_ref[0, pl.ds(s, qb), p * LANES_V7X:(p + 1) * LANES_V7X],
             lambda p: v_ref[0, pl.ds(s, qb), p * LANES_V7X:(p + 1) * LANES_V7X],
             tri_ref[...], None, False)
        return c

    lax.fori_loop(0, i, body, 0)
    step(lambda p: km_ref[:, p * LANES_V7X:(p + 1) * LANES_V7X],
         lambda p: vm_ref[:, p * LANES_V7X:(p + 1) * LANES_V7X],
         jnp.concatenate([tri_ref[:META_PAD, :META_PAD], tri_ref[qb:qb + META_PAD, :META_PAD]],
                         axis=0), meta_valid, False)
    for p in range(tiles):
        o = jnp.where(lane < HEAD_DIM, acc_ref[2 * p], acc_ref[2 * p + 1])
        o_ref[0, :, p * LANES_V7X:(p + 1) * LANES_V7X] = (o * _lane_tile(g_ref, p)).astype(o_ref.dtype)


def _sb_attention(q, k, v, km, vm, g, tri, qb, tiles):
    b, s, width = q.shape
    tw = tiles * LANES_V7X
    qspec = pl.BlockSpec((1, qb, tw), lambda bi, p, i: (bi, i, p))
    kspec = pl.BlockSpec((1, s, tw), lambda bi, p, i: (bi, 0, p))
    mspec = pl.BlockSpec((META_PAD, tw), lambda bi, p, i: (0, p))
    return pl.pallas_call(
        functools.partial(_sb_kernel, qb=qb, tiles=tiles),
        grid=(b, width // tw, s // qb),
        in_specs=[qspec, kspec, kspec, mspec, mspec, qspec,
                  pl.BlockSpec(tri.shape, lambda bi, p, i: (0, 0))],
        out_specs=qspec,
        out_shape=jax.ShapeDtypeStruct((b, s, width), jnp.bfloat16),
        scratch_shapes=[pltpu.VMEM((2 * tiles, qb, LANES_V7X), jnp.float32),
                        pltpu.VMEM((2 * tiles, qb, LANES_V7X), jnp.float32)],
        compiler_params=pltpu.CompilerParams(
            dimension_semantics=("parallel", "parallel", "arbitrary"),
            vmem_limit_bytes=VMEM_LIMIT_BYTES_V7X),
        name="sb_attn",
    )(q, k, v, km, vm, g, tri)


def _diff_kernel(q_ref, k_ref, v_ref, km_ref, vm_ref, g_ref,
                 lq1_ref, lk1_ref, lq2_ref, lk2_ref, sub_ref, o_ref, acc_ref, m_ref, l_ref,
                 *, qb, tiles):
    i = pl.program_id(2)
    lane = lax.broadcasted_iota(jnp.int32, (1, LANES_V7X), 1)
    row = lax.broadcasted_iota(jnp.int32, (qb, qb), 0)
    colk = lax.broadcasted_iota(jnp.int32, (qb, qb), 1)
    chunk_ok = (row // CHUNK) >= (colk // CHUNK)
    meta_valid = lax.broadcasted_iota(jnp.int32, (qb, META_PAD), 1) < N_META
    nt = (((1,), (1,)), ((), ()))
    chains = [(p, c) for p in range(tiles) for c in range(2)]
    qms = []
    for p, c in chains:
        q = _lane_tile(q_ref, p).astype(jnp.float32)
        in_half = ((lane // (HEAD_DIM // 2)) % 2) == c
        qms.append(jnp.where(in_half, q, 0.0).astype(jnp.bfloat16))

    def step(kblk, vblk, mask, first):
        ss = [lax.dot_general(qms[n], kblk(p), nt, preferred_element_type=jnp.float32)
              for n, (p, c) in enumerate(chains)]
        for n, (p, c) in enumerate(chains):
            s = ss[n]
            if mask is not None:
                s = jnp.where(mask, s, NEG_BIG)
            m_blk = jnp.broadcast_to(jnp.max(s, axis=-1, keepdims=True), (qb, LANES_V7X))
            m_new = m_blk if first else jnp.maximum(m_ref[n], m_blk)
            pr = jnp.exp(s - _widen(m_new, s.shape[1]))
            l_blk = jnp.broadcast_to(jnp.sum(pr, axis=-1, keepdims=True), (qb, LANES_V7X))
            pv = jnp.dot(pr.astype(jnp.bfloat16), vblk(p), preferred_element_type=jnp.float32)
            if first:
                l_ref[n] = l_blk
                acc_ref[n] = pv
            else:
                alpha = jnp.exp(m_ref[n] - m_new)
                l_ref[n] = l_ref[n] * alpha + l_blk
                acc_ref[n] = acc_ref[n] * alpha + pv
            m_ref[n] = m_new

    start = pl.multiple_of(i * qb, qb)
    step(lambda p: k_ref[0, pl.ds(start, qb), p * LANES_V7X:(p + 1) * LANES_V7X],
         lambda p: v_ref[0, pl.ds(start, qb), p * LANES_V7X:(p + 1) * LANES_V7X],
         chunk_ok, True)

    def body(t, c):
        s0 = pl.multiple_of(t * qb, qb)
        step(lambda p: k_ref[0, pl.ds(s0, qb), p * LANES_V7X:(p + 1) * LANES_V7X],
             lambda p: v_ref[0, pl.ds(s0, qb), p * LANES_V7X:(p + 1) * LANES_V7X],
             None, False)
        return c

    lax.fori_loop(0, i, body, 0)
    step(lambda p: km_ref[:, p * LANES_V7X:(p + 1) * LANES_V7X],
         lambda p: vm_ref[:, p * LANES_V7X:(p + 1) * LANES_V7X],
         meta_valid, False)

    lam = (jnp.exp(jnp.sum(lq1_ref[...] * lk1_ref[...], axis=-1, keepdims=True))
           - jnp.exp(jnp.sum(lq2_ref[...] * lk2_ref[...], axis=-1, keepdims=True))
           + LAMBDA_INIT)
    for p in range(tiles):
        o = acc_ref[2 * p] / l_ref[2 * p] - lam * (acc_ref[2 * p + 1] / l_ref[2 * p + 1])
        o = _rms(o, sub_ref[...]) * (1.0 - LAMBDA_INIT)
        o_ref[0, :, p * LANES_V7X:(p + 1) * LANES_V7X] = (o * _lane_tile(g_ref, p)).astype(o_ref.dtype)


def _diff_attention(q, k, v, km, vm, g, lq1, lk1, lq2, lk2, sub, qb, tiles):
    b, s, width = q.shape
    tw = tiles * LANES_V7X
    qspec = pl.BlockSpec((1, qb, tw), lambda bi, h, i: (bi, i, h))
    kspec = pl.BlockSpec((1, s, tw), lambda bi, h, i: (bi, 0, h))
    mspec = pl.BlockSpec((META_PAD, tw), lambda bi, h, i: (0, h))
    full = lambda a: pl.BlockSpec(a.shape, lambda bi, h, i: (0, 0))
    stat = pltpu.VMEM((2 * tiles, qb, LANES_V7X), jnp.float32)
    return pl.pallas_call(
        functools.partial(_diff_kernel, qb=qb, tiles=tiles),
        grid=(b, width // tw, s // qb),
        in_specs=[qspec, kspec, kspec, mspec, mspec, qspec,
                  full(lq1), full(lk1), full(lq2), full(lk2), full(sub)],
        out_specs=qspec,
        out_shape=jax.ShapeDtypeStruct((b, s, width), jnp.bfloat16),
        scratch_shapes=[stat, stat, stat],
        compiler_params=pltpu.CompilerParams(
            dimension_semantics=("parallel", "parallel", "arbitrary"),
            vmem_limit_bytes=VMEM_LIMIT_BYTES_V7X),
        name="diff_attn",
    )(q, k, v, km, vm, g, lq1, lk1, lq2, lk2, sub)


def _out_kernel(a_ref, b_ref, w_ref, x_ref, g_ref, o_ref):
    half = a_ref.shape[1]
    h = (x_ref[...]
         + jnp.dot(a_ref[...], w_ref[:half, :], preferred_element_type=jnp.float32)
         + jnp.dot(b_ref[...], w_ref[half:, :], preferred_element_type=jnp.float32))
    o_ref[...] = _rms(h, g_ref[...])


def _out_project(mix_a, mix_b, wo, x2d, gain, tm):
    rows, d = x2d.shape
    return pl.pallas_call(
        _out_kernel,
        grid=(rows // tm,),
        in_specs=[pl.BlockSpec((tm, mix_a.shape[1]), lambda i: (i, 0)),
                  pl.BlockSpec((tm, mix_b.shape[1]), lambda i: (i, 0)),
                  pl.BlockSpec(wo.shape, lambda i: (0, 0)),
                  pl.BlockSpec((tm, d), lambda i: (i, 0)),
                  pl.BlockSpec((1, d), lambda i: (0, 0))],
        out_specs=pl.BlockSpec((tm, d), lambda i: (i, 0)),
        out_shape=jax.ShapeDtypeStruct((rows, d), jnp.float32),
        compiler_params=pltpu.CompilerParams(
            dimension_semantics=("parallel",), vmem_limit_bytes=VMEM_LIMIT_BYTES_V7X),
        name="out_proj",
    )(mix_a, mix_b, wo, x2d, gain)


def _rope_column_order():
    h = HEAD_DIM // 2
    one = np.concatenate([np.arange(0, h), np.arange(2 * h, 3 * h),
                          np.arange(h, 2 * h), np.arange(3 * h, 4 * h)])
    return np.concatenate([one + LANES_V7X * j for j in range(GROUP // LANES_V7X)])


def _rope_tables(length):
    inv = 1.0 / (ROPE_THETA ** (jnp.arange(0, HEAD_DIM, 2, dtype=jnp.float32) / HEAD_DIM))
    ang = jnp.arange(length, dtype=jnp.float32)[:, None] * inv[None, :]
    cos, sin = jnp.cos(ang), jnp.sin(ang)
    return (jnp.concatenate([cos, cos, cos, cos], axis=1),
            jnp.concatenate([-sin, -sin, sin, sin], axis=1))


def _forward(x, meta_tokens, norm_gain, w_in, w_out, lambda_q1, lambda_k1, lambda_q2, lambda_k2,
             subln_gain, final_norm_gain, *, qb, tm, tiles):
    b, s, d = x.shape
    order = _rope_column_order()
    w = w_in[0]
    perm = np.concatenate([np.arange(4 * GROUP), 4 * GROUP + order, 5 * GROUP + order,
                           np.arange(6 * GROUP, 8 * GROUP)])
    wb = w[:, perm].astype(jnp.bfloat16)
    wo = w_out[0].astype(jnp.bfloat16)
    cos, sin = _rope_tables(N_META + s)
    gain = norm_gain[0][None, :]
    x2d = x.reshape(b * s, d)

    px = _project(x2d, gain, wb, cos[N_META:], sin[N_META:], tm)
    pm = _project(meta_tokens, gain, wb, cos[:N_META], sin[:N_META], N_META)
    sbq, sbk, sbv, sbg, dq, dk, dv, dg = [a.reshape(b, s, GROUP) for a in px]
    pad = lambda a: jnp.pad(a, ((0, META_PAD - N_META), (0, 0)))
    sbk_m, sbv_m, dk_m, dv_m = pad(pm[1]), pad(pm[2]), pad(pm[5]), pad(pm[6])

    idx = np.arange(qb)
    tri = np.asarray(idx[:, None] > idx[None, :], dtype=np.float32)
    tri = jnp.asarray(np.concatenate([tri, tri], axis=0), dtype=jnp.bfloat16)

    mix_sb = _sb_attention(sbq, sbk, sbv, sbk_m, sbv_m, sbg, tri, qb, tiles)
    mix_df = _diff_attention(dq, dk, dv, dk_m, dv_m, dg,
                             lambda_q1, lambda_k1, lambda_q2, lambda_k2, subln_gain, qb, tiles)
    y = _out_project(mix_sb.reshape(b * s, GROUP), mix_df.reshape(b * s, GROUP), wo, x2d,
                     final_norm_gain[None, :], tm)
    return y.reshape(b, s, d)


def kernel(x, meta_tokens, norm_gain, w_in, w_out, lambda_q1, lambda_k1, lambda_q2, lambda_k2,
           subln_gain, final_norm_gain):
    return _forward(x, meta_tokens, norm_gain, w_in, w_out, lambda_q1, lambda_k1, lambda_q2,
                    lambda_k2, subln_gain, final_norm_gain, qb=256, tm=512, tiles=4)
```

```python
import functools

import numpy as np
import jax
import jax.numpy as jnp
from jax import lax
from jax.experimental import pallas as pl
from jax.experimental.pallas import tpu as pltpu

LANES_V7X = 128
SUBLANES_BF16_V7X = 16
VMEM_LIMIT_BYTES_V7X = 56 * 1024 * 1024

N_META = 16
CHUNK = 64
ROPE_THETA = 10000.0
RMS_EPS = 1e-6
HEAD_DIM = 64
GROUP = 512
KEY_BLOCK = 256
NEG_BIG = -1e30
LOG2_E = 1.4426950408889634
LAMBDA_INIT = 0.8 - 0.6 * float(np.exp(-0.3 * 0))


def _rms(x, g):
    return x * lax.rsqrt(jnp.mean(x * x, axis=-1, keepdims=True) + RMS_EPS) * g


def _proj_kernel(x_ref, g_ref, w_ref, wt_ref, cos_ref, sin_ref, *out_refs, meta):
    u = _rms(x_ref[...], g_ref[...]).astype(jnp.bfloat16)
    nt = (((1,), (1,)), ((), ()))

    def col(c):
        return jnp.dot(u, w_ref[:, c * GROUP:(c + 1) * GROUP], preferred_element_type=jnp.float32)

    def col_t(c):
        return lax.dot_general(wt_ref[c * GROUP:(c + 1) * GROUP, :], u, nt,
                               preferred_element_type=jnp.float32).astype(jnp.bfloat16)

    def silu(g):
        return g / (1.0 + jnp.exp(-g))

    def rope(t):
        cos = jnp.concatenate([cos_ref[...]] * (GROUP // LANES_V7X), axis=1)
        sin = jnp.concatenate([sin_ref[...]] * (GROUP // LANES_V7X), axis=1)
        parts = [pltpu.roll(t[:, j * LANES_V7X:(j + 1) * LANES_V7X], LANES_V7X // 2, 1)
                 for j in range(GROUP // LANES_V7X)]
        return t * cos + jnp.concatenate(parts, axis=1) * sin

    scale = HEAD_DIM ** -0.5
    if meta:
        sbk_ref, sbv_ref, dk_ref, dv_ref = out_refs
        sbk_ref[...] = col(1).astype(jnp.bfloat16)
        sbv_ref[...] = col(2).astype(jnp.bfloat16)
        dk_ref[...] = rope(col(5)).astype(jnp.bfloat16)
        dv_ref[...] = col(6).astype(jnp.bfloat16)
        return
    sbq_ref, sbk_ref, sbvt_ref, sbg_ref, dq_ref, dk_ref, dvt_ref, dg_ref = out_refs
    sbq_ref[...] = (col(0) * scale).astype(jnp.bfloat16)
    sbk_ref[...] = col(1).astype(jnp.bfloat16)
    sbg_ref[...] = silu(col(3))
    dq_ref[...] = (rope(col(4)) * scale).astype(jnp.bfloat16)
    dk_ref[...] = rope(col(5)).astype(jnp.bfloat16)
    dg_ref[...] = silu(col(7))
    for c, ref in ((0, sbvt_ref), (1, dvt_ref)):
        vt = col_t(c)
        for j in range(ref.shape[0]):
            ref[j] = vt[:, j * KEY_BLOCK:(j + 1) * KEY_BLOCK]


def _project(x2d, gain, wb, wt, cos, sin, tm, meta):
    rows, d = x2d.shape
    n_pos_tiles = cos.shape[0] // tm
    row_spec = lambda width: pl.BlockSpec((tm, width), lambda i: (i, 0))
    tab_spec = pl.BlockSpec((tm, LANES_V7X), lambda i: (i % n_pos_tiles, 0))
    bf, f32 = jnp.bfloat16, jnp.float32
    row_out = lambda dt: (row_spec(GROUP), jax.ShapeDtypeStruct((rows, GROUP), dt))
    if meta:
        outs = [row_out(bf)] * 4
    else:
        vt_out = (pl.BlockSpec((tm // KEY_BLOCK, GROUP, KEY_BLOCK), lambda i: (i, 0, 0)),
                  jax.ShapeDtypeStruct((rows // KEY_BLOCK, GROUP, KEY_BLOCK), bf))
        outs = [row_out(bf), row_out(bf), vt_out, row_out(f32),
                row_out(bf), row_out(bf), vt_out, row_out(f32)]
    return pl.pallas_call(
        functools.partial(_proj_kernel, meta=meta),
        grid=(rows // tm,),
        in_specs=[row_spec(d),
                  pl.BlockSpec((1, d), lambda i: (0, 0)),
                  pl.BlockSpec(wb.shape, lambda i: (0, 0)),
                  pl.BlockSpec(wt.shape, lambda i: (0, 0)),
                  tab_spec, tab_spec],
        out_specs=[o[0] for o in outs],
        out_shape=[o[1] for o in outs],
        compiler_params=pltpu.CompilerParams(
            dimension_semantics=("parallel",), vmem_limit_bytes=VMEM_LIMIT_BYTES_V7X),
        name="proj_meta" if meta else "proj",
    )(x2d, gain, wb, wt, cos, sin)


_NT = (((1,), (1,)), ((), ()))


def _masked_queries(q_ref, p, keep):
    q = q_ref[0, :, p * LANES_V7X:(p + 1) * LANES_V7X].astype(jnp.float32)
    return jnp.where(keep, q, 0.0).astype(jnp.bfloat16)


def _pad_rows(a, rows):
    return jnp.concatenate([a, jnp.zeros((rows - a.shape[0], a.shape[1]), a.dtype)], axis=0)


def _sb_kernel(q_ref, k_ref, vt_ref, km_ref, vmt_ref, g_ref, tri_ref, trim_ref, o_ref,
               acc_ref, car_ref, z_ref, *, qb, tiles):
    i = pl.program_id(2)
    lane = lax.broadcasted_iota(jnp.int32, (1, LANES_V7X), 1)
    key = lax.broadcasted_iota(jnp.int32, (KEY_BLOCK, qb), 0)
    qry = lax.broadcasted_iota(jnp.int32, (KEY_BLOCK, qb), 1)
    strict = key < qry
    chains = [(p, hh) for p in range(tiles) for hh in range(2)]
    qms = [_masked_queries(q_ref, p, (lane // HEAD_DIM) == hh) for p, hh in chains]

    def log_sigmoids(z, mask):
        log_beta = jnp.minimum(z, 0.0) - jnp.log(1.0 + jnp.exp2(jnp.abs(z) * (-LOG2_E)))
        log_keep = log_beta - z
        if mask is not None:
            log_keep = jnp.where(mask, log_keep, 0.0)
        hi = log_keep.astype(jnp.bfloat16)
        lo = (log_keep - hi.astype(jnp.float32)).astype(jnp.bfloat16)
        return log_beta, hi, lo

    def scores(n, p, start):
        kblk = k_ref[0, pl.ds(start, KEY_BLOCK), p * LANES_V7X:(p + 1) * LANES_V7X]
        z_ref[n] = lax.dot_general(kblk, qms[n], _NT, preferred_element_type=jnp.float32)

    def step(next_start, vtblk, mask, first):
        log_betas, sums = [], []
        for n, (p, hh) in enumerate(chains):
            log_beta, hi, lo = log_sigmoids(z_ref[n], mask)
            sums.append(jnp.dot(tri_ref[...], jnp.concatenate([hi, lo], axis=0),
                                preferred_element_type=jnp.float32))
            log_betas.append(log_beta)
            scores(n, p, next_start)
        for n, (p, hh) in enumerate(chains):
            w = jnp.exp(log_betas[n] + sums[n][:KEY_BLOCK])
            tot = sums[n][KEY_BLOCK:KEY_BLOCK + 1]
            if mask is not None:
                w = jnp.where(mask, w, 0.0)
            pv = jnp.dot(vtblk(p, hh), w.astype(jnp.bfloat16), preferred_element_type=jnp.float32)
            if first:
                acc_ref[n] = pv
                car_ref[n] = tot
            else:
                acc_ref[n] += pv * jnp.exp(car_ref[n])
                car_ref[n] += tot

    def vt_at(j):
        return lambda p, hh: vt_ref[0, j, pl.ds(p * LANES_V7X + hh * HEAD_DIM, HEAD_DIM), :]

    def block_start(j):
        return pl.multiple_of(jnp.maximum(j, 0) * KEY_BLOCK, KEY_BLOCK)

    for n, (p, hh) in enumerate(chains):
        scores(n, p, block_start(i))
    step(block_start(i - 1), vt_at(i), strict, True)

    def body(t, c):
        j = i - 1 - t
        step(block_start(j - 1), vt_at(j), None, False)
        return c

    lax.fori_loop(0, i, body, 0)

    zs = [lax.dot_general(km_ref[:, p * LANES_V7X:(p + 1) * LANES_V7X], qms[n], _NT,
                          preferred_element_type=jnp.float32)
          for n, (p, hh) in enumerate(chains)]
    log_betas, sticks = [], []
    for z in zs:
        log_beta, hi, lo = log_sigmoids(z, None)
        sticks.append(jnp.dot(trim_ref[...],
                              _pad_rows(jnp.concatenate([hi, lo], axis=0), LANES_V7X),
                              preferred_element_type=jnp.float32))
        log_betas.append(log_beta)
    ws = [_pad_rows(jnp.exp(log_betas[n] + sticks[n] + car_ref[n]).astype(jnp.bfloat16),
                    LANES_V7X) for n in range(len(chains))]
    for n, (p, hh) in enumerate(chains):
        acc_ref[n] += jnp.dot(vmt_ref[pl.ds(p * LANES_V7X + hh * HEAD_DIM, HEAD_DIM), :], ws[n],
                              preferred_element_type=jnp.float32)
    for p in range(tiles):
        o = jnp.concatenate([acc_ref[2 * p], acc_ref[2 * p + 1]], axis=0).T
        o_ref[0, :, p * LANES_V7X:(p + 1) * LANES_V7X] = (
            o * g_ref[0, :, p * LANES_V7X:(p + 1) * LANES_V7X]).astype(o_ref.dtype)


def _attn_specs(b, s, width, qb, tiles):
    tw = tiles * LANES_V7X
    qspec = pl.BlockSpec((1, qb, tw), lambda bi, p, i: (bi, i, p))
    kspec = pl.BlockSpec((1, s, tw), lambda bi, p, i: (bi, 0, p))
    vtspec = pl.BlockSpec((1, s // KEY_BLOCK, tw, KEY_BLOCK), lambda bi, p, i: (bi, 0, p, 0))
    kmspec = pl.BlockSpec((N_META, tw), lambda bi, p, i: (0, p))
    vmtspec = pl.BlockSpec((tw, LANES_V7X), lambda bi, p, i: (p, 0))
    grid = (b, width // tw, s // qb)
    return grid, qspec, kspec, vtspec, kmspec, vmtspec


def _sb_attention(q, k, vt, km, vmt, g, tri, trim, qb, tiles):
    b, s, width = q.shape
    grid, qspec, kspec, vtspec, kmspec, vmtspec = _attn_specs(b, s, width, qb, tiles)
    full = lambda a: pl.BlockSpec(a.shape, lambda bi, p, i: (0,) * a.ndim)
    return pl.pallas_call(
        functools.partial(_sb_kernel, qb=qb, tiles=tiles),
        grid=grid,
        in_specs=[qspec, kspec, vtspec, kmspec, vmtspec, qspec, full(tri), full(trim)],
        out_specs=qspec,
        out_shape=jax.ShapeDtypeStruct((b, s, width), jnp.bfloat16),
        scratch_shapes=[pltpu.VMEM((2 * tiles, HEAD_DIM, qb), jnp.float32),
                        pltpu.VMEM((2 * tiles, 1, qb), jnp.float32),
                        pltpu.VMEM((2 * tiles, KEY_BLOCK, qb), jnp.float32)],
        compiler_params=pltpu.CompilerParams(
            dimension_semantics=("parallel", "parallel", "arbitrary"),
            vmem_limit_bytes=VMEM_LIMIT_BYTES_V7X),
        name="sb_attn",
    )(q, k, vt, km, vmt, g, tri, trim)


def _diff_kernel(q_ref, k_ref, vt_ref, km_ref, vmt_ref, g_ref,
                 lq1_ref, lk1_ref, lq2_ref, lk2_ref, sub_ref, o_ref, acc_ref, m_ref, l_ref, z_ref,
                 *, qb, tiles):
    i = pl.program_id(2)
    lane = lax.broadcasted_iota(jnp.int32, (1, LANES_V7X), 1)
    key = lax.broadcasted_iota(jnp.int32, (KEY_BLOCK, qb), 0)
    qry = lax.broadcasted_iota(jnp.int32, (KEY_BLOCK, qb), 1)
    chunk_ok = (qry // CHUNK) >= (key // CHUNK)
    chains = [(p, c) for p in range(tiles) for c in range(2)]
    qms = [_masked_queries(q_ref, p, ((lane // (HEAD_DIM // 2)) % 2) == c) for p, c in chains]
    ones = jnp.ones((SUBLANES_BF16_V7X, KEY_BLOCK), jnp.bfloat16)

    def update(n, s, vt_rows, pad_to, first):
        m_blk = jnp.max(s, axis=0, keepdims=True)
        m_new = m_blk if first else jnp.maximum(m_ref[n], m_blk)
        pr = jnp.exp(s - m_new).astype(jnp.bfloat16)
        if pad_to is not None:
            pr = _pad_rows(pr, pad_to)
        pv = jnp.dot(vt_rows, pr, preferred_element_type=jnp.float32)
        num, den = pv[:LANES_V7X], pv[LANES_V7X:LANES_V7X + 1]
        if first:
            l_ref[n] = den
            acc_ref[n] = num
        else:
            alpha = jnp.exp(m_ref[n] - m_new)
            l_ref[n] = l_ref[n] * alpha + den
            acc_ref[n] = acc_ref[n] * alpha + num
        m_ref[n] = m_new

    def scores(n, p, start):
        kblk = k_ref[0, pl.ds(start, KEY_BLOCK), p * LANES_V7X:(p + 1) * LANES_V7X]
        z_ref[n] = lax.dot_general(kblk, qms[n], _NT, preferred_element_type=jnp.float32)

    def block_start(j):
        return pl.multiple_of(jnp.maximum(jnp.minimum(j, i - 1), 0) * KEY_BLOCK, KEY_BLOCK)

    def step(j, next_start, mask, first):
        for n, (p, c) in enumerate(chains):
            s = z_ref[n] if mask is None else jnp.where(mask, z_ref[n], NEG_BIG)
            vt_rows = jnp.concatenate([vt_ref[0, j, p * LANES_V7X:(p + 1) * LANES_V7X, :], ones],
                                      axis=0)
            update(n, s, vt_rows, None, first)
            scores(n, p, next_start)

    for n, (p, c) in enumerate(chains):
        scores(n, p, pl.multiple_of(i * qb, qb))
    step(i, block_start(0), chunk_ok, True)

    def body(t, c):
        step(t, block_start(t + 1), None, False)
        return c

    lax.fori_loop(0, i, body, 0)

    ss = [lax.dot_general(km_ref[:, p * LANES_V7X:(p + 1) * LANES_V7X], qms[n], _NT,
                          preferred_element_type=jnp.float32)
          for n, (p, c) in enumerate(chains)]
    for n, (p, c) in enumerate(chains):
        vt_rows = jnp.concatenate([vmt_ref[p * LANES_V7X:(p + 1) * LANES_V7X, :],
                                   ones[:, :LANES_V7X]], axis=0)
        update(n, ss[n], vt_rows, LANES_V7X, False)

    lam = (jnp.exp(jnp.sum(lq1_ref[...] * lk1_ref[...], axis=-1, keepdims=True))
           - jnp.exp(jnp.sum(lq2_ref[...] * lk2_ref[...], axis=-1, keepdims=True))
           + LAMBDA_INIT)
    for p in range(tiles):
        ot = (acc_ref[2 * p] * (1.0 / l_ref[2 * p])
              - acc_ref[2 * p + 1] * (lam / l_ref[2 * p + 1]))
        ot = ot * lax.rsqrt(jnp.mean(ot * ot, axis=0, keepdims=True) + RMS_EPS)
        o = ot.T * (sub_ref[...] * (1.0 - LAMBDA_INIT))
        o_ref[0, :, p * LANES_V7X:(p + 1) * LANES_V7X] = (
            o * g_ref[0, :, p * LANES_V7X:(p + 1) * LANES_V7X]).astype(o_ref.dtype)


def _diff_attention(q, k, vt, km, vmt, g, lq1, lk1, lq2, lk2, sub, qb, tiles):
    b, s, width = q.shape
    grid, qspec, kspec, vtspec, kmspec, vmtspec = _attn_specs(b, s, width, qb, tiles)
    full = lambda a: pl.BlockSpec(a.shape, lambda bi, h, i: (0,) * a.ndim)
    stat = pltpu.VMEM((2 * tiles, 1, qb), jnp.float32)
    return pl.pallas_call(
        functools.partial(_diff_kernel, qb=qb, tiles=tiles),
        grid=grid,
        in_specs=[qspec, kspec, vtspec, kmspec, vmtspec, qspec,
                  full(lq1), full(lk1), full(lq2), full(lk2), full(sub)],
        out_specs=qspec,
        out_shape=jax.ShapeDtypeStruct((b, s, width), jnp.bfloat16),
        scratch_shapes=[pltpu.VMEM((2 * tiles, LANES_V7X, qb), jnp.float32), stat, stat,
                        pltpu.VMEM((2 * tiles, KEY_BLOCK, qb), jnp.float32)],
        compiler_params=pltpu.CompilerParams(
            dimension_semantics=("parallel", "parallel", "arbitrary"),
            vmem_limit_bytes=VMEM_LIMIT_BYTES_V7X),
        name="diff_attn",
    )(q, k, vt, km, vmt, g, lq1, lk1, lq2, lk2, sub)


def _out_kernel(a_ref, b_ref, w_ref, x_ref, g_ref, o_ref):
    half = a_ref.shape[1]
    h = (x_ref[...]
         + jnp.dot(a_ref[...], w_ref[:half, :], preferred_element_type=jnp.float32)
         + jnp.dot(b_ref[...], w_ref[half:, :], preferred_element_type=jnp.float32))
    o_ref[...] = _rms(h, g_ref[...])


def _out_project(mix_a, mix_b, wo, x2d, gain, tm):
    rows, d = x2d.shape
    return pl.pallas_call(
        _out_kernel,
        grid=(rows // tm,),
        in_specs=[pl.BlockSpec((tm, mix_a.shape[1]), lambda i: (i, 0)),
                  pl.BlockSpec((tm, mix_b.shape[1]), lambda i: (i, 0)),
                  pl.BlockSpec(wo.shape, lambda i: (0, 0)),
                  pl.BlockSpec((tm, d), lambda i: (i, 0)),
                  pl.BlockSpec((1, d), lambda i: (0, 0))],
        out_specs=pl.BlockSpec((tm, d), lambda i: (i, 0)),
        out_shape=jax.ShapeDtypeStruct((rows, d), jnp.float32),
        compiler_params=pltpu.CompilerParams(
            dimension_semantics=("parallel",), vmem_limit_bytes=VMEM_LIMIT_BYTES_V7X),
        name="out_proj",
    )(mix_a, mix_b, wo, x2d, gain)


def _rope_column_order():
    h = HEAD_DIM // 2
    one = np.concatenate([np.arange(0, h), np.arange(2 * h, 3 * h),
                          np.arange(h, 2 * h), np.arange(3 * h, 4 * h)])
    return np.concatenate([one + LANES_V7X * j for j in range(GROUP // LANES_V7X)])


def _rope_tables(length):
    inv = 1.0 / (ROPE_THETA ** (jnp.arange(0, HEAD_DIM, 2, dtype=jnp.float32) / HEAD_DIM))
    ang = jnp.arange(length, dtype=jnp.float32)[:, None] * inv[None, :]
    cos, sin = jnp.cos(ang), jnp.sin(ang)
    return (jnp.concatenate([cos, cos, cos, cos], axis=1),
            jnp.concatenate([-sin, -sin, sin, sin], axis=1))


def _forward(x, meta_tokens, norm_gain, w_in, w_out, lambda_q1, lambda_k1, lambda_q2, lambda_k2,
             subln_gain, final_norm_gain, *, qb, tm, tiles):
    assert qb == KEY_BLOCK and tm % KEY_BLOCK == 0
    b, s, d = x.shape
    order = _rope_column_order()
    w = w_in[0]
    perm = np.concatenate([np.arange(4 * GROUP), 4 * GROUP + order, 5 * GROUP + order,
                           np.arange(6 * GROUP, 8 * GROUP)])
    wb = w[:, perm].astype(jnp.bfloat16)
    wt = jnp.concatenate([w[:, 2 * GROUP:3 * GROUP].T, w[:, 6 * GROUP:7 * GROUP].T],
                         axis=0).astype(jnp.bfloat16)
    wo = w_out[0].astype(jnp.bfloat16)
    cos, sin = _rope_tables(N_META + s)
    gain = norm_gain[0][None, :]
    x2d = x.reshape(b * s, d)

    px = _project(x2d, gain, wb, wt, cos[N_META:], sin[N_META:], tm, False)
    sbk_m, sbv_m, dk_m, dv_m = _project(meta_tokens, gain, wb, wt, cos[:N_META], sin[:N_META],
                                        N_META, True)
    sbq, sbk, sbvt, sbg, dq, dk, dvt, dg = px
    rows3 = lambda a: a.reshape(b, s, GROUP)
    blocks = lambda a: a.reshape(b, s // KEY_BLOCK, GROUP, KEY_BLOCK)
    lanes = lambda a: jnp.pad(a.T, ((0, 0), (0, LANES_V7X - N_META)))

    kb = np.arange(KEY_BLOCK)
    later = (kb[None, :] > kb[:, None]).astype(np.float32)
    tri = np.concatenate([np.concatenate([later, later], axis=1),
                          np.ones((SUBLANES_BF16_V7X, 2 * KEY_BLOCK), np.float32)], axis=0)
    trim = np.zeros((N_META, LANES_V7X), np.float32)
    trim[:, :N_META] = later[:N_META, :N_META]
    trim[:, N_META:2 * N_META] = later[:N_META, :N_META]
    tri, trim = jnp.asarray(tri, jnp.bfloat16), jnp.asarray(trim, jnp.bfloat16)

    mix_sb = _sb_attention(rows3(sbq), rows3(sbk), blocks(sbvt), sbk_m, lanes(sbv_m), rows3(sbg),
                           tri, trim, qb, tiles)
    mix_df = _diff_attention(rows3(dq), rows3(dk), blocks(dvt), dk_m, lanes(dv_m), rows3(dg),
                             lambda_q1, lambda_k1, lambda_q2, lambda_k2, subln_gain, qb, tiles)
    y = _out_project(mix_sb.reshape(b * s, GROUP), mix_df.reshape(b * s, GROUP), wo, x2d,
                     final_norm_gain[None, :], tm)
    return y.reshape(b, s, d)


def kernel(x, meta_tokens, norm_gain, w_in, w_out, lambda_q1, lambda_k1, lambda_q2, lambda_k2,
           subln_gain, final_norm_gain):
    return _forward(x, meta_tokens, norm_gain, w_in, w_out, lambda_q1, lambda_k1, lambda_q2,
                    lambda_k2, subln_gain, final_norm_gain, qb=KEY_BLOCK, tm=512, tiles=4)
```

```python
import functools

import numpy as np
import jax
import jax.numpy as jnp
from jax import lax
from jax.experimental import pallas as pl
from jax.experimental.pallas import tpu as pltpu

LANES_V7X = 128
SUBLANES_BF16_V7X = 16
VMEM_LIMIT_BYTES_V7X = 56 * 1024 * 1024

N_META = 16
CHUNK = 64
ROPE_THETA = 10000.0
RMS_EPS = 1e-6
HEAD_DIM = 64
GROUP = 512
KEY_BLOCK = 256
NEG_BIG = -1e30
LOG2_E = 1.4426950408889634
LAMBDA_INIT = 0.8 - 0.6 * float(np.exp(-0.3 * 0))


def _rms(x, g):
    return x * lax.rsqrt(jnp.mean(x * x, axis=-1, keepdims=True) + RMS_EPS) * g


def _proj_kernel(x_ref, g_ref, w_ref, wt_ref, cos_ref, sin_ref, *out_refs, meta):
    u = _rms(x_ref[...], g_ref[...]).astype(jnp.bfloat16)
    nt = (((1,), (1,)), ((), ()))

    def col(c):
        return jnp.dot(u, w_ref[:, c * GROUP:(c + 1) * GROUP], preferred_element_type=jnp.float32)

    def col_t(c):
        return lax.dot_general(wt_ref[c * GROUP:(c + 1) * GROUP, :], u, nt,
                               preferred_element_type=jnp.float32).astype(jnp.bfloat16)

    def silu(g):
        return g / (1.0 + jnp.exp(-g))

    def rope(t):
        cos = jnp.concatenate([cos_ref[...]] * (GROUP // LANES_V7X), axis=1)
        sin = jnp.concatenate([sin_ref[...]] * (GROUP // LANES_V7X), axis=1)
        parts = [pltpu.roll(t[:, j * LANES_V7X:(j + 1) * LANES_V7X], LANES_V7X // 2, 1)
                 for j in range(GROUP // LANES_V7X)]
        return t * cos + jnp.concatenate(parts, axis=1) * sin

    scale = HEAD_DIM ** -0.5
    if meta:
        sbk_ref, sbv_ref, dk_ref, dv_ref = out_refs
        sbk_ref[...] = col(1).astype(jnp.bfloat16)
        sbv_ref[...] = col(2).astype(jnp.bfloat16)
        dk_ref[...] = rope(col(5)).astype(jnp.bfloat16)
        dv_ref[...] = col(6).astype(jnp.bfloat16)
        return
    sbq_ref, sbk_ref, sbvt_ref, sbg_ref, dq_ref, dk_ref, dvt_ref, dg_ref = out_refs
    sbq_ref[...] = (col(0) * scale).astype(jnp.bfloat16)
    sbk_ref[...] = col(1).astype(jnp.bfloat16)
    sbg_ref[...] = silu(col(3))
    dq_ref[...] = (rope(col(4)) * scale).astype(jnp.bfloat16)
    dk_ref[...] = rope(col(5)).astype(jnp.bfloat16)
    dg_ref[...] = silu(col(7))
    for c, ref in ((0, sbvt_ref), (1, dvt_ref)):
        vt = col_t(c)
        for j in range(ref.shape[0]):
            ref[j] = vt[:, j * KEY_BLOCK:(j + 1) * KEY_BLOCK]


def _project(x2d, gain, wb, wt, cos, sin, tm, meta):
    rows, d = x2d.shape
    n_pos_tiles = cos.shape[0] // tm
    row_spec = lambda width: pl.BlockSpec((tm, width), lambda i: (i, 0))
    tab_spec = pl.BlockSpec((tm, LANES_V7X), lambda i: (i % n_pos_tiles, 0))
    bf, f32 = jnp.bfloat16, jnp.float32
    row_out = lambda dt: (row_spec(GROUP), jax.ShapeDtypeStruct((rows, GROUP), dt))
    if meta:
        outs = [row_out(bf)] * 4
    else:
        vt_out = (pl.BlockSpec((tm // KEY_BLOCK, GROUP, KEY_BLOCK), lambda i: (i, 0, 0)),
                  jax.ShapeDtypeStruct((rows // KEY_BLOCK, GROUP, KEY_BLOCK), bf))
        outs = [row_out(bf), row_out(bf), vt_out, row_out(f32),
                row_out(bf), row_out(bf), vt_out, row_out(f32)]
    return pl.pallas_call(
        functools.partial(_proj_kernel, meta=meta),
        grid=(rows // tm,),
        in_specs=[row_spec(d),
                  pl.BlockSpec((1, d), lambda i: (0, 0)),
                  pl.BlockSpec(wb.shape, lambda i: (0, 0)),
                  pl.BlockSpec(wt.shape, lambda i: (0, 0)),
                  tab_spec, tab_spec],
        out_specs=[o[0] for o in outs],
        out_shape=[o[1] for o in outs],
        compiler_params=pltpu.CompilerParams(
            dimension_semantics=("parallel",), vmem_limit_bytes=VMEM_LIMIT_BYTES_V7X),
        name="proj_meta" if meta else "proj",
    )(x2d, gain, wb, wt, cos, sin)


_NT = (((1,), (1,)), ((), ()))


def _masked_queries(q_ref, p, keep):
    q = q_ref[0, :, p * LANES_V7X:(p + 1) * LANES_V7X].astype(jnp.float32)
    return jnp.where(keep, q, 0.0).astype(jnp.bfloat16)


def _pad_rows(a, rows):
    return jnp.concatenate([a, jnp.zeros((rows - a.shape[0], a.shape[1]), a.dtype)], axis=0)


def _sb_kernel(q_ref, k_ref, vt_ref, km_ref, vmt_ref, g_ref, tri_ref, trim_ref, o_ref,
               acc_ref, car_ref, z_ref, *, qb, tiles):
    i = pl.program_id(2)
    lane = lax.broadcasted_iota(jnp.int32, (1, LANES_V7X), 1)
    key = lax.broadcasted_iota(jnp.int32, (KEY_BLOCK, qb), 0)
    qry = lax.broadcasted_iota(jnp.int32, (KEY_BLOCK, qb), 1)
    strict = key < qry
    chains = [(p, hh) for p in range(tiles) for hh in range(2)]
    qms = [_masked_queries(q_ref, p, (lane // HEAD_DIM) == hh) for p, hh in chains]

    def log_sigmoids(z, mask):
        log_beta = jnp.minimum(z, 0.0) - jnp.log(1.0 + jnp.exp2(jnp.abs(z) * (-LOG2_E)))
        log_keep = log_beta - z
        if mask is not None:
            log_keep = jnp.where(mask, log_keep, 0.0)
        return log_beta, log_keep.astype(jnp.bfloat16)

    def scores(n, p, start):
        kblk = k_ref[0, pl.ds(start, KEY_BLOCK), p * LANES_V7X:(p + 1) * LANES_V7X]
        z_ref[n] = lax.dot_general(kblk, qms[n], _NT, preferred_element_type=jnp.float32)

    def step(next_start, vtblk, mask, first):
        log_betas, sums = [], []
        for n, (p, hh) in enumerate(chains):
            log_beta, log_keep = log_sigmoids(z_ref[n], mask)
            sums.append(jnp.dot(tri_ref[...], log_keep, preferred_element_type=jnp.float32))
            log_betas.append(log_beta)
            scores(n, p, next_start)
        for n, (p, hh) in enumerate(chains):
            w = jnp.exp(log_betas[n] + sums[n][:KEY_BLOCK])
            tot = sums[n][KEY_BLOCK:KEY_BLOCK + 1]
            if mask is not None:
                w = jnp.where(mask, w, 0.0)
            pv = jnp.dot(vtblk(p, hh), w.astype(jnp.bfloat16), preferred_element_type=jnp.float32)
            if first:
                acc_ref[n] = pv
                car_ref[n] = tot
            else:
                acc_ref[n] += pv * jnp.exp(car_ref[n])
                car_ref[n] += tot

    def vt_at(j):
        return lambda p, hh: vt_ref[0, j, pl.ds(p * LANES_V7X + hh * HEAD_DIM, HEAD_DIM), :]

    def block_start(j):
        return pl.multiple_of(jnp.maximum(j, 0) * KEY_BLOCK, KEY_BLOCK)

    for n, (p, hh) in enumerate(chains):
        scores(n, p, block_start(i))
    step(block_start(i - 1), vt_at(i), strict, True)

    def body(t, c):
        j = i - 1 - t
        step(block_start(j - 1), vt_at(j), None, False)
        return c

    lax.fori_loop(0, i, body, 0)

    zs = [lax.dot_general(km_ref[:, p * LANES_V7X:(p + 1) * LANES_V7X], qms[n], _NT,
                          preferred_element_type=jnp.float32)
          for n, (p, hh) in enumerate(chains)]
    log_betas, sticks = [], []
    for z in zs:
        log_beta, log_keep = log_sigmoids(z, None)
        sticks.append(jnp.dot(trim_ref[...], _pad_rows(log_keep, LANES_V7X),
                              preferred_element_type=jnp.float32))
        log_betas.append(log_beta)
    ws = [_pad_rows(jnp.exp(log_betas[n] + sticks[n] + car_ref[n]).astype(jnp.bfloat16),
                    LANES_V7X) for n in range(len(chains))]
    for n, (p, hh) in enumerate(chains):
        acc_ref[n] += jnp.dot(vmt_ref[pl.ds(p * LANES_V7X + hh * HEAD_DIM, HEAD_DIM), :], ws[n],
                              preferred_element_type=jnp.float32)
    for p in range(tiles):
        o = jnp.concatenate([acc_ref[2 * p], acc_ref[2 * p + 1]], axis=0).T
        o_ref[0, :, p * LANES_V7X:(p + 1) * LANES_V7X] = (
            o * g_ref[0, :, p * LANES_V7X:(p + 1) * LANES_V7X]).astype(o_ref.dtype)


def _attn_specs(b, s, width, qb, tiles):
    tw = tiles * LANES_V7X
    qspec = pl.BlockSpec((1, qb, tw), lambda bi, p, i: (bi, i, p))
    kspec = pl.BlockSpec((1, s, tw), lambda bi, p, i: (bi, 0, p))
    vtspec = pl.BlockSpec((1, s // KEY_BLOCK, tw, KEY_BLOCK), lambda bi, p, i: (bi, 0, p, 0))
    kmspec = pl.BlockSpec((N_META, tw), lambda bi, p, i: (0, p))
    vmtspec = pl.BlockSpec((tw, LANES_V7X), lambda bi, p, i: (p, 0))
    grid = (b, width // tw, s // qb)
    return grid, qspec, kspec, vtspec, kmspec, vmtspec


def _sb_attention(q, k, vt, km, vmt, g, tri, trim, qb, tiles):
    b, s, width = q.shape
    grid, qspec, kspec, vtspec, kmspec, vmtspec = _attn_specs(b, s, width, qb, tiles)
    full = lambda a: pl.BlockSpec(a.shape, lambda bi, p, i: (0,) * a.ndim)
    return pl.pallas_call(
        functools.partial(_sb_kernel, qb=qb, tiles=tiles),
        grid=grid,
        in_specs=[qspec, kspec, vtspec, kmspec, vmtspec, qspec, full(tri), full(trim)],
        out_specs=qspec,
        out_shape=jax.ShapeDtypeStruct((b, s, width), jnp.bfloat16),
        scratch_shapes=[pltpu.VMEM((2 * tiles, HEAD_DIM, qb), jnp.float32),
                        pltpu.VMEM((2 * tiles, 1, qb), jnp.float32),
                        pltpu.VMEM((2 * tiles, KEY_BLOCK, qb), jnp.float32)],
        compiler_params=pltpu.CompilerParams(
            dimension_semantics=("parallel", "parallel", "arbitrary"),
            vmem_limit_bytes=VMEM_LIMIT_BYTES_V7X),
        name="sb_attn",
    )(q, k, vt, km, vmt, g, tri, trim)


def _diff_kernel(q_ref, k_ref, vt_ref, km_ref, vmt_ref, g_ref,
                 lq1_ref, lk1_ref, lq2_ref, lk2_ref, sub_ref, o_ref, acc_ref, m_ref, l_ref, z_ref,
                 *, qb, tiles):
    i = pl.program_id(2)
    lane = lax.broadcasted_iota(jnp.int32, (1, LANES_V7X), 1)
    key = lax.broadcasted_iota(jnp.int32, (KEY_BLOCK, qb), 0)
    qry = lax.broadcasted_iota(jnp.int32, (KEY_BLOCK, qb), 1)
    chunk_ok = (qry // CHUNK) >= (key // CHUNK)
    chains = [(p, c) for p in range(tiles) for c in range(2)]
    qms = [_masked_queries(q_ref, p, ((lane // (HEAD_DIM // 2)) % 2) == c) for p, c in chains]
    ones = jnp.ones((SUBLANES_BF16_V7X, KEY_BLOCK), jnp.bfloat16)

    def update(n, s, vt_rows, pad_to, first):
        m_blk = jnp.max(s, axis=0, keepdims=True)
        m_new = m_blk if first else jnp.maximum(m_ref[n], m_blk)
        pr = jnp.exp(s - m_new).astype(jnp.bfloat16)
        if pad_to is not None:
            pr = _pad_rows(pr, pad_to)
        pv = jnp.dot(vt_rows, pr, preferred_element_type=jnp.float32)
        num, den = pv[:LANES_V7X], pv[LANES_V7X:LANES_V7X + 1]
        if first:
            l_ref[n] = den
            acc_ref[n] = num
        else:
            alpha = jnp.exp(m_ref[n] - m_new)
            l_ref[n] = l_ref[n] * alpha + den
            acc_ref[n] = acc_ref[n] * alpha + num
        m_ref[n] = m_new

    def scores(n, p, start):
        kblk = k_ref[0, pl.ds(start, KEY_BLOCK), p * LANES_V7X:(p + 1) * LANES_V7X]
        z_ref[n] = lax.dot_general(kblk, qms[n], _NT, preferred_element_type=jnp.float32)

    def block_start(j):
        return pl.multiple_of(jnp.maximum(jnp.minimum(j, i - 1), 0) * KEY_BLOCK, KEY_BLOCK)

    def step(j, next_start, mask, first):
        for n, (p, c) in enumerate(chains):
            s = z_ref[n] if mask is None else jnp.where(mask, z_ref[n], NEG_BIG)
            vt_rows = jnp.concatenate([vt_ref[0, j, p * LANES_V7X:(p + 1) * LANES_V7X, :], ones],
                                      axis=0)
            update(n, s, vt_rows, None, first)
            scores(n, p, next_start)

    for n, (p, c) in enumerate(chains):
        scores(n, p, pl.multiple_of(i * qb, qb))
    step(i, block_start(0), chunk_ok, True)

    def body(t, c):
        step(t, block_start(t + 1), None, False)
        return c

    lax.fori_loop(0, i, body, 0)

    ss = [lax.dot_general(km_ref[:, p * LANES_V7X:(p + 1) * LANES_V7X], qms[n], _NT,
                          preferred_element_type=jnp.float32)
          for n, (p, c) in enumerate(chains)]
    for n, (p, c) in enumerate(chains):
        vt_rows = jnp.concatenate([vmt_ref[p * LANES_V7X:(p + 1) * LANES_V7X, :],
                                   ones[:, :LANES_V7X]], axis=0)
        update(n, ss[n], vt_rows, LANES_V7X, False)

    lam = (jnp.exp(jnp.sum(lq1_ref[...] * lk1_ref[...], axis=-1, keepdims=True))
           - jnp.exp(jnp.sum(lq2_ref[...] * lk2_ref[...], axis=-1, keepdims=True))
           + LAMBDA_INIT)
    for p in range(tiles):
        ot = (acc_ref[2 * p] * (1.0 / l_ref[2 * p])
              - acc_ref[2 * p + 1] * (lam / l_ref[2 * p + 1]))
        ot = ot * lax.rsqrt(jnp.mean(ot * ot, axis=0, keepdims=True) + RMS_EPS)
        o = ot.T * (sub_ref[...] * (1.0 - LAMBDA_INIT))
        o_ref[0, :, p * LANES_V7X:(p + 1) * LANES_V7X] = (
            o * g_ref[0, :, p * LANES_V7X:(p + 1) * LANES_V7X]).astype(o_ref.dtype)


def _diff_attention(q, k, vt, km, vmt, g, lq1, lk1, lq2, lk2, sub, qb, tiles):
    b, s, width = q.shape
    grid, qspec, kspec, vtspec, kmspec, vmtspec = _attn_specs(b, s, width, qb, tiles)
    full = lambda a: pl.BlockSpec(a.shape, lambda bi, h, i: (0,) * a.ndim)
    stat = pltpu.VMEM((2 * tiles, 1, qb), jnp.float32)
    return pl.pallas_call(
        functools.partial(_diff_kernel, qb=qb, tiles=tiles),
        grid=grid,
        in_specs=[qspec, kspec, vtspec, kmspec, vmtspec, qspec,
                  full(lq1), full(lk1), full(lq2), full(lk2), full(sub)],
        out_specs=qspec,
        out_shape=jax.ShapeDtypeStruct((b, s, width), jnp.bfloat16),
        scratch_shapes=[pltpu.VMEM((2 * tiles, LANES_V7X, qb), jnp.float32), stat, stat,
                        pltpu.VMEM((2 * tiles, KEY_BLOCK, qb), jnp.float32)],
        compiler_params=pltpu.CompilerParams(
            dimension_semantics=("parallel", "parallel", "arbitrary"),
            vmem_limit_bytes=VMEM_LIMIT_BYTES_V7X),
        name="diff_attn",
    )(q, k, vt, km, vmt, g, lq1, lk1, lq2, lk2, sub)


def _out_kernel(a_ref, b_ref, w_ref, x_ref, g_ref, o_ref):
    half = a_ref.shape[1]
    h = (x_ref[...]
         + jnp.dot(a_ref[...], w_ref[:half, :], preferred_element_type=jnp.float32)
         + jnp.dot(b_ref[...], w_ref[half:, :], preferred_element_type=jnp.float32))
    o_ref[...] = _rms(h, g_ref[...])


def _out_project(mix_a, mix_b, wo, x2d, gain, tm):
    rows, d = x2d.shape
    return pl.pallas_call(
        _out_kernel,
        grid=(rows // tm,),
        in_specs=[pl.BlockSpec((tm, mix_a.shape[1]), lambda i: (i, 0)),
                  pl.BlockSpec((tm, mix_b.shape[1]), lambda i: (i, 0)),
                  pl.BlockSpec(wo.shape, lambda i: (0, 0)),
                  pl.BlockSpec((tm, d), lambda i: (i, 0)),
                  pl.BlockSpec((1, d), lambda i: (0, 0))],
        out_specs=pl.BlockSpec((tm, d), lambda i: (i, 0)),
        out_shape=jax.ShapeDtypeStruct((rows, d), jnp.float32),
        compiler_params=pltpu.CompilerParams(
            dimension_semantics=("parallel",), vmem_limit_bytes=VMEM_LIMIT_BYTES_V7X),
        name="out_proj",
    )(mix_a, mix_b, wo, x2d, gain)


def _rope_column_order():
    h = HEAD_DIM // 2
    one = np.concatenate([np.arange(0, h), np.arange(2 * h, 3 * h),
                          np.arange(h, 2 * h), np.arange(3 * h, 4 * h)])
    return np.concatenate([one + LANES_V7X * j for j in range(GROUP // LANES_V7X)])


def _rope_tables(length):
    inv = 1.0 / (ROPE_THETA ** (jnp.arange(0, HEAD_DIM, 2, dtype=jnp.float32) / HEAD_DIM))
    ang = jnp.arange(length, dtype=jnp.float32)[:, None] * inv[None, :]
    cos, sin = jnp.cos(ang), jnp.sin(ang)
    return (jnp.concatenate([cos, cos, cos, cos], axis=1),
            jnp.concatenate([-sin, -sin, sin, sin], axis=1))


def _forward(x, meta_tokens, norm_gain, w_in, w_out, lambda_q1, lambda_k1, lambda_q2, lambda_k2,
             subln_gain, final_norm_gain, *, qb, tm, tiles):
    assert qb == KEY_BLOCK and tm % KEY_BLOCK == 0
    b, s, d = x.shape
    order = _rope_column_order()
    w = w_in[0]
    perm = np.concatenate([np.arange(4 * GROUP), 4 * GROUP + order, 5 * GROUP + order,
                           np.arange(6 * GROUP, 8 * GROUP)])
    wb = w[:, perm].astype(jnp.bfloat16)
    wt = jnp.concatenate([w[:, 2 * GROUP:3 * GROUP].T, w[:, 6 * GROUP:7 * GROUP].T],
                         axis=0).astype(jnp.bfloat16)
    wo = w_out[0].astype(jnp.bfloat16)
    cos, sin = _rope_tables(N_META + s)
    gain = norm_gain[0][None, :]
    x2d = x.reshape(b * s, d)

    px = _project(x2d, gain, wb, wt, cos[N_META:], sin[N_META:], tm, False)
    sbk_m, sbv_m, dk_m, dv_m = _project(meta_tokens, gain, wb, wt, cos[:N_META], sin[:N_META],
                                        N_META, True)
    sbq, sbk, sbvt, sbg, dq, dk, dvt, dg = px
    rows3 = lambda a: a.reshape(b, s, GROUP)
    blocks = lambda a: a.reshape(b, s // KEY_BLOCK, GROUP, KEY_BLOCK)
    lanes = lambda a: jnp.pad(a.T, ((0, 0), (0, LANES_V7X - N_META)))

    kb = np.arange(KEY_BLOCK)
    later = (kb[None, :] > kb[:, None]).astype(np.float32)
    tri = np.concatenate([later, np.ones((SUBLANES_BF16_V7X, KEY_BLOCK), np.float32)], axis=0)
    trim = np.zeros((N_META, LANES_V7X), np.float32)
    trim[:, :N_META] = later[:N_META, :N_META]
    tri, trim = jnp.asarray(tri, jnp.bfloat16), jnp.asarray(trim, jnp.bfloat16)

    mix_sb = _sb_attention(rows3(sbq), rows3(sbk), blocks(sbvt), sbk_m, lanes(sbv_m), rows3(sbg),
                           tri, trim, qb, tiles)
    mix_df = _diff_attention(rows3(dq), rows3(dk), blocks(dvt), dk_m, lanes(dv_m), rows3(dg),
                             lambda_q1, lambda_k1, lambda_q2, lambda_k2, subln_gain, qb, tiles)
    y = _out_project(mix_sb.reshape(b * s, GROUP), mix_df.reshape(b * s, GROUP), wo, x2d,
                     final_norm_gain[None, :], tm)
    return y.reshape(b, s, d)


def kernel(x, meta_tokens, norm_gain, w_in, w_out, lambda_q1, lambda_k1, lambda_q2, lambda_k2,
           subln_gain, final_norm_gain):
    return _forward(x, meta_tokens, norm_gain, w_in, w_out, lambda_q1, lambda_k1, lambda_q2,
                    lambda_k2, subln_gain, final_norm_gain, qb=KEY_BLOCK, tm=512, tiles=4)
```

```python
import functools

import numpy as np
import jax
import jax.numpy as jnp
from jax import lax
from jax.experimental import pallas as pl
from jax.experimental.pallas import tpu as pltpu

LANES_V7X = 128
SUBLANES_BF16_V7X = 16
VMEM_LIMIT_BYTES_V7X = 56 * 1024 * 1024

N_META = 16
CHUNK = 64
ROPE_THETA = 10000.0
RMS_EPS = 1e-6
HEAD_DIM = 64
GROUP = 512
KEY_BLOCK = 256
NEG_BIG = -1e30
LOG2_E = 1.4426950408889634
EXP_UNDERFLOW = -104.0
LAMBDA_INIT = 0.8 - 0.6 * float(np.exp(-0.3 * 0))


def _rms(x, g):
    return x * lax.rsqrt(jnp.mean(x * x, axis=-1, keepdims=True) + RMS_EPS) * g


def _proj_kernel(x_ref, g_ref, w_ref, wt_ref, cos_ref, sin_ref, *out_refs, meta):
    u = _rms(x_ref[...], g_ref[...]).astype(jnp.bfloat16)
    nt = (((1,), (1,)), ((), ()))

    def col(c):
        return jnp.dot(u, w_ref[:, c * GROUP:(c + 1) * GROUP], preferred_element_type=jnp.float32)

    def col_t(c):
        return lax.dot_general(wt_ref[c * GROUP:(c + 1) * GROUP, :], u, nt,
                               preferred_element_type=jnp.float32).astype(jnp.bfloat16)

    def silu(g):
        return g / (1.0 + jnp.exp(-g))

    def rope(t):
        cos = jnp.concatenate([cos_ref[...]] * (GROUP // LANES_V7X), axis=1)
        sin = jnp.concatenate([sin_ref[...]] * (GROUP // LANES_V7X), axis=1)
        parts = [pltpu.roll(t[:, j * LANES_V7X:(j + 1) * LANES_V7X], LANES_V7X // 2, 1)
                 for j in range(GROUP // LANES_V7X)]
        return t * cos + jnp.concatenate(parts, axis=1) * sin

    scale = HEAD_DIM ** -0.5
    if meta:
        sbk_ref, sbv_ref, dk_ref, dv_ref = out_refs
        sbk_ref[...] = col(1).astype(jnp.bfloat16)
        sbv_ref[...] = col(2).astype(jnp.bfloat16)
        dk_ref[...] = rope(col(5)).astype(jnp.bfloat16)
        dv_ref[...] = col(6).astype(jnp.bfloat16)
        return
    sbq_ref, sbk_ref, sbvt_ref, sbg_ref, dq_ref, dk_ref, dvt_ref, dg_ref = out_refs
    sbq_ref[...] = (col(0) * scale).astype(jnp.bfloat16)
    sbk_ref[...] = col(1).astype(jnp.bfloat16)
    sbg_ref[...] = silu(col(3))
    dq_ref[...] = (rope(col(4)) * scale).astype(jnp.bfloat16)
    dk_ref[...] = rope(col(5)).astype(jnp.bfloat16)
    dg_ref[...] = silu(col(7))
    for c, ref in ((0, sbvt_ref), (1, dvt_ref)):
        vt = col_t(c)
        for j in range(ref.shape[0]):
            ref[j] = vt[:, j * KEY_BLOCK:(j + 1) * KEY_BLOCK]


def _project(x2d, gain, wb, wt, cos, sin, tm, meta):
    rows, d = x2d.shape
    n_pos_tiles = cos.shape[0] // tm
    row_spec = lambda width: pl.BlockSpec((tm, width), lambda i: (i, 0))
    tab_spec = pl.BlockSpec((tm, LANES_V7X), lambda i: (i % n_pos_tiles, 0))
    bf, f32 = jnp.bfloat16, jnp.float32
    row_out = lambda dt: (row_spec(GROUP), jax.ShapeDtypeStruct((rows, GROUP), dt))
    if meta:
        outs = [row_out(bf)] * 4
    else:
        vt_out = (pl.BlockSpec((tm // KEY_BLOCK, GROUP, KEY_BLOCK), lambda i: (i, 0, 0)),
                  jax.ShapeDtypeStruct((rows // KEY_BLOCK, GROUP, KEY_BLOCK), bf))
        outs = [row_out(bf), row_out(bf), vt_out, row_out(f32),
                row_out(bf), row_out(bf), vt_out, row_out(f32)]
    return pl.pallas_call(
        functools.partial(_proj_kernel, meta=meta),
        grid=(rows // tm,),
        in_specs=[row_spec(d),
                  pl.BlockSpec((1, d), lambda i: (0, 0)),
                  pl.BlockSpec(wb.shape, lambda i: (0, 0)),
                  pl.BlockSpec(wt.shape, lambda i: (0, 0)),
                  tab_spec, tab_spec],
        out_specs=[o[0] for o in outs],
        out_shape=[o[1] for o in outs],
        compiler_params=pltpu.CompilerParams(
            dimension_semantics=("parallel",), vmem_limit_bytes=VMEM_LIMIT_BYTES_V7X),
        name="proj_meta" if meta else "proj",
    )(x2d, gain, wb, wt, cos, sin)


_NT = (((1,), (1,)), ((), ()))


def _masked_queries(q_ref, p, keep):
    q = q_ref[0, :, p * LANES_V7X:(p + 1) * LANES_V7X].astype(jnp.float32)
    return jnp.where(keep, q, 0.0).astype(jnp.bfloat16)


def _pad_rows(a, rows):
    return jnp.concatenate([a, jnp.zeros((rows - a.shape[0], a.shape[1]), a.dtype)], axis=0)


def _sb_kernel(q_ref, k_ref, vt_ref, km_ref, vmt_ref, g_ref, tri_ref, trim_ref, o_ref,
               acc_ref, car_ref, z_ref, *, qb, tiles):
    i = pl.program_id(2)
    lane = lax.broadcasted_iota(jnp.int32, (1, LANES_V7X), 1)
    key = lax.broadcasted_iota(jnp.int32, (KEY_BLOCK, qb), 0)
    qry = lax.broadcasted_iota(jnp.int32, (KEY_BLOCK, qb), 1)
    strict = key < qry
    chains = [(p, hh) for p in range(tiles) for hh in range(2)]
    qms = [_masked_queries(q_ref, p, (lane // HEAD_DIM) == hh) for p, hh in chains]

    def log_sigmoids(z, mask):
        log_beta = jnp.minimum(z, 0.0) - jnp.log(1.0 + jnp.exp2(jnp.abs(z) * (-LOG2_E)))
        log_keep = log_beta - z
        if mask is not None:
            log_keep = jnp.where(mask, log_keep, 0.0)
        return log_beta, log_keep.astype(jnp.bfloat16)

    def scores(n, p, start):
        kblk = k_ref[0, pl.ds(start, KEY_BLOCK), p * LANES_V7X:(p + 1) * LANES_V7X]
        z_ref[n] = lax.dot_general(kblk, qms[n], _NT, preferred_element_type=jnp.float32)

    def step(next_start, vtblk, mask, first):
        log_betas, sums = {}, {}

        def stage1(n, p):
            log_beta, log_keep = log_sigmoids(z_ref[n], mask)
            sums[n] = jnp.dot(tri_ref[...], log_keep, preferred_element_type=jnp.float32)
            log_betas[n] = log_beta
            scores(n, p, next_start)

        def stage2(n, p, hh):
            w = jnp.exp(log_betas.pop(n) + sums[n][:KEY_BLOCK])
            tot = sums.pop(n)[KEY_BLOCK:KEY_BLOCK + 1]
            if mask is not None:
                w = jnp.where(mask, w, 0.0)
            pv = jnp.dot(vtblk(p, hh), w.astype(jnp.bfloat16), preferred_element_type=jnp.float32)
            if first:
                acc_ref[n] = pv
                car_ref[n] = tot
            else:
                acc_ref[n] += pv * jnp.exp(car_ref[n])
                car_ref[n] += tot

        for n, (p, hh) in enumerate(chains):
            stage1(n, p)
        for n, (p, hh) in enumerate(chains):
            stage2(n, p, hh)

    def vt_at(j):
        return lambda p, hh: vt_ref[0, j, pl.ds(p * LANES_V7X + hh * HEAD_DIM, HEAD_DIM), :]

    def block_start(j):
        return pl.multiple_of(jnp.maximum(j, 0) * KEY_BLOCK, KEY_BLOCK)

    for n, (p, hh) in enumerate(chains):
        scores(n, p, block_start(i))
    step(block_start(i - 1), vt_at(i), strict, True)

    def largest_carry():
        return jnp.max(functools.reduce(jnp.maximum, [car_ref[n] for n in range(len(chains))]))

    def live(state):
        t, carry_max = state
        return jnp.logical_and(t < i, carry_max > EXP_UNDERFLOW)

    def body(state):
        t, _ = state
        j = i - 1 - t
        step(block_start(j - 1), vt_at(j), None, False)
        return t + 1, largest_carry()

    _, carry_max = lax.while_loop(live, body, (jnp.int32(0), largest_carry()))

    @pl.when(carry_max > EXP_UNDERFLOW)
    def _meta():
        zs = [lax.dot_general(km_ref[:, p * LANES_V7X:(p + 1) * LANES_V7X], qms[n], _NT,
                              preferred_element_type=jnp.float32)
              for n, (p, hh) in enumerate(chains)]
        log_betas, sticks = [], []
        for z in zs:
            log_beta, log_keep = log_sigmoids(z, None)
            sticks.append(jnp.dot(trim_ref[...], _pad_rows(log_keep, LANES_V7X),
                                  preferred_element_type=jnp.float32))
            log_betas.append(log_beta)
        ws = [_pad_rows(jnp.exp(log_betas[n] + sticks[n] + car_ref[n]).astype(jnp.bfloat16),
                        LANES_V7X) for n in range(len(chains))]
        for n, (p, hh) in enumerate(chains):
            acc_ref[n] += jnp.dot(vmt_ref[pl.ds(p * LANES_V7X + hh * HEAD_DIM, HEAD_DIM), :],
                                  ws[n], preferred_element_type=jnp.float32)

    for p in range(tiles):
        o = jnp.concatenate([acc_ref[2 * p], acc_ref[2 * p + 1]], axis=0).T
        o_ref[0, :, p * LANES_V7X:(p + 1) * LANES_V7X] = (
            o * g_ref[0, :, p * LANES_V7X:(p + 1) * LANES_V7X]).astype(o_ref.dtype)


def _attn_specs(b, s, width, qb, tiles):
    tw = tiles * LANES_V7X
    qspec = pl.BlockSpec((1, qb, tw), lambda bi, p, i: (bi, i, p))
    kspec = pl.BlockSpec((1, s, tw), lambda bi, p, i: (bi, 0, p))
    vtspec = pl.BlockSpec((1, s // KEY_BLOCK, tw, KEY_BLOCK), lambda bi, p, i: (bi, 0, p, 0))
    kmspec = pl.BlockSpec((N_META, tw), lambda bi, p, i: (0, p))
    vmtspec = pl.BlockSpec((tw, LANES_V7X), lambda bi, p, i: (p, 0))
    grid = (b, width // tw, s // qb)
    return grid, qspec, kspec, vtspec, kmspec, vmtspec


def _sb_attention(q, k, vt, km, vmt, g, tri, trim, qb, tiles):
    b, s, width = q.shape
    grid, qspec, kspec, vtspec, kmspec, vmtspec = _attn_specs(b, s, width, qb, tiles)
    full = lambda a: pl.BlockSpec(a.shape, lambda bi, p, i: (0,) * a.ndim)
    return pl.pallas_call(
        functools.partial(_sb_kernel, qb=qb, tiles=tiles),
        grid=grid,
        in_specs=[qspec, kspec, vtspec, kmspec, vmtspec, qspec, full(tri), full(trim)],
        out_specs=qspec,
        out_shape=jax.ShapeDtypeStruct((b, s, width), jnp.bfloat16),
        scratch_shapes=[pltpu.VMEM((2 * tiles, HEAD_DIM, qb), jnp.float32),
                        pltpu.VMEM((2 * tiles, 1, qb), jnp.float32),
                        pltpu.VMEM((2 * tiles, KEY_BLOCK, qb), jnp.float32)],
        compiler_params=pltpu.CompilerParams(
            dimension_semantics=("parallel", "parallel", "arbitrary"),
            vmem_limit_bytes=VMEM_LIMIT_BYTES_V7X),
        name="sb_attn",
    )(q, k, vt, km, vmt, g, tri, trim)


def _diff_kernel(q_ref, k_ref, vt_ref, km_ref, vmt_ref, g_ref,
                 lq1_ref, lk1_ref, lq2_ref, lk2_ref, sub_ref, o_ref, acc_ref, m_ref, l_ref, z_ref,
                 *, qb, tiles):
    i = pl.program_id(2)
    lane = lax.broadcasted_iota(jnp.int32, (1, LANES_V7X), 1)
    key = lax.broadcasted_iota(jnp.int32, (KEY_BLOCK, qb), 0)
    qry = lax.broadcasted_iota(jnp.int32, (KEY_BLOCK, qb), 1)
    chunk_ok = (qry // CHUNK) >= (key // CHUNK)
    chains = [(p, c) for p in range(tiles) for c in range(2)]
    qms = [_masked_queries(q_ref, p, ((lane // (HEAD_DIM // 2)) % 2) == c) for p, c in chains]
    ones = jnp.ones((SUBLANES_BF16_V7X, KEY_BLOCK), jnp.bfloat16)

    def update(n, s, vt_rows, pad_to, first):
        m_blk = jnp.max(s, axis=0, keepdims=True)
        m_new = m_blk if first else jnp.maximum(m_ref[n], m_blk)
        pr = jnp.exp(s - m_new).astype(jnp.bfloat16)
        if pad_to is not None:
            pr = _pad_rows(pr, pad_to)
        pv = jnp.dot(vt_rows, pr, preferred_element_type=jnp.float32)
        num, den = pv[:LANES_V7X], pv[LANES_V7X:LANES_V7X + 1]
        if first:
            l_ref[n] = den
            acc_ref[n] = num
        else:
            alpha = jnp.exp(m_ref[n] - m_new)
            l_ref[n] = l_ref[n] * alpha + den
            acc_ref[n] = acc_ref[n] * alpha + num
        m_ref[n] = m_new

    def scores(n, p, start):
        kblk = k_ref[0, pl.ds(start, KEY_BLOCK), p * LANES_V7X:(p + 1) * LANES_V7X]
        z_ref[n] = lax.dot_general(kblk, qms[n], _NT, preferred_element_type=jnp.float32)

    def block_start(j):
        return pl.multiple_of(jnp.maximum(jnp.minimum(j, i - 1), 0) * KEY_BLOCK, KEY_BLOCK)

    def step(j, next_start, mask, first):
        for n, (p, c) in enumerate(chains):
            s = z_ref[n] if mask is None else jnp.where(mask, z_ref[n], NEG_BIG)
            vt_rows = jnp.concatenate([vt_ref[0, j, p * LANES_V7X:(p + 1) * LANES_V7X, :], ones],
                                      axis=0)
            update(n, s, vt_rows, None, first)
            scores(n, p, next_start)

    for n, (p, c) in enumerate(chains):
        scores(n, p, pl.multiple_of(i * qb, qb))
    ss = [lax.dot_general(km_ref[:, p * LANES_V7X:(p + 1) * LANES_V7X], qms[n], _NT,
                          preferred_element_type=jnp.float32)
          for n, (p, c) in enumerate(chains)]
    step(i, block_start(0), chunk_ok, True)
    for n, (p, c) in enumerate(chains):
        vt_rows = jnp.concatenate([vmt_ref[p * LANES_V7X:(p + 1) * LANES_V7X, :],
                                   ones[:, :LANES_V7X]], axis=0)
        update(n, ss[n], vt_rows, LANES_V7X, False)

    def body(t, c):
        step(t, block_start(t + 1), None, False)
        return c

    lax.fori_loop(0, i, body, 0)

    lam = (jnp.exp(jnp.sum(lq1_ref[...] * lk1_ref[...], axis=-1, keepdims=True))
           - jnp.exp(jnp.sum(lq2_ref[...] * lk2_ref[...], axis=-1, keepdims=True))
           + LAMBDA_INIT)
    for p in range(tiles):
        ot = (acc_ref[2 * p] * (1.0 / l_ref[2 * p])
              - acc_ref[2 * p + 1] * (lam / l_ref[2 * p + 1]))
        ot = ot * lax.rsqrt(jnp.mean(ot * ot, axis=0, keepdims=True) + RMS_EPS)
        o = ot.T * (sub_ref[...] * (1.0 - LAMBDA_INIT))
        o_ref[0, :, p * LANES_V7X:(p + 1) * LANES_V7X] = (
            o * g_ref[0, :, p * LANES_V7X:(p + 1) * LANES_V7X]).astype(o_ref.dtype)


def _diff_attention(q, k, vt, km, vmt, g, lq1, lk1, lq2, lk2, sub, qb, tiles):
    b, s, width = q.shape
    grid, qspec, kspec, vtspec, kmspec, vmtspec = _attn_specs(b, s, width, qb, tiles)
    full = lambda a: pl.BlockSpec(a.shape, lambda bi, h, i: (0,) * a.ndim)
    stat = pltpu.VMEM((2 * tiles, 1, qb), jnp.float32)
    return pl.pallas_call(
        functools.partial(_diff_kernel, qb=qb, tiles=tiles),
        grid=grid,
        in_specs=[qspec, kspec, vtspec, kmspec, vmtspec, qspec,
                  full(lq1), full(lk1), full(lq2), full(lk2), full(sub)],
        out_specs=qspec,
        out_shape=jax.ShapeDtypeStruct((b, s, width), jnp.bfloat16),
        scratch_shapes=[pltpu.VMEM((2 * tiles, LANES_V7X, qb), jnp.float32), stat, stat,
                        pltpu.VMEM((2 * tiles, KEY_BLOCK, qb), jnp.float32)],
        compiler_params=pltpu.CompilerParams(
            dimension_semantics=("parallel", "parallel", "arbitrary"),
            vmem_limit_bytes=VMEM_LIMIT_BYTES_V7X),
        name="diff_attn",
    )(q, k, vt, km, vmt, g, lq1, lk1, lq2, lk2, sub)


def _out_kernel(a_ref, b_ref, w_ref, x_ref, g_ref, o_ref):
    half = a_ref.shape[1]
    h = (x_ref[...]
         + jnp.dot(a_ref[...], w_ref[:half, :], preferred_element_type=jnp.float32)
         + jnp.dot(b_ref[...], w_ref[half:, :], preferred_element_type=jnp.float32))
    o_ref[...] = _rms(h, g_ref[...])


def _out_project(mix_a, mix_b, wo, x2d, gain, tm):
    rows, d = x2d.shape
    return pl.pallas_call(
        _out_kernel,
        grid=(rows // tm,),
        in_specs=[pl.BlockSpec((tm, mix_a.shape[1]), lambda i: (i, 0)),
                  pl.BlockSpec((tm, mix_b.shape[1]), lambda i: (i, 0)),
                  pl.BlockSpec(wo.shape, lambda i: (0, 0)),
                  pl.BlockSpec((tm, d), lambda i: (i, 0)),
                  pl.BlockSpec((1, d), lambda i: (0, 0))],
        out_specs=pl.BlockSpec((tm, d), lambda i: (i, 0)),
        out_shape=jax.ShapeDtypeStruct((rows, d), jnp.float32),
        compiler_params=pltpu.CompilerParams(
            dimension_semantics=("parallel",), vmem_limit_bytes=VMEM_LIMIT_BYTES_V7X),
        name="out_proj",
    )(mix_a, mix_b, wo, x2d, gain)


def _rope_column_order():
    h = HEAD_DIM // 2
    one = np.concatenate([np.arange(0, h), np.arange(2 * h, 3 * h),
                          np.arange(h, 2 * h), np.arange(3 * h, 4 * h)])
    return np.concatenate([one + LANES_V7X * j for j in range(GROUP // LANES_V7X)])


def _rope_tables(length):
    inv = 1.0 / (ROPE_THETA ** (jnp.arange(0, HEAD_DIM, 2, dtype=jnp.float32) / HEAD_DIM))
    ang = jnp.arange(length, dtype=jnp.float32)[:, None] * inv[None, :]
    cos, sin = jnp.cos(ang), jnp.sin(ang)
    return (jnp.concatenate([cos, cos, cos, cos], axis=1),
            jnp.concatenate([-sin, -sin, sin, sin], axis=1))


def _forward(x, meta_tokens, norm_gain, w_in, w_out, lambda_q1, lambda_k1, lambda_q2, lambda_k2,
             subln_gain, final_norm_gain, *, qb, tm, tiles):
    assert qb == KEY_BLOCK and tm % KEY_BLOCK == 0
    b, s, d = x.shape
    order = _rope_column_order()
    w = w_in[0]
    perm = np.concatenate([np.arange(4 * GROUP), 4 * GROUP + order, 5 * GROUP + order,
                           np.arange(6 * GROUP, 8 * GROUP)])
    wb = w[:, perm].astype(jnp.bfloat16)
    wt = jnp.concatenate([w[:, 2 * GROUP:3 * GROUP].T, w[:, 6 * GROUP:7 * GROUP].T],
                         axis=0).astype(jnp.bfloat16)
    wo = w_out[0].astype(jnp.bfloat16)
    cos, sin = _rope_tables(N_META + s)
    gain = norm_gain[0][None, :]
    x2d = x.reshape(b * s, d)

    px = _project(x2d, gain, wb, wt, cos[N_META:], sin[N_META:], tm, False)
    sbk_m, sbv_m, dk_m, dv_m = _project(meta_tokens, gain, wb, wt, cos[:N_META], sin[:N_META],
                                        N_META, True)
    sbq, sbk, sbvt, sbg, dq, dk, dvt, dg = px
    rows3 = lambda a: a.reshape(b, s, GROUP)
    blocks = lambda a: a.reshape(b, s // KEY_BLOCK, GROUP, KEY_BLOCK)
    lanes = lambda a: jnp.pad(a.T, ((0, 0), (0, LANES_V7X - N_META)))

    kb = np.arange(KEY_BLOCK)
    later = (kb[None, :] > kb[:, None]).astype(np.float32)
    tri = np.concatenate([later, np.ones((SUBLANES_BF16_V7X, KEY_BLOCK), np.float32)], axis=0)
    trim = np.zeros((N_META, LANES_V7X), np.float32)
    trim[:, :N_META] = later[:N_META, :N_META]
    tri, trim = jnp.asarray(tri, jnp.bfloat16), jnp.asarray(trim, jnp.bfloat16)

    mix_sb = _sb_attention(rows3(sbq), rows3(sbk), blocks(sbvt), sbk_m, lanes(sbv_m), rows3(sbg),
                           tri, trim, qb, tiles)
    mix_df = _diff_attention(rows3(dq), rows3(dk), blocks(dvt), dk_m, lanes(dv_m), rows3(dg),
                             lambda_q1, lambda_k1, lambda_q2, lambda_k2, subln_gain, qb, tiles)
    y = _out_project(mix_sb.reshape(b * s, GROUP), mix_df.reshape(b * s, GROUP), wo, x2d,
                     final_norm_gain[None, :], tm)
    return y.reshape(b, s, d)


def kernel(x, meta_tokens, norm_gain, w_in, w_out, lambda_q1, lambda_k1, lambda_q2, lambda_k2,
           subln_gain, final_norm_gain):
    return _forward(x, meta_tokens, norm_gain, w_in, w_out, lambda_q1, lambda_k1, lambda_q2,
                    lambda_k2, subln_gain, final_norm_gain, qb=KEY_BLOCK, tm=512, tiles=4)
```

```python
import functools

import numpy as np
import jax
import jax.numpy as jnp
from jax import lax
from jax.experimental import pallas as pl
from jax.experimental.pallas import tpu as pltpu

LANES_V7X = 128
SUBLANES_BF16_V7X = 16
VMEM_LIMIT_BYTES_V7X = 56 * 1024 * 1024

N_META = 16
CHUNK = 64
ROPE_THETA = 10000.0
RMS_EPS = 1e-6
HEAD_DIM = 64
GROUP = 512
KEY_BLOCK = 256
NEG_BIG = -1e30
LOG2_E = 1.4426950408889634
EXP_UNDERFLOW = -104.0
LAMBDA_INIT = 0.8 - 0.6 * float(np.exp(-0.3 * 0))


def _rms(x, g):
    return x * lax.rsqrt(jnp.mean(x * x, axis=-1, keepdims=True) + RMS_EPS) * g


def _proj_kernel(x_ref, g_ref, w_ref, cos_ref, sin_ref, *out_refs, meta):
    u = _rms(x_ref[...], g_ref[...]).astype(jnp.bfloat16)

    def col(c):
        return jnp.dot(u, w_ref[:, c * GROUP:(c + 1) * GROUP], preferred_element_type=jnp.float32)

    def col_t(c, rows):
        v = col(c)
        if rows > v.shape[0]:
            v = _pad_rows(v, rows)
        return v.astype(jnp.bfloat16).T

    def silu(g):
        return g / (1.0 + jnp.exp(-g))

    def rope(t):
        cos = jnp.concatenate([cos_ref[...]] * (GROUP // LANES_V7X), axis=1)
        sin = jnp.concatenate([sin_ref[...]] * (GROUP // LANES_V7X), axis=1)
        parts = [pltpu.roll(t[:, j * LANES_V7X:(j + 1) * LANES_V7X], LANES_V7X // 2, 1)
                 for j in range(GROUP // LANES_V7X)]
        return t * cos + jnp.concatenate(parts, axis=1) * sin

    scale = HEAD_DIM ** -0.5
    if meta:
        sbk_ref, sbvt_ref, dk_ref, dvt_ref = out_refs
        sbk_ref[...] = col(1).astype(jnp.bfloat16)
        sbvt_ref[...] = col_t(2, LANES_V7X)
        dk_ref[...] = rope(col(5)).astype(jnp.bfloat16)
        dvt_ref[...] = col_t(6, LANES_V7X)
        return
    sbq_ref, sbk_ref, sbvt_ref, sbg_ref, dq_ref, dk_ref, dvt_ref, dg_ref = out_refs
    sbq_ref[...] = (col(0) * scale).astype(jnp.bfloat16)
    sbk_ref[...] = col(1).astype(jnp.bfloat16)
    sbg_ref[...] = silu(col(3))
    dq_ref[...] = (rope(col(4)) * scale).astype(jnp.bfloat16)
    dk_ref[...] = rope(col(5)).astype(jnp.bfloat16)
    dg_ref[...] = silu(col(7))
    for c, ref in ((2, sbvt_ref), (6, dvt_ref)):
        vt = col_t(c, 0)
        for j in range(ref.shape[0]):
            ref[j] = vt[:, j * KEY_BLOCK:(j + 1) * KEY_BLOCK]


def _project(x2d, gain, wb, cos, sin, tm, meta):
    rows, d = x2d.shape
    n_pos_tiles = cos.shape[0] // tm
    row_spec = lambda width: pl.BlockSpec((tm, width), lambda i: (i, 0))
    tab_spec = pl.BlockSpec((tm, LANES_V7X), lambda i: (i % n_pos_tiles, 0))
    bf, f32 = jnp.bfloat16, jnp.float32
    row_out = lambda dt: (row_spec(GROUP), jax.ShapeDtypeStruct((rows, GROUP), dt))
    if meta:
        vt_out = (pl.BlockSpec((GROUP, LANES_V7X), lambda i: (0, 0)),
                  jax.ShapeDtypeStruct((GROUP, LANES_V7X), bf))
        outs = [row_out(bf), vt_out, row_out(bf), vt_out]
    else:
        vt_out = (pl.BlockSpec((tm // KEY_BLOCK, GROUP, KEY_BLOCK), lambda i: (i, 0, 0)),
                  jax.ShapeDtypeStruct((rows // KEY_BLOCK, GROUP, KEY_BLOCK), bf))
        outs = [row_out(bf), row_out(bf), vt_out, row_out(f32),
                row_out(bf), row_out(bf), vt_out, row_out(f32)]
    return pl.pallas_call(
        functools.partial(_proj_kernel, meta=meta),
        grid=(rows // tm,),
        in_specs=[row_spec(d),
                  pl.BlockSpec((1, d), lambda i: (0, 0)),
                  pl.BlockSpec(wb.shape, lambda i: (0, 0)),
                  tab_spec, tab_spec],
        out_specs=[o[0] for o in outs],
        out_shape=[o[1] for o in outs],
        compiler_params=pltpu.CompilerParams(
            dimension_semantics=("parallel",), vmem_limit_bytes=VMEM_LIMIT_BYTES_V7X),
        name="proj_meta" if meta else "proj",
    )(x2d, gain, wb, cos, sin)


_NT = (((1,), (1,)), ((), ()))


def _masked_queries(q_ref, p, keep):
    q = q_ref[0, :, p * LANES_V7X:(p + 1) * LANES_V7X].astype(jnp.float32)
    return jnp.where(keep, q, 0.0).astype(jnp.bfloat16)


def _pad_rows(a, rows):
    return jnp.concatenate([a, jnp.zeros((rows - a.shape[0], a.shape[1]), a.dtype)], axis=0)


def _sb_kernel(q_ref, k_ref, vt_ref, km_ref, vmt_ref, g_ref, tri_ref, trim_ref, o_ref,
               acc_ref, car_ref, z_ref, *, qb, tiles):
    i = pl.program_id(2)
    lane = lax.broadcasted_iota(jnp.int32, (1, LANES_V7X), 1)
    key = lax.broadcasted_iota(jnp.int32, (KEY_BLOCK, qb), 0)
    qry = lax.broadcasted_iota(jnp.int32, (KEY_BLOCK, qb), 1)
    strict = key < qry
    chains = [(p, hh) for p in range(tiles) for hh in range(2)]
    qms = [_masked_queries(q_ref, p, (lane // HEAD_DIM) == hh) for p, hh in chains]

    def log_sigmoids(z, mask):
        log_beta = jnp.minimum(z, 0.0) - jnp.log(1.0 + jnp.exp2(jnp.abs(z) * (-LOG2_E)))
        log_keep = log_beta - z
        if mask is not None:
            log_keep = jnp.where(mask, log_keep, 0.0)
        return log_beta, log_keep.astype(jnp.bfloat16)

    def scores(n, p, start):
        kblk = k_ref[0, pl.ds(start, KEY_BLOCK), p * LANES_V7X:(p + 1) * LANES_V7X]
        z_ref[n] = lax.dot_general(kblk, qms[n], _NT, preferred_element_type=jnp.float32)

    def step(next_start, vtblk, mask, first):
        log_betas, sums = {}, {}

        def stage1(n, p):
            log_beta, log_keep = log_sigmoids(z_ref[n], mask)
            sums[n] = jnp.dot(tri_ref[...], log_keep, preferred_element_type=jnp.float32)
            log_betas[n] = log_beta
            scores(n, p, next_start)

        def stage2(n, p, hh):
            w = jnp.exp(log_betas.pop(n) + sums[n][:KEY_BLOCK])
            tot = sums.pop(n)[KEY_BLOCK:KEY_BLOCK + 1]
            if mask is not None:
                w = jnp.where(mask, w, 0.0)
            pv = jnp.dot(vtblk(p, hh), w.astype(jnp.bfloat16), preferred_element_type=jnp.float32)
            if first:
                acc_ref[n] = pv
                car_ref[n] = tot
            else:
                acc_ref[n] += pv * jnp.exp(car_ref[n])
                car_ref[n] += tot

        for n, (p, hh) in enumerate(chains):
            stage1(n, p)
        for n, (p, hh) in enumerate(chains):
            stage2(n, p, hh)

    def vt_at(j):
        return lambda p, hh: vt_ref[0, j, pl.ds(p * LANES_V7X + hh * HEAD_DIM, HEAD_DIM), :]

    def block_start(j):
        return pl.multiple_of(jnp.maximum(j, 0) * KEY_BLOCK, KEY_BLOCK)

    for n, (p, hh) in enumerate(chains):
        scores(n, p, block_start(i))
    step(block_start(i - 1), vt_at(i), strict, True)

    def largest_carry():
        return jnp.max(functools.reduce(jnp.maximum, [car_ref[n] for n in range(len(chains))]))

    def live(state):
        t, carry_max = state
        return jnp.logical_and(t < i, carry_max > EXP_UNDERFLOW)

    def body(state):
        t, _ = state
        j = i - 1 - t
        step(block_start(j - 1), vt_at(j), None, False)
        return t + 1, largest_carry()

    _, carry_max = lax.while_loop(live, body, (jnp.int32(0), largest_carry()))

    @pl.when(carry_max > EXP_UNDERFLOW)
    def _meta():
        zs = [lax.dot_general(km_ref[:, p * LANES_V7X:(p + 1) * LANES_V7X], qms[n], _NT,
                              preferred_element_type=jnp.float32)
              for n, (p, hh) in enumerate(chains)]
        log_betas, sticks = [], []
        for z in zs:
            log_beta, log_keep = log_sigmoids(z, None)
            sticks.append(jnp.dot(trim_ref[...], _pad_rows(log_keep, LANES_V7X),
                                  preferred_element_type=jnp.float32))
            log_betas.append(log_beta)
        ws = [_pad_rows(jnp.exp(log_betas[n] + sticks[n] + car_ref[n]).astype(jnp.bfloat16),
                        LANES_V7X) for n in range(len(chains))]
        for n, (p, hh) in enumerate(chains):
            acc_ref[n] += jnp.dot(vmt_ref[pl.ds(p * LANES_V7X + hh * HEAD_DIM, HEAD_DIM), :],
                                  ws[n], preferred_element_type=jnp.float32)

    for p in range(tiles):
        o = jnp.concatenate([acc_ref[2 * p], acc_ref[2 * p + 1]], axis=0).T
        o_ref[0, :, p * LANES_V7X:(p + 1) * LANES_V7X] = (
            o * g_ref[0, :, p * LANES_V7X:(p + 1) * LANES_V7X]).astype(o_ref.dtype)


def _attn_specs(b, s, width, qb, tiles):
    tw = tiles * LANES_V7X
    qspec = pl.BlockSpec((1, qb, tw), lambda bi, p, i: (bi, i, p))
    kspec = pl.BlockSpec((1, s, tw), lambda bi, p, i: (bi, 0, p))
    vtspec = pl.BlockSpec((1, s // KEY_BLOCK, tw, KEY_BLOCK), lambda bi, p, i: (bi, 0, p, 0))
    kmspec = pl.BlockSpec((N_META, tw), lambda bi, p, i: (0, p))
    vmtspec = pl.BlockSpec((tw, LANES_V7X), lambda bi, p, i: (p, 0))
    grid = (b, width // tw, s // qb)
    return grid, qspec, kspec, vtspec, kmspec, vmtspec


def _sb_attention(q, k, vt, km, vmt, g, tri, trim, qb, tiles):
    b, s, width = q.shape
    grid, qspec, kspec, vtspec, kmspec, vmtspec = _attn_specs(b, s, width, qb, tiles)
    full = lambda a: pl.BlockSpec(a.shape, lambda bi, p, i: (0,) * a.ndim)
    return pl.pallas_call(
        functools.partial(_sb_kernel, qb=qb, tiles=tiles),
        grid=grid,
        in_specs=[qspec, kspec, vtspec, kmspec, vmtspec, qspec, full(tri), full(trim)],
        out_specs=qspec,
        out_shape=jax.ShapeDtypeStruct((b, s, width), jnp.bfloat16),
        scratch_shapes=[pltpu.VMEM((2 * tiles, HEAD_DIM, qb), jnp.float32),
                        pltpu.VMEM((2 * tiles, 1, qb), jnp.float32),
                        pltpu.VMEM((2 * tiles, KEY_BLOCK, qb), jnp.float32)],
        compiler_params=pltpu.CompilerParams(
            dimension_semantics=("parallel", "parallel", "arbitrary"),
            vmem_limit_bytes=VMEM_LIMIT_BYTES_V7X),
        name="sb_attn",
    )(q, k, vt, km, vmt, g, tri, trim)


def _diff_kernel(q_ref, k_ref, vt_ref, km_ref, vmt_ref, g_ref,
                 lq1_ref, lk1_ref, lq2_ref, lk2_ref, sub_ref, o_ref, acc_ref, m_ref, l_ref, z_ref,
                 *, qb, tiles):
    i = pl.program_id(2)
    lane = lax.broadcasted_iota(jnp.int32, (1, LANES_V7X), 1)
    key = lax.broadcasted_iota(jnp.int32, (KEY_BLOCK, qb), 0)
    qry = lax.broadcasted_iota(jnp.int32, (KEY_BLOCK, qb), 1)
    chunk_ok = (qry // CHUNK) >= (key // CHUNK)
    chains = [(p, c) for p in range(tiles) for c in range(2)]
    qms = [_masked_queries(q_ref, p, ((lane // (HEAD_DIM // 2)) % 2) == c) for p, c in chains]
    ones = jnp.ones((SUBLANES_BF16_V7X, KEY_BLOCK), jnp.bfloat16)

    def update(n, s, vt_rows, pad_to, first):
        m_blk = jnp.max(s, axis=0, keepdims=True)
        m_new = m_blk if first else jnp.maximum(m_ref[n], m_blk)
        pr = jnp.exp(s - m_new).astype(jnp.bfloat16)
        if pad_to is not None:
            pr = _pad_rows(pr, pad_to)
        pv = jnp.dot(vt_rows, pr, preferred_element_type=jnp.float32)
        num, den = pv[:LANES_V7X], pv[LANES_V7X:LANES_V7X + 1]
        if first:
            l_ref[n] = den
            acc_ref[n] = num
        else:
            alpha = jnp.exp(m_ref[n] - m_new)
            l_ref[n] = l_ref[n] * alpha + den
            acc_ref[n] = acc_ref[n] * alpha + num
        m_ref[n] = m_new

    def scores(n, p, start):
        kblk = k_ref[0, pl.ds(start, KEY_BLOCK), p * LANES_V7X:(p + 1) * LANES_V7X]
        z_ref[n] = lax.dot_general(kblk, qms[n], _NT, preferred_element_type=jnp.float32)

    def block_start(j):
        return pl.multiple_of(jnp.maximum(jnp.minimum(j, i - 1), 0) * KEY_BLOCK, KEY_BLOCK)

    def step(j, next_start, mask, first):
        for n, (p, c) in enumerate(chains):
            s = z_ref[n] if mask is None else jnp.where(mask, z_ref[n], NEG_BIG)
            vt_rows = jnp.concatenate([vt_ref[0, j, p * LANES_V7X:(p + 1) * LANES_V7X, :], ones],
                                      axis=0)
            update(n, s, vt_rows, None, first)
            scores(n, p, next_start)

    for n, (p, c) in enumerate(chains):
        scores(n, p, pl.multiple_of(i * qb, qb))
    ss = [lax.dot_general(km_ref[:, p * LANES_V7X:(p + 1) * LANES_V7X], qms[n], _NT,
                          preferred_element_type=jnp.float32)
          for n, (p, c) in enumerate(chains)]
    step(i, block_start(0), chunk_ok, True)
    for n, (p, c) in enumerate(chains):
        vt_rows = jnp.concatenate([vmt_ref[p * LANES_V7X:(p + 1) * LANES_V7X, :],
                                   ones[:, :LANES_V7X]], axis=0)
        update(n, ss[n], vt_rows, LANES_V7X, False)

    def body(t, c):
        step(t, block_start(t + 1), None, False)
        return c

    lax.fori_loop(0, i, body, 0)

    lam = (jnp.exp(jnp.sum(lq1_ref[...] * lk1_ref[...], axis=-1, keepdims=True))
           - jnp.exp(jnp.sum(lq2_ref[...] * lk2_ref[...], axis=-1, keepdims=True))
           + LAMBDA_INIT)
    for p in range(tiles):
        ot = (acc_ref[2 * p] * (1.0 / l_ref[2 * p])
              - acc_ref[2 * p + 1] * (lam / l_ref[2 * p + 1]))
        ot = ot * lax.rsqrt(jnp.mean(ot * ot, axis=0, keepdims=True) + RMS_EPS)
        o = ot.T * (sub_ref[...] * (1.0 - LAMBDA_INIT))
        o_ref[0, :, p * LANES_V7X:(p + 1) * LANES_V7X] = (
            o * g_ref[0, :, p * LANES_V7X:(p + 1) * LANES_V7X]).astype(o_ref.dtype)


def _diff_attention(q, k, vt, km, vmt, g, lq1, lk1, lq2, lk2, sub, qb, tiles):
    b, s, width = q.shape
    grid, qspec, kspec, vtspec, kmspec, vmtspec = _attn_specs(b, s, width, qb, tiles)
    full = lambda a: pl.BlockSpec(a.shape, lambda bi, h, i: (0,) * a.ndim)
    stat = pltpu.VMEM((2 * tiles, 1, qb), jnp.float32)
    return pl.pallas_call(
        functools.partial(_diff_kernel, qb=qb, tiles=tiles),
        grid=grid,
        in_specs=[qspec, kspec, vtspec, kmspec, vmtspec, qspec,
                  full(lq1), full(lk1), full(lq2), full(lk2), full(sub)],
        out_specs=qspec,
        out_shape=jax.ShapeDtypeStruct((b, s, width), jnp.bfloat16),
        scratch_shapes=[pltpu.VMEM((2 * tiles, LANES_V7X, qb), jnp.float32), stat, stat,
                        pltpu.VMEM((2 * tiles, KEY_BLOCK, qb), jnp.float32)],
        compiler_params=pltpu.CompilerParams(
            dimension_semantics=("parallel", "parallel", "arbitrary"),
            vmem_limit_bytes=VMEM_LIMIT_BYTES_V7X),
        name="diff_attn",
    )(q, k, vt, km, vmt, g, lq1, lk1, lq2, lk2, sub)


def _out_kernel(a_ref, b_ref, w_ref, x_ref, g_ref, o_ref):
    half = a_ref.shape[1]
    h = (x_ref[...]
         + jnp.dot(a_ref[...], w_ref[:half, :], preferred_element_type=jnp.float32)
         + jnp.dot(b_ref[...], w_ref[half:, :], preferred_element_type=jnp.float32))
    o_ref[...] = _rms(h, g_ref[...])


def _out_project(mix_a, mix_b, wo, x2d, gain, tm):
    rows, d = x2d.shape
    return pl.pallas_call(
        _out_kernel,
        grid=(rows // tm,),
        in_specs=[pl.BlockSpec((tm, mix_a.shape[1]), lambda i: (i, 0)),
                  pl.BlockSpec((tm, mix_b.shape[1]), lambda i: (i, 0)),
                  pl.BlockSpec(wo.shape, lambda i: (0, 0)),
                  pl.BlockSpec((tm, d), lambda i: (i, 0)),
                  pl.BlockSpec((1, d), lambda i: (0, 0))],
        out_specs=pl.BlockSpec((tm, d), lambda i: (i, 0)),
        out_shape=jax.ShapeDtypeStruct((rows, d), jnp.float32),
        compiler_params=pltpu.CompilerParams(
            dimension_semantics=("parallel",), vmem_limit_bytes=VMEM_LIMIT_BYTES_V7X),
        name="out_proj",
    )(mix_a, mix_b, wo, x2d, gain)


def _rope_tables(length):
    inv = 1.0 / (ROPE_THETA ** (jnp.arange(0, HEAD_DIM, 2, dtype=jnp.float32) / HEAD_DIM))
    ang = jnp.arange(length, dtype=jnp.float32)[:, None] * inv[None, :]
    cos, sin = jnp.cos(ang), jnp.sin(ang)
    return (jnp.concatenate([cos, cos, cos, cos], axis=1),
            jnp.concatenate([-sin, -sin, sin, sin], axis=1))


def _forward(x, meta_tokens, norm_gain, w_in, w_out, lambda_q1, lambda_k1, lambda_q2, lambda_k2,
             subln_gain, final_norm_gain, *, qb, tm, tiles):
    assert qb == KEY_BLOCK and tm % KEY_BLOCK == 0
    b, s, d = x.shape
    w = w_in[0]
    half = HEAD_DIM // 2
    wqk = w[:, 4 * GROUP:6 * GROUP].reshape(d, 2 * GROUP // LANES_V7X, 2, 2, half)
    wqk = wqk.transpose(0, 1, 3, 2, 4).reshape(d, 2 * GROUP)
    wb = jnp.concatenate([w[:, :4 * GROUP], wqk, w[:, 6 * GROUP:]], axis=1).astype(jnp.bfloat16)
    wo = w_out[0].astype(jnp.bfloat16)
    cos, sin = _rope_tables(N_META + s)
    gain = norm_gain[0][None, :]
    x2d = x.reshape(b * s, d)

    px = _project(x2d, gain, wb, cos[N_META:], sin[N_META:], tm, False)
    sbk_m, sbvt_m, dk_m, dvt_m = _project(meta_tokens, gain, wb, cos[:N_META], sin[:N_META],
                                          N_META, True)
    sbq, sbk, sbvt, sbg, dq, dk, dvt, dg = px
    rows3 = lambda a: a.reshape(b, s, GROUP)
    blocks = lambda a: a.reshape(b, s // KEY_BLOCK, GROUP, KEY_BLOCK)

    kb = np.arange(KEY_BLOCK)
    later = (kb[None, :] > kb[:, None]).astype(np.float32)
    tri = np.concatenate([later, np.ones((SUBLANES_BF16_V7X, KEY_BLOCK), np.float32)], axis=0)
    trim = np.zeros((N_META, LANES_V7X), np.float32)
    trim[:, :N_META] = later[:N_META, :N_META]
    tri, trim = jnp.asarray(tri, jnp.bfloat16), jnp.asarray(trim, jnp.bfloat16)

    mix_sb = _sb_attention(rows3(sbq), rows3(sbk), blocks(sbvt), sbk_m, sbvt_m, rows3(sbg),
                           tri, trim, qb, tiles)
    mix_df = _diff_attention(rows3(dq), rows3(dk), blocks(dvt), dk_m, dvt_m, rows3(dg),
                             lambda_q1, lambda_k1, lambda_q2, lambda_k2, subln_gain, qb, tiles)
    tm_out = 2 * tm if (b * s) % (2 * tm) == 0 else tm
    y = _out_project(mix_sb.reshape(b * s, GROUP), mix_df.reshape(b * s, GROUP), wo, x2d,
                     final_norm_gain[None, :], tm_out)
    return y.reshape(b, s, d)


def kernel(x, meta_tokens, norm_gain, w_in, w_out, lambda_q1, lambda_k1, lambda_q2, lambda_k2,
           subln_gain, final_norm_gain):
    return _forward(x, meta_tokens, norm_gain, w_in, w_out, lambda_q1, lambda_k1, lambda_q2,
                    lambda_k2, subln_gain, final_norm_gain, qb=KEY_BLOCK, tm=512, tiles=4)
```

```python
import functools

import numpy as np
import jax
import jax.numpy as jnp
from jax import lax
from jax.experimental import pallas as pl
from jax.experimental.pallas import tpu as pltpu

LANES_V7X = 128
SUBLANES_BF16_V7X = 16
VMEM_LIMIT_BYTES_V7X = 56 * 1024 * 1024

N_META = 16
CHUNK = 64
ROPE_THETA = 10000.0
RMS_EPS = 1e-6
HEAD_DIM = 64
GROUP = 512
KEY_BLOCK = 256
NEG_BIG = -1e30
LOG2_E = 1.4426950408889634
LOOKAHEAD = 1
EXP_UNDERFLOW = -104.0
LAMBDA_INIT = 0.8 - 0.6 * float(np.exp(-0.3 * 0))


def _rms(x, g):
    return x * lax.rsqrt(jnp.mean(x * x, axis=-1, keepdims=True) + RMS_EPS) * g


def _proj_kernel(x_ref, g_ref, w_ref, cos_ref, sin_ref, *out_refs, meta):
    u = _rms(x_ref[...], g_ref[...]).astype(jnp.bfloat16)

    def col(c):
        return jnp.dot(u, w_ref[:, c * GROUP:(c + 1) * GROUP], preferred_element_type=jnp.float32)

    def col_t(c, rows):
        v = col(c)
        if rows > v.shape[0]:
            v = _pad_rows(v, rows)
        return v.astype(jnp.bfloat16).T

    def silu(g):
        return g / (1.0 + jnp.exp(-g))

    def rope(t):
        cos = jnp.concatenate([cos_ref[...]] * (GROUP // LANES_V7X), axis=1)
        sin = jnp.concatenate([sin_ref[...]] * (GROUP // LANES_V7X), axis=1)
        parts = [pltpu.roll(t[:, j * LANES_V7X:(j + 1) * LANES_V7X], LANES_V7X // 2, 1)
                 for j in range(GROUP // LANES_V7X)]
        return t * cos + jnp.concatenate(parts, axis=1) * sin

    scale = HEAD_DIM ** -0.5
    if meta:
        sbk_ref, sbvt_ref, dk_ref, dvt_ref = out_refs
        sbk_ref[...] = col(1).astype(jnp.bfloat16)
        sbvt_ref[...] = col_t(2, LANES_V7X)
        dk_ref[...] = rope(col(5)).astype(jnp.bfloat16)
        dvt_ref[...] = col_t(6, LANES_V7X)
        return
    sbq_ref, sbk_ref, sbvt_ref, sbg_ref, dq_ref, dk_ref, dvt_ref, dg_ref = out_refs
    sbq_ref[...] = (col(0) * scale).astype(jnp.bfloat16)
    sbk_ref[...] = col(1).astype(jnp.bfloat16)
    sbg_ref[...] = silu(col(3))
    dq_ref[...] = (rope(col(4)) * scale).astype(jnp.bfloat16)
    dk_ref[...] = rope(col(5)).astype(jnp.bfloat16)
    dg_ref[...] = silu(col(7))
    for c, ref in ((2, sbvt_ref), (6, dvt_ref)):
        vt = col_t(c, 0)
        for j in range(ref.shape[0]):
            ref[j] = vt[:, j * KEY_BLOCK:(j + 1) * KEY_BLOCK]


def _project(x2d, gain, wb, cos, sin, tm, meta):
    rows, d = x2d.shape
    n_pos_tiles = cos.shape[0] // tm
    row_spec = lambda width: pl.BlockSpec((tm, width), lambda i: (i, 0))
    tab_spec = pl.BlockSpec((tm, LANES_V7X), lambda i: (i % n_pos_tiles, 0))
    bf, f32 = jnp.bfloat16, jnp.float32
    row_out = lambda dt: (row_spec(GROUP), jax.ShapeDtypeStruct((rows, GROUP), dt))
    if meta:
        vt_out = (pl.BlockSpec((GROUP, LANES_V7X), lambda i: (0, 0)),
                  jax.ShapeDtypeStruct((GROUP, LANES_V7X), bf))
        outs = [row_out(bf), vt_out, row_out(bf), vt_out]
    else:
        vt_out = (pl.BlockSpec((tm // KEY_BLOCK, GROUP, KEY_BLOCK), lambda i: (i, 0, 0)),
                  jax.ShapeDtypeStruct((rows // KEY_BLOCK, GROUP, KEY_BLOCK), bf))
        outs = [row_out(bf), row_out(bf), vt_out, row_out(f32),
                row_out(bf), row_out(bf), vt_out, row_out(f32)]
    return pl.pallas_call(
        functools.partial(_proj_kernel, meta=meta),
        grid=(rows // tm,),
        in_specs=[row_spec(d),
                  pl.BlockSpec((1, d), lambda i: (0, 0)),
                  pl.BlockSpec(wb.shape, lambda i: (0, 0)),
                  tab_spec, tab_spec],
        out_specs=[o[0] for o in outs],
        out_shape=[o[1] for o in outs],
        compiler_params=pltpu.CompilerParams(
            dimension_semantics=("parallel",), vmem_limit_bytes=VMEM_LIMIT_BYTES_V7X),
        name="proj_meta" if meta else "proj",
    )(x2d, gain, wb, cos, sin)


_NT = (((1,), (1,)), ((), ()))


def _masked_queries(q_ref, p, keep):
    q = q_ref[0, :, p * LANES_V7X:(p + 1) * LANES_V7X].astype(jnp.float32)
    return jnp.where(keep, q, 0.0).astype(jnp.bfloat16)


def _pad_rows(a, rows):
    return jnp.concatenate([a, jnp.zeros((rows - a.shape[0], a.shape[1]), a.dtype)], axis=0)


def _sb_kernel(q_ref, k_ref, vt_ref, km_ref, vmt_ref, g_ref, tri_ref, trim_ref, o_ref,
               acc_ref, car_ref, z_ref, *, qb, tiles):
    i = pl.program_id(2)
    lane = lax.broadcasted_iota(jnp.int32, (1, LANES_V7X), 1)
    key = lax.broadcasted_iota(jnp.int32, (KEY_BLOCK, qb), 0)
    qry = lax.broadcasted_iota(jnp.int32, (KEY_BLOCK, qb), 1)
    strict = key < qry
    chains = [(p, hh) for p in range(tiles) for hh in range(2)]
    qms = [_masked_queries(q_ref, p, (lane // HEAD_DIM) == hh) for p, hh in chains]

    def log_sigmoids(z, mask):
        log_beta = jnp.minimum(z, 0.0) - jnp.log(1.0 + jnp.exp2(jnp.abs(z) * (-LOG2_E)))
        log_keep = log_beta - z
        if mask is not None:
            log_keep = jnp.where(mask, log_keep, 0.0)
        return log_beta, log_keep.astype(jnp.bfloat16)

    def scores(n, p, start):
        kblk = k_ref[0, pl.ds(start, KEY_BLOCK), p * LANES_V7X:(p + 1) * LANES_V7X]
        z_ref[n] = lax.dot_general(kblk, qms[n], _NT, preferred_element_type=jnp.float32)

    half = KEY_BLOCK // 2
    strict_half = strict[:half, :half]

    def split(a, diagonal):
        return [a[:half, :half], a[:half, half:], a[half:, half:]] if diagonal else [a]

    def join(parts, diagonal):
        if not diagonal:
            return parts[0]
        older = jnp.concatenate(parts[:2], axis=1)
        newer = jnp.concatenate([jnp.zeros_like(parts[2]), parts[2]], axis=1)
        return jnp.concatenate([older, newer], axis=0)

    def step(next_start, vtblk, diagonal, exists=None):
        masks = [strict_half, None, strict_half] if diagonal else [None]
        log_betas, sums = {}, {}

        def stage1(n, p):
            parts = [log_sigmoids(z, m) for z, m in zip(split(z_ref[n], diagonal), masks)]
            log_betas[n] = [lb for lb, _ in parts]
            log_keep = join([lk for _, lk in parts], diagonal)
            sums[n] = jnp.dot(tri_ref[...], log_keep, preferred_element_type=jnp.float32)
            scores(n, p, next_start)

        def stage2(n, p, hh):
            ws = []
            for lb, st, m in zip(log_betas.pop(n), split(sums[n][:KEY_BLOCK], diagonal), masks):
                w = jnp.exp(lb + st)
                ws.append((w if m is None else jnp.where(m, w, 0.0)).astype(jnp.bfloat16))
            tot = sums.pop(n)[KEY_BLOCK:KEY_BLOCK + 1]
            pv = jnp.dot(vtblk(p, hh), join(ws, diagonal), preferred_element_type=jnp.float32)
            if diagonal:
                acc_ref[n] = pv
                car_ref[n] = tot
            else:
                factor = jnp.exp(car_ref[n])
                if exists is not None:
                    factor, tot = factor * exists, tot * exists
                acc_ref[n] += pv * factor
                car_ref[n] += tot

        for n, (p, hh) in enumerate(chains):
            stage1(n, p)
        for n, (p, hh) in enumerate(chains):
            stage2(n, p, hh)

    def vt_at(j):
        return lambda p, hh: vt_ref[0, j, pl.ds(p * LANES_V7X + hh * HEAD_DIM, HEAD_DIM), :]

    def block_start(j):
        return pl.multiple_of(jnp.maximum(j, 0) * KEY_BLOCK, KEY_BLOCK)

    for n, (p, hh) in enumerate(chains):
        scores(n, p, block_start(i))
    step(block_start(i - 1), vt_at(i), True)
    step(block_start(i - 2), vt_at(jnp.maximum(i - 1, 0)), False,
         exists=jnp.where(i > 0, 1.0, 0.0).astype(jnp.float32))

    def largest_carry():
        return jnp.max(functools.reduce(jnp.maximum, [car_ref[n] for n in range(len(chains))]))

    def live(state):
        t, carry_max = state
        return jnp.logical_and(t < i, carry_max > EXP_UNDERFLOW)

    def body(state):
        t, _ = state
        j = i - 1 - t
        step(block_start(j - 1), vt_at(j), False)
        return t + 1, largest_carry()

    _, carry_max = lax.while_loop(live, body, (jnp.int32(1), largest_carry()))

    @pl.when(carry_max > EXP_UNDERFLOW)
    def _meta():
        zs = [lax.dot_general(km_ref[:, p * LANES_V7X:(p + 1) * LANES_V7X], qms[n], _NT,
                              preferred_element_type=jnp.float32)
              for n, (p, hh) in enumerate(chains)]
        log_betas, sticks = [], []
        for z in zs:
            log_beta, log_keep = log_sigmoids(z, None)
            sticks.append(jnp.dot(trim_ref[...], _pad_rows(log_keep, LANES_V7X),
                                  preferred_element_type=jnp.float32))
            log_betas.append(log_beta)
        ws = [_pad_rows(jnp.exp(log_betas[n] + sticks[n] + car_ref[n]).astype(jnp.bfloat16),
                        LANES_V7X) for n in range(len(chains))]
        for n, (p, hh) in enumerate(chains):
            acc_ref[n] += jnp.dot(vmt_ref[pl.ds(p * LANES_V7X + hh * HEAD_DIM, HEAD_DIM), :],
                                  ws[n], preferred_element_type=jnp.float32)

    for p in range(tiles):
        o = jnp.concatenate([acc_ref[2 * p], acc_ref[2 * p + 1]], axis=0).T
        o_ref[0, :, p * LANES_V7X:(p + 1) * LANES_V7X] = (
            o * g_ref[0, :, p * LANES_V7X:(p + 1) * LANES_V7X]).astype(o_ref.dtype)


def _attn_specs(b, s, width, qb, tiles):
    tw = tiles * LANES_V7X
    qspec = pl.BlockSpec((1, qb, tw), lambda bi, p, i: (bi, i, p))
    kspec = pl.BlockSpec((1, s, tw), lambda bi, p, i: (bi, 0, p))
    vtspec = pl.BlockSpec((1, s // KEY_BLOCK, tw, KEY_BLOCK), lambda bi, p, i: (bi, 0, p, 0))
    kmspec = pl.BlockSpec((N_META, tw), lambda bi, p, i: (0, p))
    vmtspec = pl.BlockSpec((tw, LANES_V7X), lambda bi, p, i: (p, 0))
    grid = (b, width // tw, s // qb)
    return grid, qspec, kspec, vtspec, kmspec, vmtspec


def _sb_attention(q, k, vt, km, vmt, g, tri, trim, qb, tiles):
    b, s, width = q.shape
    grid, qspec, kspec, vtspec, kmspec, vmtspec = _attn_specs(b, s, width, qb, tiles)
    full = lambda a: pl.BlockSpec(a.shape, lambda bi, p, i: (0,) * a.ndim)
    return pl.pallas_call(
        functools.partial(_sb_kernel, qb=qb, tiles=tiles),
        grid=grid,
        in_specs=[qspec, kspec, vtspec, kmspec, vmtspec, qspec, full(tri), full(trim)],
        out_specs=qspec,
        out_shape=jax.ShapeDtypeStruct((b, s, width), jnp.bfloat16),
        scratch_shapes=[pltpu.VMEM((2 * tiles, HEAD_DIM, qb), jnp.float32),
                        pltpu.VMEM((2 * tiles, 1, qb), jnp.float32),
                        pltpu.VMEM((2 * tiles, KEY_BLOCK, qb), jnp.float32)],
        compiler_params=pltpu.CompilerParams(
            dimension_semantics=("parallel", "parallel", "arbitrary"),
            vmem_limit_bytes=VMEM_LIMIT_BYTES_V7X),
        name="sb_attn",
    )(q, k, vt, km, vmt, g, tri, trim)


def _diff_kernel(q_ref, k_ref, vt_ref, km_ref, vmt_ref, g_ref,
                 lq1_ref, lk1_ref, lq2_ref, lk2_ref, sub_ref, o_ref, acc_ref, m_ref, l_ref, z_ref,
                 *, qb, tiles):
    i = pl.program_id(2)
    lane = lax.broadcasted_iota(jnp.int32, (1, LANES_V7X), 1)
    key = lax.broadcasted_iota(jnp.int32, (KEY_BLOCK, qb), 0)
    qry = lax.broadcasted_iota(jnp.int32, (KEY_BLOCK, qb), 1)
    chunk_ok = (qry // CHUNK) >= (key // CHUNK)
    chains = [(p, c) for p in range(tiles) for c in range(2)]
    qms = [_masked_queries(q_ref, p, ((lane // (HEAD_DIM // 2)) % 2) == c) for p, c in chains]
    ones = jnp.ones((SUBLANES_BF16_V7X, KEY_BLOCK), jnp.bfloat16)

    def update(n, s, vt_rows, pad_to, first):
        m_blk = jnp.max(s, axis=0, keepdims=True)
        m_new = m_blk if first else jnp.maximum(m_ref[n], m_blk)
        pr = jnp.exp(s - m_new).astype(jnp.bfloat16)
        if pad_to is not None:
            pr = _pad_rows(pr, pad_to)
        pv = jnp.dot(vt_rows, pr, preferred_element_type=jnp.float32)
        num, den = pv[:LANES_V7X], pv[LANES_V7X:LANES_V7X + 1]
        if first:
            l_ref[n] = den
            acc_ref[n] = num
        else:
            alpha = jnp.exp(m_ref[n] - m_new)
            l_ref[n] = l_ref[n] * alpha + den
            acc_ref[n] = acc_ref[n] * alpha + num
        m_ref[n] = m_new

    def scores(n, p, start):
        kblk = k_ref[0, pl.ds(start, KEY_BLOCK), p * LANES_V7X:(p + 1) * LANES_V7X]
        z_ref[n] = lax.dot_general(kblk, qms[n], _NT, preferred_element_type=jnp.float32)

    def block_start(j):
        return pl.multiple_of(jnp.maximum(jnp.minimum(j, i - 1), 0) * KEY_BLOCK, KEY_BLOCK)

    def step(j, next_start, mask, first):
        def next_scores(n):
            p = chains[n][0]
            kblk = k_ref[0, pl.ds(next_start, KEY_BLOCK), p * LANES_V7X:(p + 1) * LANES_V7X]
            return lax.dot_general(kblk, qms[n], _NT, preferred_element_type=jnp.float32)

        ahead = {n: next_scores(n) for n in range(LOOKAHEAD)}
        for n, (p, c) in enumerate(chains):
            if n + LOOKAHEAD < len(chains):
                ahead[n + LOOKAHEAD] = next_scores(n + LOOKAHEAD)
            s = z_ref[n] if mask is None else jnp.where(mask, z_ref[n], NEG_BIG)
            vt_rows = jnp.concatenate([vt_ref[0, j, p * LANES_V7X:(p + 1) * LANES_V7X, :], ones],
                                      axis=0)
            update(n, s, vt_rows, None, first)
            z_ref[n] = ahead.pop(n)

    for n, (p, c) in enumerate(chains):
        scores(n, p, pl.multiple_of(i * qb, qb))
    ss = [lax.dot_general(km_ref[:, p * LANES_V7X:(p + 1) * LANES_V7X], qms[n], _NT,
                          preferred_element_type=jnp.float32)
          for n, (p, c) in enumerate(chains)]
    step(i, block_start(0), chunk_ok, True)
    for n, (p, c) in enumerate(chains):
        vt_rows = jnp.concatenate([vmt_ref[p * LANES_V7X:(p + 1) * LANES_V7X, :],
                                   ones[:, :LANES_V7X]], axis=0)
        update(n, ss[n], vt_rows, LANES_V7X, False)

    def body(t, c):
        step(t, block_start(t + 1), None, False)
        return c

    lax.fori_loop(0, i, body, 0)

    lam = (jnp.exp(jnp.sum(lq1_ref[...] * lk1_ref[...], axis=-1, keepdims=True))
           - jnp.exp(jnp.sum(lq2_ref[...] * lk2_ref[...], axis=-1, keepdims=True))
           + LAMBDA_INIT)
    for p in range(tiles):
        ot = (acc_ref[2 * p] * (1.0 / l_ref[2 * p])
              - acc_ref[2 * p + 1] * (lam / l_ref[2 * p + 1]))
        ot = ot * lax.rsqrt(jnp.mean(ot * ot, axis=0, keepdims=True) + RMS_EPS)
        o = ot.T * (sub_ref[...] * (1.0 - LAMBDA_INIT))
        o_ref[0, :, p * LANES_V7X:(p + 1) * LANES_V7X] = (
            o * g_ref[0, :, p * LANES_V7X:(p + 1) * LANES_V7X]).astype(o_ref.dtype)


def _diff_attention(q, k, vt, km, vmt, g, lq1, lk1, lq2, lk2, sub, qb, tiles):
    b, s, width = q.shape
    grid, qspec, kspec, vtspec, kmspec, vmtspec = _attn_specs(b, s, width, qb, tiles)
    full = lambda a: pl.BlockSpec(a.shape, lambda bi, h, i: (0,) * a.ndim)
    stat = pltpu.VMEM((2 * tiles, 1, qb), jnp.float32)
    return pl.pallas_call(
        functools.partial(_diff_kernel, qb=qb, tiles=tiles),
        grid=grid,
        in_specs=[qspec, kspec, vtspec, kmspec, vmtspec, qspec,
                  full(lq1), full(lk1), full(lq2), full(lk2), full(sub)],
        out_specs=qspec,
        out_shape=jax.ShapeDtypeStruct((b, s, width), jnp.bfloat16),
        scratch_shapes=[pltpu.VMEM((2 * tiles, LANES_V7X, qb), jnp.float32), stat, stat,
                        pltpu.VMEM((2 * tiles, KEY_BLOCK, qb), jnp.float32)],
        compiler_params=pltpu.CompilerParams(
            dimension_semantics=("parallel", "parallel", "arbitrary"),
            vmem_limit_bytes=VMEM_LIMIT_BYTES_V7X),
        name="diff_attn",
    )(q, k, vt, km, vmt, g, lq1, lk1, lq2, lk2, sub)


def _out_kernel(a_ref, b_ref, w_ref, x_ref, g_ref, o_ref):
    half = a_ref.shape[1]
    h = (x_ref[...]
         + jnp.dot(a_ref[...], w_ref[:half, :], preferred_element_type=jnp.float32)
         + jnp.dot(b_ref[...], w_ref[half:, :], preferred_element_type=jnp.float32))
    o_ref[...] = _rms(h, g_ref[...])


def _out_project(mix_a, mix_b, wo, x2d, gain, tm):
    rows, d = x2d.shape
    return pl.pallas_call(
        _out_kernel,
        grid=(rows // tm,),
        in_specs=[pl.BlockSpec((tm, mix_a.shape[1]), lambda i: (i, 0)),
                  pl.BlockSpec((tm, mix_b.shape[1]), lambda i: (i, 0)),
                  pl.BlockSpec(wo.shape, lambda i: (0, 0)),
                  pl.BlockSpec((tm, d), lambda i: (i, 0)),
                  pl.BlockSpec((1, d), lambda i: (0, 0))],
        out_specs=pl.BlockSpec((tm, d), lambda i: (i, 0)),
        out_shape=jax.ShapeDtypeStruct((rows, d), jnp.float32),
        compiler_params=pltpu.CompilerParams(
            dimension_semantics=("parallel",), vmem_limit_bytes=VMEM_LIMIT_BYTES_V7X),
        name="out_proj",
    )(mix_a, mix_b, wo, x2d, gain)


def _rope_tables(first, length):
    f32 = np.float32
    inv = f32(1.0) / (f32(ROPE_THETA) ** (np.arange(0, HEAD_DIM, 2, dtype=f32) / f32(HEAD_DIM)))
    ang = np.arange(first, first + length, dtype=f32)[:, None] * inv[None, :]
    cos, sin = np.cos(ang).astype(f32), np.sin(ang).astype(f32)
    return (jnp.asarray(np.concatenate([cos, cos, cos, cos], axis=1)),
            jnp.asarray(np.concatenate([-sin, -sin, sin, sin], axis=1)))


def _forward(x, meta_tokens, norm_gain, w_in, w_out, lambda_q1, lambda_k1, lambda_q2, lambda_k2,
             subln_gain, final_norm_gain, *, qb, tm, tiles):
    assert qb == KEY_BLOCK and tm % KEY_BLOCK == 0
    b, s, d = x.shape
    w = w_in[0]
    half = HEAD_DIM // 2
    wqk = w[:, 4 * GROUP:6 * GROUP].reshape(d, 2 * GROUP // LANES_V7X, 2, 2, half)
    wqk = wqk.transpose(0, 1, 3, 2, 4).reshape(d, 2 * GROUP)
    wb = jnp.concatenate([w[:, :4 * GROUP], wqk, w[:, 6 * GROUP:]], axis=1).astype(jnp.bfloat16)
    wo = w_out[0].astype(jnp.bfloat16)
    gain = norm_gain[0][None, :]
    x2d = x.reshape(b * s, d)

    px = _project(x2d, gain, wb, *_rope_tables(N_META, s), tm, False)
    sbk_m, sbvt_m, dk_m, dvt_m = _project(meta_tokens, gain, wb, *_rope_tables(0, N_META),
                                          N_META, True)
    sbq, sbk, sbvt, sbg, dq, dk, dvt, dg = px
    rows3 = lambda a: a.reshape(b, s, GROUP)
    blocks = lambda a: a.reshape(b, s // KEY_BLOCK, GROUP, KEY_BLOCK)

    kb = np.arange(KEY_BLOCK)
    later = (kb[None, :] > kb[:, None]).astype(np.float32)
    tri = np.concatenate([later, np.ones((SUBLANES_BF16_V7X, KEY_BLOCK), np.float32)], axis=0)
    trim = np.zeros((N_META, LANES_V7X), np.float32)
    trim[:, :N_META] = later[:N_META, :N_META]
    tri, trim = jnp.asarray(tri, jnp.bfloat16), jnp.asarray(trim, jnp.bfloat16)

    mix_sb = _sb_attention(rows3(sbq), rows3(sbk), blocks(sbvt), sbk_m, sbvt_m, rows3(sbg),
                           tri, trim, qb, tiles)
    mix_df = _diff_attention(rows3(dq), rows3(dk), blocks(dvt), dk_m, dvt_m, rows3(dg),
                             lambda_q1, lambda_k1, lambda_q2, lambda_k2, subln_gain, qb, tiles)
    tm_out = 2 * tm if (b * s) % (2 * tm) == 0 else tm
    y = _out_project(mix_sb.reshape(b * s, GROUP), mix_df.reshape(b * s, GROUP), wo, x2d,
                     final_norm_gain[None, :], tm_out)
    return y.reshape(b, s, d)


def kernel(x, meta_tokens, norm_gain, w_in, w_out, lambda_q1, lambda_k1, lambda_q2, lambda_k2,
           subln_gain, final_norm_gain):
    return _forward(x, meta_tokens, norm_gain, w_in, w_out, lambda_q1, lambda_k1, lambda_q2,
                    lambda_k2, subln_gain, final_norm_gain, qb=KEY_BLOCK, tm=512, tiles=4)
```

```python
import functools

import numpy as np
import jax
import jax.numpy as jnp
from jax import lax
from jax.experimental import pallas as pl
from jax.experimental.pallas import tpu as pltpu

LANES_V7X = 128
SUBLANES_BF16_V7X = 16
VMEM_LIMIT_BYTES_V7X = 56 * 1024 * 1024

N_META = 16
CHUNK = 64
ROPE_THETA = 10000.0
RMS_EPS = 1e-6
HEAD_DIM = 64
GROUP = 512
KEY_BLOCK = 256
NEG_BIG = -1e30
LOG2_E = 1.4426950408889634
LOOKAHEAD = 1
EXP_UNDERFLOW = -104.0
LAMBDA_INIT = 0.8 - 0.6 * float(np.exp(-0.3 * 0))


def _rms(x, g):
    return x * lax.rsqrt(jnp.mean(x * x, axis=-1, keepdims=True) + RMS_EPS) * g


def _proj_kernel(x_ref, g_ref, w_ref, cos_ref, sin_ref, *out_refs, meta):
    u = _rms(x_ref[...], g_ref[...]).astype(jnp.bfloat16)

    def col(c):
        return jnp.dot(u, w_ref[:, c * GROUP:(c + 1) * GROUP], preferred_element_type=jnp.float32)

    def col_t(c, rows):
        v = col(c)
        if rows > v.shape[0]:
            v = _pad_rows(v, rows)
        return v.astype(jnp.bfloat16).T

    def silu(g):
        return g / (1.0 + jnp.exp(-g))

    def rope(t):
        cos = jnp.concatenate([cos_ref[...]] * (GROUP // LANES_V7X), axis=1)
        sin = jnp.concatenate([sin_ref[...]] * (GROUP // LANES_V7X), axis=1)
        parts = [pltpu.roll(t[:, j * LANES_V7X:(j + 1) * LANES_V7X], LANES_V7X // 2, 1)
                 for j in range(GROUP // LANES_V7X)]
        return t * cos + jnp.concatenate(parts, axis=1) * sin

    scale = HEAD_DIM ** -0.5
    if meta:
        sbk_ref, sbvt_ref, dk_ref, dvt_ref = out_refs
        sbk_ref[...] = col(1).astype(jnp.bfloat16)
        sbvt_ref[...] = col_t(2, LANES_V7X)
        dk_ref[...] = rope(col(5)).astype(jnp.bfloat16)
        dvt_ref[...] = col_t(6, LANES_V7X)
        return
    sbq_ref, sbk_ref, sbvt_ref, sbg_ref, dq_ref, dk_ref, dvt_ref, dg_ref = out_refs
    sbq_ref[...] = (col(0) * scale).astype(jnp.bfloat16)
    sbk_ref[...] = col(1).astype(jnp.bfloat16)
    sbg_ref[...] = silu(col(3))
    dq_ref[...] = (rope(col(4)) * scale).astype(jnp.bfloat16)
    dk_ref[...] = rope(col(5)).astype(jnp.bfloat16)
    dg_ref[...] = silu(col(7))
    for c, ref in ((2, sbvt_ref), (6, dvt_ref)):
        vt = col_t(c, 0)
        for j in range(ref.shape[0]):
            ref[j] = vt[:, j * KEY_BLOCK:(j + 1) * KEY_BLOCK]


def _project(x2d, gain, wb, cos, sin, tm, meta):
    rows, d = x2d.shape
    n_pos_tiles = cos.shape[0] // tm
    row_spec = lambda width: pl.BlockSpec((tm, width), lambda i: (i, 0))
    tab_spec = pl.BlockSpec((tm, LANES_V7X), lambda i: (i % n_pos_tiles, 0))
    bf, f32 = jnp.bfloat16, jnp.float32
    row_out = lambda dt: (row_spec(GROUP), jax.ShapeDtypeStruct((rows, GROUP), dt))
    if meta:
        vt_out = (pl.BlockSpec((GROUP, LANES_V7X), lambda i: (0, 0)),
                  jax.ShapeDtypeStruct((GROUP, LANES_V7X), bf))
        outs = [row_out(bf), vt_out, row_out(bf), vt_out]
    else:
        vt_out = (pl.BlockSpec((tm // KEY_BLOCK, GROUP, KEY_BLOCK), lambda i: (i, 0, 0)),
                  jax.ShapeDtypeStruct((rows // KEY_BLOCK, GROUP, KEY_BLOCK), bf))
        outs = [row_out(bf), row_out(bf), vt_out, row_out(f32),
                row_out(bf), row_out(bf), vt_out, row_out(f32)]
    return pl.pallas_call(
        functools.partial(_proj_kernel, meta=meta),
        grid=(rows // tm,),
        in_specs=[row_spec(d),
                  pl.BlockSpec((1, d), lambda i: (0, 0)),
                  pl.BlockSpec(wb.shape, lambda i: (0, 0)),
                  tab_spec, tab_spec],
        out_specs=[o[0] for o in outs],
        out_shape=[o[1] for o in outs],
        compiler_params=pltpu.CompilerParams(
            dimension_semantics=("parallel",), vmem_limit_bytes=VMEM_LIMIT_BYTES_V7X),
        name="proj_meta" if meta else "proj",
    )(x2d, gain, wb, cos, sin)


_NT = (((1,), (1,)), ((), ()))


def _masked_queries(q_ref, block, qb, p, keep):
    rows = pl.ds(pl.multiple_of(block * qb, qb), qb)
    q = q_ref[0, rows, p * LANES_V7X:(p + 1) * LANES_V7X].astype(jnp.float32)
    return jnp.where(keep, q, 0.0).astype(jnp.bfloat16)


def _diagonal_scores(q_ref, k_ref, z_ref, block, qb, chains, keeps):
    start = pl.multiple_of(block * qb, qb)
    for n, (p, _) in enumerate(chains):
        kblk = k_ref[0, pl.ds(start, KEY_BLOCK), p * LANES_V7X:(p + 1) * LANES_V7X]
        z_ref[n] = lax.dot_general(kblk, _masked_queries(q_ref, block, qb, p, keeps[n]), _NT,
                                   preferred_element_type=jnp.float32)


def _pad_rows(a, rows):
    return jnp.concatenate([a, jnp.zeros((rows - a.shape[0], a.shape[1]), a.dtype)], axis=0)


def _sb_kernel(q_ref, k_ref, vt_ref, km_ref, vmt_ref, g_ref, tri_ref, trim_ref, o_ref,
               acc_ref, car_ref, z_ref, *, qb, tiles):
    i = pl.program_id(2)
    lane = lax.broadcasted_iota(jnp.int32, (1, LANES_V7X), 1)
    key = lax.broadcasted_iota(jnp.int32, (KEY_BLOCK, qb), 0)
    qry = lax.broadcasted_iota(jnp.int32, (KEY_BLOCK, qb), 1)
    strict = key < qry
    chains = [(p, hh) for p in range(tiles) for hh in range(2)]
    keeps = [(lane // HEAD_DIM) == hh for p, hh in chains]
    qms = [_masked_queries(q_ref, i, qb, p, keeps[n]) for n, (p, hh) in enumerate(chains)]

    def log_sigmoids(z, mask):
        log_beta = jnp.minimum(z, 0.0) - jnp.log(1.0 + jnp.exp2(jnp.abs(z) * (-LOG2_E)))
        log_keep = log_beta - z
        if mask is not None:
            log_keep = jnp.where(mask, log_keep, 0.0)
        return log_beta, log_keep.astype(jnp.bfloat16)

    def scores(n, p, start):
        kblk = k_ref[0, pl.ds(start, KEY_BLOCK), p * LANES_V7X:(p + 1) * LANES_V7X]
        z_ref[n] = lax.dot_general(kblk, qms[n], _NT, preferred_element_type=jnp.float32)

    half = KEY_BLOCK // 2
    strict_half = strict[:half, :half]

    def split(a, diagonal):
        return [a[:half, :half], a[:half, half:], a[half:, half:]] if diagonal else [a]

    def join(parts, diagonal):
        if not diagonal:
            return parts[0]
        older = jnp.concatenate(parts[:2], axis=1)
        newer = jnp.concatenate([jnp.zeros_like(parts[2]), parts[2]], axis=1)
        return jnp.concatenate([older, newer], axis=0)

    def step(next_start, vtblk, diagonal, exists=None):
        masks = [strict_half, None, strict_half] if diagonal else [None]
        log_betas, sums = {}, {}

        def stage1(n, p):
            parts = [log_sigmoids(z, m) for z, m in zip(split(z_ref[n], diagonal), masks)]
            log_betas[n] = [lb for lb, _ in parts]
            log_keep = join([lk for _, lk in parts], diagonal)
            sums[n] = jnp.dot(tri_ref[...], log_keep, preferred_element_type=jnp.float32)
            scores(n, p, next_start)

        def stage2(n, p, hh):
            ws = []
            for lb, st, m in zip(log_betas.pop(n), split(sums[n][:KEY_BLOCK], diagonal), masks):
                w = jnp.exp(lb + st)
                ws.append((w if m is None else jnp.where(m, w, 0.0)).astype(jnp.bfloat16))
            tot = sums.pop(n)[KEY_BLOCK:KEY_BLOCK + 1]
            pv = jnp.dot(vtblk(p, hh), join(ws, diagonal), preferred_element_type=jnp.float32)
            if diagonal:
                acc_ref[n] = pv
                car_ref[n] = tot
            else:
                factor = jnp.exp(car_ref[n])
                if exists is not None:
                    factor, tot = factor * exists, tot * exists
                acc_ref[n] += pv * factor
                car_ref[n] += tot

        for n, (p, hh) in enumerate(chains):
            stage1(n, p)
        for n, (p, hh) in enumerate(chains):
            stage2(n, p, hh)

    def vt_at(j):
        return lambda p, hh: vt_ref[0, j, pl.ds(p * LANES_V7X + hh * HEAD_DIM, HEAD_DIM), :]

    def block_start(j):
        return pl.multiple_of(jnp.maximum(j, 0) * KEY_BLOCK, KEY_BLOCK)

    @pl.when(i == 0)
    def _first():
        _diagonal_scores(q_ref, k_ref, z_ref, i, qb, chains, keeps)

    step(block_start(i - 1), vt_at(i), True)
    step(block_start(i - 2), vt_at(jnp.maximum(i - 1, 0)), False,
         exists=jnp.where(i > 0, 1.0, 0.0).astype(jnp.float32))

    def largest_carry():
        return jnp.max(functools.reduce(jnp.maximum, [car_ref[n] for n in range(len(chains))]))

    def live(state):
        t, carry_max = state
        return jnp.logical_and(t < i, carry_max > EXP_UNDERFLOW)

    def body(state):
        t, _ = state
        j = i - 1 - t
        step(block_start(j - 1), vt_at(j), False)
        return t + 1, largest_carry()

    _, carry_max = lax.while_loop(live, body, (jnp.int32(1), largest_carry()))

    @pl.when(carry_max > EXP_UNDERFLOW)
    def _meta():
        zs = [lax.dot_general(km_ref[:, p * LANES_V7X:(p + 1) * LANES_V7X], qms[n], _NT,
                              preferred_element_type=jnp.float32)
              for n, (p, hh) in enumerate(chains)]
        log_betas, sticks = [], []
        for z in zs:
            log_beta, log_keep = log_sigmoids(z, None)
            sticks.append(jnp.dot(trim_ref[...], _pad_rows(log_keep, LANES_V7X),
                                  preferred_element_type=jnp.float32))
            log_betas.append(log_beta)
        ws = [_pad_rows(jnp.exp(log_betas[n] + sticks[n] + car_ref[n]).astype(jnp.bfloat16),
                        LANES_V7X) for n in range(len(chains))]
        for n, (p, hh) in enumerate(chains):
            acc_ref[n] += jnp.dot(vmt_ref[pl.ds(p * LANES_V7X + hh * HEAD_DIM, HEAD_DIM), :],
                                  ws[n], preferred_element_type=jnp.float32)

    _diagonal_scores(q_ref, k_ref, z_ref, jnp.minimum(i + 1, pl.num_programs(2) - 1), qb, chains,
                     keeps)
    for p in range(tiles):
        o = jnp.concatenate([acc_ref[2 * p], acc_ref[2 * p + 1]], axis=0).T
        o_ref[0, :, p * LANES_V7X:(p + 1) * LANES_V7X] = (
            o * g_ref[0, :, p * LANES_V7X:(p + 1) * LANES_V7X]).astype(o_ref.dtype)


def _attn_specs(b, s, width, qb, tiles):
    tw = tiles * LANES_V7X
    qspec = pl.BlockSpec((1, qb, tw), lambda bi, p, i: (bi, i, p))
    kspec = pl.BlockSpec((1, s, tw), lambda bi, p, i: (bi, 0, p))
    vtspec = pl.BlockSpec((1, s // KEY_BLOCK, tw, KEY_BLOCK), lambda bi, p, i: (bi, 0, p, 0))
    kmspec = pl.BlockSpec((N_META, tw), lambda bi, p, i: (0, p))
    vmtspec = pl.BlockSpec((tw, LANES_V7X), lambda bi, p, i: (p, 0))
    grid = (b, width // tw, s // qb)
    return grid, qspec, kspec, vtspec, kmspec, vmtspec


def _sb_attention(q, k, vt, km, vmt, g, tri, trim, qb, tiles):
    b, s, width = q.shape
    grid, qspec, kspec, vtspec, kmspec, vmtspec = _attn_specs(b, s, width, qb, tiles)
    full = lambda a: pl.BlockSpec(a.shape, lambda bi, p, i: (0,) * a.ndim)
    return pl.pallas_call(
        functools.partial(_sb_kernel, qb=qb, tiles=tiles),
        grid=grid,
        in_specs=[kspec, kspec, vtspec, kmspec, vmtspec, qspec, full(tri), full(trim)],
        out_specs=qspec,
        out_shape=jax.ShapeDtypeStruct((b, s, width), jnp.bfloat16),
        scratch_shapes=[pltpu.VMEM((2 * tiles, HEAD_DIM, qb), jnp.float32),
                        pltpu.VMEM((2 * tiles, 1, qb), jnp.float32),
                        pltpu.VMEM((2 * tiles, KEY_BLOCK, qb), jnp.float32)],
        compiler_params=pltpu.CompilerParams(
            dimension_semantics=("parallel", "parallel", "arbitrary"),
            vmem_limit_bytes=VMEM_LIMIT_BYTES_V7X),
        name="sb_attn",
    )(q, k, vt, km, vmt, g, tri, trim)


def _diff_kernel(q_ref, k_ref, vt_ref, km_ref, vmt_ref, g_ref,
                 lq1_ref, lk1_ref, lq2_ref, lk2_ref, sub_ref, o_ref, acc_ref, m_ref, l_ref, z_ref,
                 *, qb, tiles):
    i = pl.program_id(2)
    lane = lax.broadcasted_iota(jnp.int32, (1, LANES_V7X), 1)
    key = lax.broadcasted_iota(jnp.int32, (KEY_BLOCK, qb), 0)
    qry = lax.broadcasted_iota(jnp.int32, (KEY_BLOCK, qb), 1)
    chunk_ok = (qry // CHUNK) >= (key // CHUNK)
    chains = [(p, c) for p in range(tiles) for c in range(2)]
    keeps = [((lane // (HEAD_DIM // 2)) % 2) == c for p, c in chains]
    qms = [_masked_queries(q_ref, i, qb, p, keeps[n]) for n, (p, c) in enumerate(chains)]
    ones = jnp.ones((SUBLANES_BF16_V7X, KEY_BLOCK), jnp.bfloat16)

    def update(n, s, vt_rows, pad_to, first):
        m_blk = jnp.max(s, axis=0, keepdims=True)
        m_new = m_blk if first else jnp.maximum(m_ref[n], m_blk)
        pr = jnp.exp(s - m_new).astype(jnp.bfloat16)
        if pad_to is not None:
            pr = _pad_rows(pr, pad_to)
        pv = jnp.dot(vt_rows, pr, preferred_element_type=jnp.float32)
        num, den = pv[:LANES_V7X], pv[LANES_V7X:LANES_V7X + 1]
        if first:
            l_ref[n] = den
            acc_ref[n] = num
        else:
            alpha = jnp.exp(m_ref[n] - m_new)
            l_ref[n] = l_ref[n] * alpha + den
            acc_ref[n] = acc_ref[n] * alpha + num
        m_ref[n] = m_new

    def block_start(j):
        return pl.multiple_of(jnp.maximum(jnp.minimum(j, i - 1), 0) * KEY_BLOCK, KEY_BLOCK)

    def step(j, next_start, mask, first):
        def next_scores(n):
            p = chains[n][0]
            kblk = k_ref[0, pl.ds(next_start, KEY_BLOCK), p * LANES_V7X:(p + 1) * LANES_V7X]
            return lax.dot_general(kblk, qms[n], _NT, preferred_element_type=jnp.float32)

        ahead = {n: next_scores(n) for n in range(LOOKAHEAD)}
        for n, (p, c) in enumerate(chains):
            if n + LOOKAHEAD < len(chains):
                ahead[n + LOOKAHEAD] = next_scores(n + LOOKAHEAD)
            s = z_ref[n] if mask is None else jnp.where(mask, z_ref[n], NEG_BIG)
            vt_rows = jnp.concatenate([vt_ref[0, j, p * LANES_V7X:(p + 1) * LANES_V7X, :], ones],
                                      axis=0)
            update(n, s, vt_rows, None, first)
            z_ref[n] = ahead.pop(n)

    @pl.when(i == 0)
    def _first():
        _diagonal_scores(q_ref, k_ref, z_ref, i, qb, chains, keeps)

    ss = [lax.dot_general(km_ref[:, p * LANES_V7X:(p + 1) * LANES_V7X], qms[n], _NT,
                          preferred_element_type=jnp.float32)
          for n, (p, c) in enumerate(chains)]
    step(i, block_start(0), chunk_ok, True)
    for n, (p, c) in enumerate(chains):
        vt_rows = jnp.concatenate([vmt_ref[p * LANES_V7X:(p + 1) * LANES_V7X, :],
                                   ones[:, :LANES_V7X]], axis=0)
        update(n, ss[n], vt_rows, LANES_V7X, False)

    def body(t, c):
        step(t, block_start(t + 1), None, False)
        return c

    lax.fori_loop(0, i, body, 0)

    lam = (jnp.exp(jnp.sum(lq1_ref[...] * lk1_ref[...], axis=-1, keepdims=True))
           - jnp.exp(jnp.sum(lq2_ref[...] * lk2_ref[...], axis=-1, keepdims=True))
           + LAMBDA_INIT)
    _diagonal_scores(q_ref, k_ref, z_ref, jnp.minimum(i + 1, pl.num_programs(2) - 1), qb, chains,
                     keeps)
    for p in range(tiles):
        ot = (acc_ref[2 * p] * (1.0 / l_ref[2 * p])
              - acc_ref[2 * p + 1] * (lam / l_ref[2 * p + 1]))
        ot = ot * lax.rsqrt(jnp.mean(ot * ot, axis=0, keepdims=True) + RMS_EPS)
        o = ot.T * (sub_ref[...] * (1.0 - LAMBDA_INIT))
        o_ref[0, :, p * LANES_V7X:(p + 1) * LANES_V7X] = (
            o * g_ref[0, :, p * LANES_V7X:(p + 1) * LANES_V7X]).astype(o_ref.dtype)


def _diff_attention(q, k, vt, km, vmt, g, lq1, lk1, lq2, lk2, sub, qb, tiles):
    b, s, width = q.shape
    grid, qspec, kspec, vtspec, kmspec, vmtspec = _attn_specs(b, s, width, qb, tiles)
    full = lambda a: pl.BlockSpec(a.shape, lambda bi, h, i: (0,) * a.ndim)
    stat = pltpu.VMEM((2 * tiles, 1, qb), jnp.float32)
    return pl.pallas_call(
        functools.partial(_diff_kernel, qb=qb, tiles=tiles),
        grid=grid,
        in_specs=[kspec, kspec, vtspec, kmspec, vmtspec, qspec,
                  full(lq1), full(lk1), full(lq2), full(lk2), full(sub)],
        out_specs=qspec,
        out_shape=jax.ShapeDtypeStruct((b, s, width), jnp.bfloat16),
        scratch_shapes=[pltpu.VMEM((2 * tiles, LANES_V7X, qb), jnp.float32), stat, stat,
                        pltpu.VMEM((2 * tiles, KEY_BLOCK, qb), jnp.float32)],
        compiler_params=pltpu.CompilerParams(
            dimension_semantics=("parallel", "parallel", "arbitrary"),
            vmem_limit_bytes=VMEM_LIMIT_BYTES_V7X),
        name="diff_attn",
    )(q, k, vt, km, vmt, g, lq1, lk1, lq2, lk2, sub)


def _out_kernel(a_ref, b_ref, w_ref, x_ref, g_ref, o_ref):
    half = a_ref.shape[1]
    h = (x_ref[...]
         + jnp.dot(a_ref[...], w_ref[:half, :], preferred_element_type=jnp.float32)
         + jnp.dot(b_ref[...], w_ref[half:, :], preferred_element_type=jnp.float32))
    o_ref[...] = _rms(h, g_ref[...])


def _out_project(mix_a, mix_b, wo, x2d, gain, tm):
    rows, d = x2d.shape
    return pl.pallas_call(
        _out_kernel,
        grid=(rows // tm,),
        in_specs=[pl.BlockSpec((tm, mix_a.shape[1]), lambda i: (i, 0)),
                  pl.BlockSpec((tm, mix_b.shape[1]), lambda i: (i, 0)),
                  pl.BlockSpec(wo.shape, lambda i: (0, 0)),
                  pl.BlockSpec((tm, d), lambda i: (i, 0)),
                  pl.BlockSpec((1, d), lambda i: (0, 0))],
        out_specs=pl.BlockSpec((tm, d), lambda i: (i, 0)),
        out_shape=jax.ShapeDtypeStruct((rows, d), jnp.float32),
        compiler_params=pltpu.CompilerParams(
            dimension_semantics=("parallel",), vmem_limit_bytes=VMEM_LIMIT_BYTES_V7X),
        name="out_proj",
    )(mix_a, mix_b, wo, x2d, gain)


def _rope_tables(first, length):
    f32 = np.float32
    inv = f32(1.0) / (f32(ROPE_THETA) ** (np.arange(0, HEAD_DIM, 2, dtype=f32) / f32(HEAD_DIM)))
    ang = np.arange(first, first + length, dtype=f32)[:, None] * inv[None, :]
    cos, sin = np.cos(ang).astype(f32), np.sin(ang).astype(f32)
    return (jnp.asarray(np.concatenate([cos, cos, cos, cos], axis=1)),
            jnp.asarray(np.concatenate([-sin, -sin, sin, sin], axis=1)))


def _forward(x, meta_tokens, norm_gain, w_in, w_out, lambda_q1, lambda_k1, lambda_q2, lambda_k2,
             subln_gain, final_norm_gain, *, qb, tm, tiles):
    assert qb == KEY_BLOCK and tm % KEY_BLOCK == 0
    b, s, d = x.shape
    w = w_in[0]
    half = HEAD_DIM // 2
    wqk = w[:, 4 * GROUP:6 * GROUP].reshape(d, 2 * GROUP // LANES_V7X, 2, 2, half)
    wqk = wqk.transpose(0, 1, 3, 2, 4).reshape(d, 2 * GROUP)
    wb = jnp.concatenate([w[:, :4 * GROUP], wqk, w[:, 6 * GROUP:]], axis=1).astype(jnp.bfloat16)
    wo = w_out[0].astype(jnp.bfloat16)
    gain = norm_gain[0][None, :]
    x2d = x.reshape(b * s, d)

    px = _project(x2d, gain, wb, *_rope_tables(N_META, s), tm, False)
    sbk_m, sbvt_m, dk_m, dvt_m = _project(meta_tokens, gain, wb, *_rope_tables(0, N_META),
                                          N_META, True)
    sbq, sbk, sbvt, sbg, dq, dk, dvt, dg = px
    rows3 = lambda a: a.reshape(b, s, GROUP)
    blocks = lambda a: a.reshape(b, s // KEY_BLOCK, GROUP, KEY_BLOCK)

    kb = np.arange(KEY_BLOCK)
    later = (kb[None, :] > kb[:, None]).astype(np.float32)
    tri = np.concatenate([later, np.ones((SUBLANES_BF16_V7X, KEY_BLOCK), np.float32)], axis=0)
    trim = np.zeros((N_META, LANES_V7X), np.float32)
    trim[:, :N_META] = later[:N_META, :N_META]
    tri, trim = jnp.asarray(tri, jnp.bfloat16), jnp.asarray(trim, jnp.bfloat16)

    mix_sb = _sb_attention(rows3(sbq), rows3(sbk), blocks(sbvt), sbk_m, sbvt_m, rows3(sbg),
                           tri, trim, qb, tiles)
    mix_df = _diff_attention(rows3(dq), rows3(dk), blocks(dvt), dk_m, dvt_m, rows3(dg),
                             lambda_q1, lambda_k1, lambda_q2, lambda_k2, subln_gain, qb, tiles)
    tm_out = 2 * tm if (b * s) % (2 * tm) == 0 else tm
    y = _out_project(mix_sb.reshape(b * s, GROUP), mix_df.reshape(b * s, GROUP), wo, x2d,
                     final_norm_gain[None, :], tm_out)
    return y.reshape(b, s, d)


def kernel(x, meta_tokens, norm_gain, w_in, w_out, lambda_q1, lambda_k1, lambda_q2, lambda_k2,
           subln_gain, final_norm_gain):
    return _forward(x, meta_tokens, norm_gain, w_in, w_out, lambda_q1, lambda_k1, lambda_q2,
                    lambda_k2, subln_gain, final_norm_gain, qb=KEY_BLOCK, tm=512, tiles=4)
```

```python
import functools

import numpy as np
import jax
import jax.numpy as jnp
from jax import lax
from jax.experimental import pallas as pl
from jax.experimental.pallas import tpu as pltpu

LANES_V7X = 128
SUBLANES_BF16_V7X = 16
VMEM_LIMIT_BYTES_V7X = 56 * 1024 * 1024

N_META = 16
CHUNK = 64
ROPE_THETA = 10000.0
RMS_EPS = 1e-6
HEAD_DIM = 64
GROUP = 512
KEY_BLOCK = 256
NEG_BIG = -1e30
LOG2_E = 1.4426950408889634
LOOKAHEAD = 1
EXP_UNDERFLOW = -104.0
LAMBDA_INIT = 0.8 - 0.6 * float(np.exp(-0.3 * 0))


def _rms(x, g):
    return x * lax.rsqrt(jnp.mean(x * x, axis=-1, keepdims=True) + RMS_EPS) * g


def _proj_kernel(x_ref, g_ref, w_ref, cos_ref, sin_ref, *out_refs, meta):
    u = _rms(x_ref[...], g_ref[...]).astype(jnp.bfloat16)

    def col(c):
        return jnp.dot(u, w_ref[:, c * GROUP:(c + 1) * GROUP], preferred_element_type=jnp.float32)

    def col_t(c, rows):
        v = col(c)
        if rows > v.shape[0]:
            v = _pad_rows(v, rows)
        return v.astype(jnp.bfloat16).T

    def silu(g):
        return g / (1.0 + jnp.exp(-g))

    def rope(t):
        cos = jnp.concatenate([cos_ref[...]] * (GROUP // LANES_V7X), axis=1)
        sin = jnp.concatenate([sin_ref[...]] * (GROUP // LANES_V7X), axis=1)
        parts = [pltpu.roll(t[:, j * LANES_V7X:(j + 1) * LANES_V7X], LANES_V7X // 2, 1)
                 for j in range(GROUP // LANES_V7X)]
        return t * cos + jnp.concatenate(parts, axis=1) * sin

    scale = HEAD_DIM ** -0.5
    if meta:
        sbk_ref, sbvt_ref, dk_ref, dvt_ref = out_refs
        sbk_ref[...] = col(1).astype(jnp.bfloat16)
        sbvt_ref[...] = col_t(2, LANES_V7X)
        dk_ref[...] = rope(col(5)).astype(jnp.bfloat16)
        dvt_ref[...] = col_t(6, LANES_V7X)
        return
    sbq_ref, sbk_ref, sbvt_ref, sbg_ref, dq_ref, dk_ref, dvt_ref, dg_ref = out_refs
    sbq_ref[...] = (col(0) * scale).astype(jnp.bfloat16)
    sbk_ref[...] = col(1).astype(jnp.bfloat16)
    sbg_ref[...] = silu(col(3))
    dq_ref[...] = (rope(col(4)) * scale).astype(jnp.bfloat16)
    dk_ref[...] = rope(col(5)).astype(jnp.bfloat16)
    dg_ref[...] = silu(col(7))
    for c, ref in ((2, sbvt_ref), (6, dvt_ref)):
        vt = col_t(c, 0)
        for j in range(ref.shape[0]):
            ref[j] = vt[:, j * KEY_BLOCK:(j + 1) * KEY_BLOCK]


def _project(x2d, gain, wb, cos, sin, tm, meta):
    rows, d = x2d.shape
    n_pos_tiles = cos.shape[0] // tm
    row_spec = lambda width: pl.BlockSpec((tm, width), lambda i: (i, 0))
    tab_spec = pl.BlockSpec((tm, LANES_V7X), lambda i: (i % n_pos_tiles, 0))
    bf, f32 = jnp.bfloat16, jnp.float32
    row_out = lambda dt: (row_spec(GROUP), jax.ShapeDtypeStruct((rows, GROUP), dt))
    if meta:
        vt_out = (pl.BlockSpec((GROUP, LANES_V7X), lambda i: (0, 0)),
                  jax.ShapeDtypeStruct((GROUP, LANES_V7X), bf))
        outs = [row_out(bf), vt_out, row_out(bf), vt_out]
    else:
        vt_out = (pl.BlockSpec((tm // KEY_BLOCK, GROUP, KEY_BLOCK), lambda i: (i, 0, 0)),
                  jax.ShapeDtypeStruct((rows // KEY_BLOCK, GROUP, KEY_BLOCK), bf))
        outs = [row_out(bf), row_out(bf), vt_out, row_out(f32),
                row_out(bf), row_out(bf), vt_out, row_out(f32)]
    return pl.pallas_call(
        functools.partial(_proj_kernel, meta=meta),
        grid=(rows // tm,),
        in_specs=[row_spec(d),
                  pl.BlockSpec((1, d), lambda i: (0, 0)),
                  pl.BlockSpec(wb.shape, lambda i: (0, 0)),
                  tab_spec, tab_spec],
        out_specs=[o[0] for o in outs],
        out_shape=[o[1] for o in outs],
        compiler_params=pltpu.CompilerParams(
            dimension_semantics=("parallel",), vmem_limit_bytes=VMEM_LIMIT_BYTES_V7X),
        name="proj_meta" if meta else "proj",
    )(x2d, gain, wb, cos, sin)


_NT = (((1,), (1,)), ((), ()))


def _masked_queries(q_ref, p, keep):
    q = q_ref[0, :, p * LANES_V7X:(p + 1) * LANES_V7X].astype(jnp.float32)
    return jnp.where(keep, q, 0.0).astype(jnp.bfloat16)


def _pad_rows(a, rows):
    return jnp.concatenate([a, jnp.zeros((rows - a.shape[0], a.shape[1]), a.dtype)], axis=0)


class _Phases:
    pass


def _sb_program(q_ref, k_ref, vt_ref, km_ref, vmt_ref, g_ref, tri_ref, trim_ref, o_ref,
                acc_ref, car_ref, z_ref, *, qb, tiles):
    i = pl.program_id(2)
    lane = lax.broadcasted_iota(jnp.int32, (1, LANES_V7X), 1)
    key = lax.broadcasted_iota(jnp.int32, (KEY_BLOCK, qb), 0)
    qry = lax.broadcasted_iota(jnp.int32, (KEY_BLOCK, qb), 1)
    strict = key < qry
    chains = [(p, hh) for p in range(tiles) for hh in range(2)]
    qms = [_masked_queries(q_ref, p, (lane // HEAD_DIM) == hh) for p, hh in chains]

    def log_sigmoids(z, mask):
        log_beta = jnp.minimum(z, 0.0) - jnp.log(1.0 + jnp.exp2(jnp.abs(z) * (-LOG2_E)))
        log_keep = log_beta - z
        if mask is not None:
            log_keep = jnp.where(mask, log_keep, 0.0)
        return log_beta, log_keep.astype(jnp.bfloat16)

    def scores(n, p, start):
        kblk = k_ref[0, pl.ds(start, KEY_BLOCK), p * LANES_V7X:(p + 1) * LANES_V7X]
        z_ref[n] = lax.dot_general(kblk, qms[n], _NT, preferred_element_type=jnp.float32)

    half = KEY_BLOCK // 2
    strict_half = strict[:half, :half]

    def split(a, diagonal):
        return [a[:half, :half], a[:half, half:], a[half:, half:]] if diagonal else [a]

    def join(parts, diagonal):
        if not diagonal:
            return parts[0]
        older = jnp.concatenate(parts[:2], axis=1)
        newer = jnp.concatenate([jnp.zeros_like(parts[2]), parts[2]], axis=1)
        return jnp.concatenate([older, newer], axis=0)

    def stages(next_start, vtblk, diagonal, exists=None):
        masks = [strict_half, None, strict_half] if diagonal else [None]
        log_betas, sums = {}, {}

        def stage1(n, p):
            parts = [log_sigmoids(z, m) for z, m in zip(split(z_ref[n], diagonal), masks)]
            log_betas[n] = [lb for lb, _ in parts]
            log_keep = join([lk for _, lk in parts], diagonal)
            sums[n] = jnp.dot(tri_ref[...], log_keep, preferred_element_type=jnp.float32)
            scores(n, p, next_start)

        def stage2(n, p, hh):
            ws = []
            for lb, st, m in zip(log_betas.pop(n), split(sums[n][:KEY_BLOCK], diagonal), masks):
                w = jnp.exp(lb + st)
                ws.append((w if m is None else jnp.where(m, w, 0.0)).astype(jnp.bfloat16))
            tot = sums.pop(n)[KEY_BLOCK:KEY_BLOCK + 1]
            pv = jnp.dot(vtblk(p, hh), join(ws, diagonal), preferred_element_type=jnp.float32)
            if diagonal:
                acc_ref[n] = pv
                car_ref[n] = tot
            else:
                factor = jnp.exp(car_ref[n])
                if exists is not None:
                    factor, tot = factor * exists, tot * exists
                acc_ref[n] += pv * factor
                car_ref[n] += tot

        return ([functools.partial(stage1, n, p) for n, (p, hh) in enumerate(chains)],
                [functools.partial(stage2, n, p, hh) for n, (p, hh) in enumerate(chains)])

    def vt_at(j):
        return lambda p, hh: vt_ref[0, j, pl.ds(p * LANES_V7X + hh * HEAD_DIM, HEAD_DIM), :]

    def block_start(j):
        return pl.multiple_of(jnp.maximum(j, 0) * KEY_BLOCK, KEY_BLOCK)

    def prologue():
        for n, (p, hh) in enumerate(chains):
            scores(n, p, block_start(i))

    def tail():
        def largest_carry():
            return jnp.max(functools.reduce(jnp.maximum,
                                            [car_ref[n] for n in range(len(chains))]))

        def live(state):
            t, carry_max = state
            return jnp.logical_and(t < i, carry_max > EXP_UNDERFLOW)

        def body(state):
            t, _ = state
            j = i - 1 - t
            first, second = stages(block_start(j - 1), vt_at(j), False)
            for f in first + second:
                f()
            return t + 1, largest_carry()

        _, carry_max = lax.while_loop(live, body, (jnp.int32(1), largest_carry()))

        @pl.when(carry_max > EXP_UNDERFLOW)
        def _meta():
            zs = [lax.dot_general(km_ref[:, p * LANES_V7X:(p + 1) * LANES_V7X], qms[n], _NT,
                                  preferred_element_type=jnp.float32)
                  for n, (p, hh) in enumerate(chains)]
            log_betas, sticks = [], []
            for z in zs:
                log_beta, log_keep = log_sigmoids(z, None)
                sticks.append(jnp.dot(trim_ref[...], _pad_rows(log_keep, LANES_V7X),
                                      preferred_element_type=jnp.float32))
                log_betas.append(log_beta)
            ws = [_pad_rows(jnp.exp(log_betas[n] + sticks[n] + car_ref[n]).astype(jnp.bfloat16),
                            LANES_V7X) for n in range(len(chains))]
            for n, (p, hh) in enumerate(chains):
                acc_ref[n] += jnp.dot(vmt_ref[pl.ds(p * LANES_V7X + hh * HEAD_DIM, HEAD_DIM), :],
                                      ws[n], preferred_element_type=jnp.float32)

        for p in range(tiles):
            o = jnp.concatenate([acc_ref[2 * p], acc_ref[2 * p + 1]], axis=0).T
            o_ref[0, :, p * LANES_V7X:(p + 1) * LANES_V7X] = (
                o * g_ref[0, :, p * LANES_V7X:(p + 1) * LANES_V7X]).astype(o_ref.dtype)

    ph = _Phases()
    ph.prologue = prologue
    ph.diagonal = stages(block_start(i - 1), vt_at(i), True)
    ph.older = stages(block_start(i - 2), vt_at(jnp.maximum(i - 1, 0)), False,
                      exists=jnp.where(i > 0, 1.0, 0.0).astype(jnp.float32))
    ph.tail = tail
    return ph


def _diff_program(q_ref, k_ref, vt_ref, km_ref, vmt_ref, g_ref,
                  lq1_ref, lk1_ref, lq2_ref, lk2_ref, sub_ref, o_ref, acc_ref, m_ref, l_ref, z_ref,
                  *, qb, tiles):
    i = pl.program_id(2)
    lane = lax.broadcasted_iota(jnp.int32, (1, LANES_V7X), 1)
    key = lax.broadcasted_iota(jnp.int32, (KEY_BLOCK, qb), 0)
    qry = lax.broadcasted_iota(jnp.int32, (KEY_BLOCK, qb), 1)
    chunk_ok = (qry // CHUNK) >= (key // CHUNK)
    chains = [(p, c) for p in range(tiles) for c in range(2)]
    qms = [_masked_queries(q_ref, p, ((lane // (HEAD_DIM // 2)) % 2) == c) for p, c in chains]
    ones = jnp.ones((SUBLANES_BF16_V7X, KEY_BLOCK), jnp.bfloat16)

    def update(n, s, vt_rows, pad_to, first):
        m_blk = jnp.max(s, axis=0, keepdims=True)
        m_new = m_blk if first else jnp.maximum(m_ref[n], m_blk)
        pr = jnp.exp(s - m_new).astype(jnp.bfloat16)
        if pad_to is not None:
            pr = _pad_rows(pr, pad_to)
        pv = jnp.dot(vt_rows, pr, preferred_element_type=jnp.float32)
        num, den = pv[:LANES_V7X], pv[LANES_V7X:LANES_V7X + 1]
        if first:
            l_ref[n] = den
            acc_ref[n] = num
        else:
            alpha = jnp.exp(m_ref[n] - m_new)
            l_ref[n] = l_ref[n] * alpha + den
            acc_ref[n] = acc_ref[n] * alpha + num
        m_ref[n] = m_new

    def scores(n, p, start):
        kblk = k_ref[0, pl.ds(start, KEY_BLOCK), p * LANES_V7X:(p + 1) * LANES_V7X]
        z_ref[n] = lax.dot_general(kblk, qms[n], _NT, preferred_element_type=jnp.float32)

    def block_start(j):
        return pl.multiple_of(jnp.maximum(jnp.minimum(j, i - 1), 0) * KEY_BLOCK, KEY_BLOCK)

    def block(j, next_start, mask, first):
        ahead = {}

        def next_scores(n):
            p = chains[n][0]
            kblk = k_ref[0, pl.ds(next_start, KEY_BLOCK), p * LANES_V7X:(p + 1) * LANES_V7X]
            ahead[n] = lax.dot_general(kblk, qms[n], _NT, preferred_element_type=jnp.float32)

        def chain(n, p):
            for m in (range(LOOKAHEAD) if n == 0 else ()):
                next_scores(m)
            if n + LOOKAHEAD < len(chains):
                next_scores(n + LOOKAHEAD)
            s = z_ref[n] if mask is None else jnp.where(mask, z_ref[n], NEG_BIG)
            vt_rows = jnp.concatenate([vt_ref[0, j, p * LANES_V7X:(p + 1) * LANES_V7X, :], ones],
                                      axis=0)
            update(n, s, vt_rows, None, first)
            z_ref[n] = ahead.pop(n)

        return [functools.partial(chain, n, p) for n, (p, c) in enumerate(chains)]

    meta_scores = {}

    def prologue():
        for n, (p, c) in enumerate(chains):
            scores(n, p, pl.multiple_of(i * qb, qb))
        for n, (p, c) in enumerate(chains):
            meta_scores[n] = lax.dot_general(km_ref[:, p * LANES_V7X:(p + 1) * LANES_V7X], qms[n],
                                             _NT, preferred_element_type=jnp.float32)

    def meta(n, p):
        vt_rows = jnp.concatenate([vmt_ref[p * LANES_V7X:(p + 1) * LANES_V7X, :],
                                   ones[:, :LANES_V7X]], axis=0)
        update(n, meta_scores.pop(n), vt_rows, LANES_V7X, False)

    def tail():
        def body(t, c):
            for f in block(t, block_start(t + 1), None, False):
                f()
            return c

        lax.fori_loop(0, i, body, 0)

        lam = (jnp.exp(jnp.sum(lq1_ref[...] * lk1_ref[...], axis=-1, keepdims=True))
               - jnp.exp(jnp.sum(lq2_ref[...] * lk2_ref[...], axis=-1, keepdims=True))
               + LAMBDA_INIT)
        for p in range(tiles):
            ot = (acc_ref[2 * p] * (1.0 / l_ref[2 * p])
                  - acc_ref[2 * p + 1] * (lam / l_ref[2 * p + 1]))
            ot = ot * lax.rsqrt(jnp.mean(ot * ot, axis=0, keepdims=True) + RMS_EPS)
            o = ot.T * (sub_ref[...] * (1.0 - LAMBDA_INIT))
            o_ref[0, :, p * LANES_V7X:(p + 1) * LANES_V7X] = (
                o * g_ref[0, :, p * LANES_V7X:(p + 1) * LANES_V7X]).astype(o_ref.dtype)

    ph = _Phases()
    ph.prologue = prologue
    ph.diagonal = block(i, block_start(0), chunk_ok, True)
    ph.meta = [functools.partial(meta, n, p) for n, (p, c) in enumerate(chains)]
    ph.tail = tail
    return ph


N_SB_IN, N_DF_IN = 8, 11


def _attn_kernel(*refs, qb, tiles):
    sb_in, df_in = refs[:N_SB_IN], refs[N_SB_IN:N_SB_IN + N_DF_IN]
    o_sb, o_df = refs[N_SB_IN + N_DF_IN:N_SB_IN + N_DF_IN + 2]
    scratch = refs[N_SB_IN + N_DF_IN + 2:]
    sb = _sb_program(*sb_in, o_sb, *scratch[:3], qb=qb, tiles=tiles)
    df = _diff_program(*df_in, o_df, *scratch[3:], qb=qb, tiles=tiles)
    sb.prologue()
    df.prologue()
    for first, other in zip(sb.diagonal[0], df.diagonal):
        first()
        other()
    for second, other in zip(sb.diagonal[1], df.meta):
        second()
        other()
    for f in sb.older[0] + sb.older[1]:
        f()
    df.tail()
    sb.tail()


def _attention(sb_args, df_args, qb, tiles):
    q = sb_args[0]
    b, s, width = q.shape
    tw = tiles * LANES_V7X
    assert tw == width, "the fused kernel handles every head of a batch element per grid step"
    qspec = pl.BlockSpec((1, qb, tw), lambda bi, p, i: (bi, i, p))
    kspec = pl.BlockSpec((1, s, tw), lambda bi, p, i: (bi, 0, p))
    vtspec = pl.BlockSpec((1, s // KEY_BLOCK, tw, KEY_BLOCK), lambda bi, p, i: (bi, 0, p, 0))
    kmspec = pl.BlockSpec((N_META, tw), lambda bi, p, i: (0, p))
    vmtspec = pl.BlockSpec((tw, LANES_V7X), lambda bi, p, i: (p, 0))
    full = lambda a: pl.BlockSpec(a.shape, lambda bi, p, i: (0,) * a.ndim)
    common = [qspec, kspec, vtspec, kmspec, vmtspec, qspec]
    in_specs = (common + [full(a) for a in sb_args[6:]] + common + [full(a) for a in df_args[6:]])
    assert len(sb_args) == N_SB_IN and len(df_args) == N_DF_IN
    stat = pltpu.VMEM((2 * tiles, 1, qb), jnp.float32)
    score = pltpu.VMEM((2 * tiles, KEY_BLOCK, qb), jnp.float32)
    out = jax.ShapeDtypeStruct((b, s, width), jnp.bfloat16)
    return pl.pallas_call(
        functools.partial(_attn_kernel, qb=qb, tiles=tiles),
        grid=(b, width // tw, s // qb),
        in_specs=in_specs,
        out_specs=[qspec, qspec],
        out_shape=[out, out],
        scratch_shapes=[pltpu.VMEM((2 * tiles, HEAD_DIM, qb), jnp.float32), stat, score,
                        pltpu.VMEM((2 * tiles, LANES_V7X, qb), jnp.float32), stat, stat, score],
        compiler_params=pltpu.CompilerParams(
            dimension_semantics=("parallel", "parallel", "arbitrary"),
            vmem_limit_bytes=VMEM_LIMIT_BYTES_V7X),
        name="attn",
    )(*sb_args, *df_args)


def _out_kernel(a_ref, b_ref, w_ref, x_ref, g_ref, o_ref):
    half = a_ref.shape[1]
    h = (x_ref[...]
         + jnp.dot(a_ref[...], w_ref[:half, :], preferred_element_type=jnp.float32)
         + jnp.dot(b_ref[...], w_ref[half:, :], preferred_element_type=jnp.float32))
    o_ref[...] = _rms(h, g_ref[...])


def _out_project(mix_a, mix_b, wo, x2d, gain, tm):
    rows, d = x2d.shape
    return pl.pallas_call(
        _out_kernel,
        grid=(rows // tm,),
        in_specs=[pl.BlockSpec((tm, mix_a.shape[1]), lambda i: (i, 0)),
                  pl.BlockSpec((tm, mix_b.shape[1]), lambda i: (i, 0)),
                  pl.BlockSpec(wo.shape, lambda i: (0, 0)),
                  pl.BlockSpec((tm, d), lambda i: (i, 0)),
                  pl.BlockSpec((1, d), lambda i: (0, 0))],
        out_specs=pl.BlockSpec((tm, d), lambda i: (i, 0)),
        out_shape=jax.ShapeDtypeStruct((rows, d), jnp.float32),
        compiler_params=pltpu.CompilerParams(
            dimension_semantics=("parallel",), vmem_limit_bytes=VMEM_LIMIT_BYTES_V7X),
        name="out_proj",
    )(mix_a, mix_b, wo, x2d, gain)


def _rope_tables(first, length):
    f32 = np.float32
    inv = f32(1.0) / (f32(ROPE_THETA) ** (np.arange(0, HEAD_DIM, 2, dtype=f32) / f32(HEAD_DIM)))
    ang = np.arange(first, first + length, dtype=f32)[:, None] * inv[None, :]
    cos, sin = np.cos(ang).astype(f32), np.sin(ang).astype(f32)
    return (jnp.asarray(np.concatenate([cos, cos, cos, cos], axis=1)),
            jnp.asarray(np.concatenate([-sin, -sin, sin, sin], axis=1)))


def _forward(x, meta_tokens, norm_gain, w_in, w_out, lambda_q1, lambda_k1, lambda_q2, lambda_k2,
             subln_gain, final_norm_gain, *, qb, tm, tiles):
    assert qb == KEY_BLOCK and tm % KEY_BLOCK == 0
    b, s, d = x.shape
    w = w_in[0]
    half = HEAD_DIM // 2
    wqk = w[:, 4 * GROUP:6 * GROUP].reshape(d, 2 * GROUP // LANES_V7X, 2, 2, half)
    wqk = wqk.transpose(0, 1, 3, 2, 4).reshape(d, 2 * GROUP)
    wb = jnp.concatenate([w[:, :4 * GROUP], wqk, w[:, 6 * GROUP:]], axis=1).astype(jnp.bfloat16)
    wo = w_out[0].astype(jnp.bfloat16)
    gain = norm_gain[0][None, :]
    x2d = x.reshape(b * s, d)

    px = _project(x2d, gain, wb, *_rope_tables(N_META, s), tm, False)
    sbk_m, sbvt_m, dk_m, dvt_m = _project(meta_tokens, gain, wb, *_rope_tables(0, N_META),
                                          N_META, True)
    sbq, sbk, sbvt, sbg, dq, dk, dvt, dg = px
    rows3 = lambda a: a.reshape(b, s, GROUP)
    blocks = lambda a: a.reshape(b, s // KEY_BLOCK, GROUP, KEY_BLOCK)

    kb = np.arange(KEY_BLOCK)
    later = (kb[None, :] > kb[:, None]).astype(np.float32)
    tri = np.concatenate([later, np.ones((SUBLANES_BF16_V7X, KEY_BLOCK), np.float32)], axis=0)
    trim = np.zeros((N_META, LANES_V7X), np.float32)
    trim[:, :N_META] = later[:N_META, :N_META]
    tri, trim = jnp.asarray(tri, jnp.bfloat16), jnp.asarray(trim, jnp.bfloat16)

    mix_sb, mix_df = _attention(
        (rows3(sbq), rows3(sbk), blocks(sbvt), sbk_m, sbvt_m, rows3(sbg), tri, trim),
        (rows3(dq), rows3(dk), blocks(dvt), dk_m, dvt_m, rows3(dg),
         lambda_q1, lambda_k1, lambda_q2, lambda_k2, subln_gain), qb, tiles)
    tm_out = 2 * tm if (b * s) % (2 * tm) == 0 else tm
    y = _out_project(mix_sb.reshape(b * s, GROUP), mix_df.reshape(b * s, GROUP), wo, x2d,
                     final_norm_gain[None, :], tm_out)
    return y.reshape(b, s, d)


def kernel(x, meta_tokens, norm_gain, w_in, w_out, lambda_q1, lambda_k1, lambda_q2, lambda_k2,
           subln_gain, final_norm_gain):
    return _forward(x, meta_tokens, norm_gain, w_in, w_out, lambda_q1, lambda_k1, lambda_q2,
                    lambda_k2, subln_gain, final_norm_gain, qb=KEY_BLOCK, tm=512, tiles=4)
```

```python
import functools

import numpy as np
import jax
import jax.numpy as jnp
from jax import lax
from jax.experimental import pallas as pl
from jax.experimental.pallas import tpu as pltpu

LANES_V7X = 128
SUBLANES_BF16_V7X = 16
VMEM_LIMIT_BYTES_V7X = 56 * 1024 * 1024

N_META = 16
CHUNK = 64
ROPE_THETA = 10000.0
RMS_EPS = 1e-6
HEAD_DIM = 64
GROUP = 512
KEY_BLOCK = 256
NEG_BIG = -1e30
LOG2_E = 1.4426950408889634
LOOKAHEAD = 1
EXP_UNDERFLOW = -104.0
LAMBDA_INIT = 0.8 - 0.6 * float(np.exp(-0.3 * 0))


def _rms(x, g):
    return x * lax.rsqrt(jnp.mean(x * x, axis=-1, keepdims=True) + RMS_EPS) * g


def _proj_kernel(x_ref, g_ref, w_ref, cos_ref, sin_ref, *out_refs, meta):
    u = _rms(x_ref[...], g_ref[...]).astype(jnp.bfloat16)

    def col(c):
        return jnp.dot(u, w_ref[:, c * GROUP:(c + 1) * GROUP], preferred_element_type=jnp.float32)

    def col_t(c, rows):
        v = col(c)
        if rows > v.shape[0]:
            v = _pad_rows(v, rows)
        return v.astype(jnp.bfloat16).T

    def silu(g):
        return g / (1.0 + jnp.exp(-g))

    def rope(t):
        cos = jnp.concatenate([cos_ref[...]] * (GROUP // LANES_V7X), axis=1)
        sin = jnp.concatenate([sin_ref[...]] * (GROUP // LANES_V7X), axis=1)
        parts = [pltpu.roll(t[:, j * LANES_V7X:(j + 1) * LANES_V7X], LANES_V7X // 2, 1)
                 for j in range(GROUP // LANES_V7X)]
        return t * cos + jnp.concatenate(parts, axis=1) * sin

    scale = HEAD_DIM ** -0.5
    if meta:
        sbk_ref, sbvt_ref, dk_ref, dvt_ref = out_refs
        sbk_ref[...] = col(1).astype(jnp.bfloat16)
        sbvt_ref[...] = col_t(2, LANES_V7X)
        dk_ref[...] = rope(col(5)).astype(jnp.bfloat16)
        dvt_ref[...] = col_t(6, LANES_V7X)
        return
    sbq_ref, sbk_ref, sbvt_ref, sbg_ref, dq_ref, dk_ref, dvt_ref, dg_ref = out_refs
    sbq_ref[...] = (col(0) * scale).astype(jnp.bfloat16)
    sbk_ref[...] = col(1).astype(jnp.bfloat16)
    sbg_ref[...] = silu(col(3))
    dq_ref[...] = (rope(col(4)) * scale).astype(jnp.bfloat16)
    dk_ref[...] = rope(col(5)).astype(jnp.bfloat16)
    dg_ref[...] = silu(col(7))
    for c, ref in ((2, sbvt_ref), (6, dvt_ref)):
        vt = col_t(c, 0)
        for j in range(ref.shape[0]):
            ref[j] = vt[:, j * KEY_BLOCK:(j + 1) * KEY_BLOCK]


def _project(x2d, gain, wb, cos, sin, tm, meta):
    rows, d = x2d.shape
    n_pos_tiles = cos.shape[0] // tm
    row_spec = lambda width: pl.BlockSpec((tm, width), lambda i: (i, 0))
    tab_spec = pl.BlockSpec((tm, LANES_V7X), lambda i: (i % n_pos_tiles, 0))
    bf, f32 = jnp.bfloat16, jnp.float32
    row_out = lambda dt: (row_spec(GROUP), jax.ShapeDtypeStruct((rows, GROUP), dt))
    if meta:
        vt_out = (pl.BlockSpec((GROUP, LANES_V7X), lambda i: (0, 0)),
                  jax.ShapeDtypeStruct((GROUP, LANES_V7X), bf))
        outs = [row_out(bf), vt_out, row_out(bf), vt_out]
    else:
        vt_out = (pl.BlockSpec((tm // KEY_BLOCK, GROUP, KEY_BLOCK), lambda i: (i, 0, 0)),
                  jax.ShapeDtypeStruct((rows // KEY_BLOCK, GROUP, KEY_BLOCK), bf))
        outs = [row_out(bf), row_out(bf), vt_out, row_out(f32),
                row_out(bf), row_out(bf), vt_out, row_out(f32)]
    return pl.pallas_call(
        functools.partial(_proj_kernel, meta=meta),
        grid=(rows // tm,),
        in_specs=[row_spec(d),
                  pl.BlockSpec((1, d), lambda i: (0, 0)),
                  pl.BlockSpec(wb.shape, lambda i: (0, 0)),
                  tab_spec, tab_spec],
        out_specs=[o[0] for o in outs],
        out_shape=[o[1] for o in outs],
        compiler_params=pltpu.CompilerParams(
            dimension_semantics=("parallel",), vmem_limit_bytes=VMEM_LIMIT_BYTES_V7X),
        name="proj_meta" if meta else "proj",
    )(x2d, gain, wb, cos, sin)


_NT = (((1,), (1,)), ((), ()))


def _masked_queries(q_ref, p, keep):
    q = q_ref[0, :, p * LANES_V7X:(p + 1) * LANES_V7X].astype(jnp.float32)
    return jnp.where(keep, q, 0.0).astype(jnp.bfloat16)


def _pad_rows(a, rows):
    return jnp.concatenate([a, jnp.zeros((rows - a.shape[0], a.shape[1]), a.dtype)], axis=0)


class _Phases:
    pass


def _sb_program(q_ref, k_ref, vt_ref, km_ref, vmt_ref, g_ref, tri_ref, trim_ref, o_ref,
                acc_ref, car_ref, z_ref, *, qb, tiles):
    i = pl.program_id(2)
    lane = lax.broadcasted_iota(jnp.int32, (1, LANES_V7X), 1)
    key = lax.broadcasted_iota(jnp.int32, (KEY_BLOCK, qb), 0)
    qry = lax.broadcasted_iota(jnp.int32, (KEY_BLOCK, qb), 1)
    strict = key < qry
    chains = [(p, hh) for p in range(tiles) for hh in range(2)]
    qms = [_masked_queries(q_ref, p, (lane // HEAD_DIM) == hh) for p, hh in chains]

    def log_sigmoids(z, mask):
        log_beta = jnp.minimum(z, 0.0) - jnp.log(1.0 + jnp.exp2(jnp.abs(z) * (-LOG2_E)))
        log_keep = log_beta - z
        if mask is not None:
            log_keep = jnp.where(mask, log_keep, 0.0)
        return log_beta, log_keep.astype(jnp.bfloat16)

    def scores(n, p, start):
        kblk = k_ref[0, pl.ds(start, KEY_BLOCK), p * LANES_V7X:(p + 1) * LANES_V7X]
        z_ref[n] = lax.dot_general(kblk, qms[n], _NT, preferred_element_type=jnp.float32)

    half = KEY_BLOCK // 2
    strict_half = strict[:half, :half]

    def split(a, diagonal):
        return [a[:half, :half], a[:half, half:], a[half:, half:]] if diagonal else [a]

    def join(parts, diagonal):
        if not diagonal:
            return parts[0]
        older = jnp.concatenate(parts[:2], axis=1)
        newer = jnp.concatenate([jnp.zeros_like(parts[2]), parts[2]], axis=1)
        return jnp.concatenate([older, newer], axis=0)

    def stages(next_start, vtblk, diagonal, exists=None):
        masks = [strict_half, None, strict_half] if diagonal else [None]
        log_betas, sums = {}, {}

        def stage1(n, p):
            parts = [log_sigmoids(z, m) for z, m in zip(split(z_ref[n], diagonal), masks)]
            log_betas[n] = [lb for lb, _ in parts]
            log_keep = join([lk for _, lk in parts], diagonal)
            sums[n] = jnp.dot(tri_ref[...], log_keep, preferred_element_type=jnp.float32)
            scores(n, p, next_start)

        def stage2(n, p, hh):
            ws = []
            for lb, st, m in zip(log_betas.pop(n), split(sums[n][:KEY_BLOCK], diagonal), masks):
                w = jnp.exp(lb + st)
                ws.append((w if m is None else jnp.where(m, w, 0.0)).astype(jnp.bfloat16))
            tot = sums.pop(n)[KEY_BLOCK:KEY_BLOCK + 1]
            pv = jnp.dot(vtblk(p, hh), join(ws, diagonal), preferred_element_type=jnp.float32)
            if diagonal:
                acc_ref[n] = pv
                car_ref[n] = tot
            else:
                factor = jnp.exp(car_ref[n])
                if exists is not None:
                    factor, tot = factor * exists, tot * exists
                acc_ref[n] += pv * factor
                car_ref[n] += tot

        return ([functools.partial(stage1, n, p) for n, (p, hh) in enumerate(chains)],
                [functools.partial(stage2, n, p, hh) for n, (p, hh) in enumerate(chains)])

    def vt_at(j):
        return lambda p, hh: vt_ref[0, j, pl.ds(p * LANES_V7X + hh * HEAD_DIM, HEAD_DIM), :]

    def block_start(j):
        return pl.multiple_of(jnp.maximum(j, 0) * KEY_BLOCK, KEY_BLOCK)

    def prologue():
        for n, (p, hh) in enumerate(chains):
            scores(n, p, block_start(i))

    def older_blocks():
        def largest_carry():
            return jnp.max(functools.reduce(jnp.maximum,
                                            [car_ref[n] for n in range(len(chains))]))

        def live(state):
            t, carry_max = state
            return jnp.logical_and(t < i, carry_max > EXP_UNDERFLOW)

        def body(state):
            t, _ = state
            j = i - 1 - t
            first, second = stages(block_start(j - 1), vt_at(j), False)
            for f in first + second:
                f()
            return t + 1, largest_carry()

        _, carry_max = lax.while_loop(live, body, (jnp.int32(1), largest_carry()))

        @pl.when(carry_max > EXP_UNDERFLOW)
        def _meta():
            zs = [lax.dot_general(km_ref[:, p * LANES_V7X:(p + 1) * LANES_V7X], qms[n], _NT,
                                  preferred_element_type=jnp.float32)
                  for n, (p, hh) in enumerate(chains)]
            log_betas, sticks = [], []
            for z in zs:
                log_beta, log_keep = log_sigmoids(z, None)
                sticks.append(jnp.dot(trim_ref[...], _pad_rows(log_keep, LANES_V7X),
                                      preferred_element_type=jnp.float32))
                log_betas.append(log_beta)
            ws = [_pad_rows(jnp.exp(log_betas[n] + sticks[n] + car_ref[n]).astype(jnp.bfloat16),
                            LANES_V7X) for n in range(len(chains))]
            for n, (p, hh) in enumerate(chains):
                acc_ref[n] += jnp.dot(vmt_ref[pl.ds(p * LANES_V7X + hh * HEAD_DIM, HEAD_DIM), :],
                                      ws[n], preferred_element_type=jnp.float32)

    def finalize():
        for p in range(tiles):
            o = jnp.concatenate([acc_ref[2 * p], acc_ref[2 * p + 1]], axis=0).T
            o_ref[0, :, p * LANES_V7X:(p + 1) * LANES_V7X] = (
                o * g_ref[0, :, p * LANES_V7X:(p + 1) * LANES_V7X]).astype(o_ref.dtype)

    ph = _Phases()
    ph.prologue = prologue
    ph.diagonal = stages(block_start(i - 1), vt_at(i), True)
    ph.older = stages(block_start(i - 2), vt_at(jnp.maximum(i - 1, 0)), False,
                      exists=jnp.where(i > 0, 1.0, 0.0).astype(jnp.float32))
    ph.older_blocks = older_blocks
    ph.finalize = finalize
    return ph


def _diff_program(q_ref, k_ref, vt_ref, km_ref, vmt_ref, g_ref,
                  lq1_ref, lk1_ref, lq2_ref, lk2_ref, sub_ref, o_ref, acc_ref, m_ref, l_ref, z_ref,
                  *, qb, tiles):
    i = pl.program_id(2)
    lane = lax.broadcasted_iota(jnp.int32, (1, LANES_V7X), 1)
    key = lax.broadcasted_iota(jnp.int32, (KEY_BLOCK, qb), 0)
    qry = lax.broadcasted_iota(jnp.int32, (KEY_BLOCK, qb), 1)
    chunk_ok = (qry // CHUNK) >= (key // CHUNK)
    chains = [(p, c) for p in range(tiles) for c in range(2)]
    qms = [_masked_queries(q_ref, p, ((lane // (HEAD_DIM // 2)) % 2) == c) for p, c in chains]
    ones = jnp.ones((SUBLANES_BF16_V7X, KEY_BLOCK), jnp.bfloat16)

    def update(n, s, vt_rows, pad_to, first):
        m_blk = jnp.max(s, axis=0, keepdims=True)
        m_new = m_blk if first else jnp.maximum(m_ref[n], m_blk)
        pr = jnp.exp(s - m_new).astype(jnp.bfloat16)
        if pad_to is not None:
            pr = _pad_rows(pr, pad_to)
        pv = jnp.dot(vt_rows, pr, preferred_element_type=jnp.float32)
        num, den = pv[:LANES_V7X], pv[LANES_V7X:LANES_V7X + 1]
        if first:
            l_ref[n] = den
            acc_ref[n] = num
        else:
            alpha = jnp.exp(m_ref[n] - m_new)
            l_ref[n] = l_ref[n] * alpha + den
            acc_ref[n] = acc_ref[n] * alpha + num
        m_ref[n] = m_new

    def scores(n, p, start):
        kblk = k_ref[0, pl.ds(start, KEY_BLOCK), p * LANES_V7X:(p + 1) * LANES_V7X]
        z_ref[n] = lax.dot_general(kblk, qms[n], _NT, preferred_element_type=jnp.float32)

    def block_start(j):
        return pl.multiple_of(jnp.maximum(jnp.minimum(j, i - 1), 0) * KEY_BLOCK, KEY_BLOCK)

    def block(j, next_start, mask, first):
        ahead = {}

        def next_scores(n):
            p = chains[n][0]
            kblk = k_ref[0, pl.ds(next_start, KEY_BLOCK), p * LANES_V7X:(p + 1) * LANES_V7X]
            ahead[n] = lax.dot_general(kblk, qms[n], _NT, preferred_element_type=jnp.float32)

        def chain(n, p):
            for m in (range(LOOKAHEAD) if n == 0 else ()):
                next_scores(m)
            if n + LOOKAHEAD < len(chains):
                next_scores(n + LOOKAHEAD)
            s = z_ref[n] if mask is None else jnp.where(mask, z_ref[n], NEG_BIG)
            vt_rows = jnp.concatenate([vt_ref[0, j, p * LANES_V7X:(p + 1) * LANES_V7X, :], ones],
                                      axis=0)
            update(n, s, vt_rows, None, first)
            z_ref[n] = ahead.pop(n)

        return [functools.partial(chain, n, p) for n, (p, c) in enumerate(chains)]

    meta_scores = {}

    def prologue():
        for n, (p, c) in enumerate(chains):
            scores(n, p, pl.multiple_of(i * qb, qb))
        for n, (p, c) in enumerate(chains):
            meta_scores[n] = lax.dot_general(km_ref[:, p * LANES_V7X:(p + 1) * LANES_V7X], qms[n],
                                             _NT, preferred_element_type=jnp.float32)

    def meta(n, p):
        vt_rows = jnp.concatenate([vmt_ref[p * LANES_V7X:(p + 1) * LANES_V7X, :],
                                   ones[:, :LANES_V7X]], axis=0)
        update(n, meta_scores.pop(n), vt_rows, LANES_V7X, False)

    def older_blocks():
        def body(t, c):
            for f in block(t, block_start(t + 1), None, False):
                f()
            return c

        lax.fori_loop(0, i, body, 0)

    def finalize():
        lam = (jnp.exp(jnp.sum(lq1_ref[...] * lk1_ref[...], axis=-1, keepdims=True))
               - jnp.exp(jnp.sum(lq2_ref[...] * lk2_ref[...], axis=-1, keepdims=True))
               + LAMBDA_INIT)
        for p in range(tiles):
            ot = (acc_ref[2 * p] * (1.0 / l_ref[2 * p])
                  - acc_ref[2 * p + 1] * (lam / l_ref[2 * p + 1]))
            ot = ot * lax.rsqrt(jnp.mean(ot * ot, axis=0, keepdims=True) + RMS_EPS)
            o = ot.T * (sub_ref[...] * (1.0 - LAMBDA_INIT))
            o_ref[0, :, p * LANES_V7X:(p + 1) * LANES_V7X] = (
                o * g_ref[0, :, p * LANES_V7X:(p + 1) * LANES_V7X]).astype(o_ref.dtype)

    ph = _Phases()
    ph.prologue = prologue
    ph.diagonal = block(i, block_start(0), chunk_ok, True)
    ph.meta = [functools.partial(meta, n, p) for n, (p, c) in enumerate(chains)]
    ph.older_blocks = older_blocks
    ph.finalize = finalize
    return ph


N_SB_IN, N_DF_IN = 8, 11


def _attn_kernel(*refs, qb, tiles):
    sb_in, df_in = refs[:N_SB_IN], refs[N_SB_IN:N_SB_IN + N_DF_IN]
    wo_ref, x_ref, fg_ref, y_ref = refs[N_SB_IN + N_DF_IN:N_SB_IN + N_DF_IN + 4]
    scratch = refs[N_SB_IN + N_DF_IN + 4:]
    mix_sb, mix_df = scratch[7:]
    sb = _sb_program(*sb_in, mix_sb, *scratch[:3], qb=qb, tiles=tiles)
    df = _diff_program(*df_in, mix_df, *scratch[3:7], qb=qb, tiles=tiles)
    sb.prologue()
    df.prologue()
    for first, other in zip(sb.diagonal[0], df.diagonal):
        first()
        other()
    for second, other in zip(sb.diagonal[1], df.meta):
        second()
        other()
    for f in sb.older[0] + sb.older[1]:
        f()
    df.older_blocks()
    sb.older_blocks()
    df.finalize()
    sb.finalize()
    half = mix_sb.shape[-1]
    h = (x_ref[0]
         + jnp.dot(mix_sb[0], wo_ref[:half, :], preferred_element_type=jnp.float32)
         + jnp.dot(mix_df[0], wo_ref[half:, :], preferred_element_type=jnp.float32))
    y_ref[0] = _rms(h, fg_ref[...])


def _attention(sb_args, df_args, wo, x, final_gain, qb, tiles):
    q = sb_args[0]
    b, s, width = q.shape
    tw = tiles * LANES_V7X
    assert tw == width, "the fused kernel handles every head of a batch element per grid step"
    qspec = pl.BlockSpec((1, qb, tw), lambda bi, p, i: (bi, i, p))
    kspec = pl.BlockSpec((1, s, tw), lambda bi, p, i: (bi, 0, p))
    vtspec = pl.BlockSpec((1, s // KEY_BLOCK, tw, KEY_BLOCK), lambda bi, p, i: (bi, 0, p, 0))
    kmspec = pl.BlockSpec((N_META, tw), lambda bi, p, i: (0, p))
    vmtspec = pl.BlockSpec((tw, LANES_V7X), lambda bi, p, i: (p, 0))
    full = lambda a: pl.BlockSpec(a.shape, lambda bi, p, i: (0,) * a.ndim)
    common = [qspec, kspec, vtspec, kmspec, vmtspec, qspec]
    xspec = pl.BlockSpec((1, qb, x.shape[-1]), lambda bi, p, i: (bi, i, 0))
    in_specs = (common + [full(a) for a in sb_args[6:]] + common + [full(a) for a in df_args[6:]]
                + [full(wo), xspec, full(final_gain)])
    assert len(sb_args) == N_SB_IN and len(df_args) == N_DF_IN
    stat = pltpu.VMEM((2 * tiles, 1, qb), jnp.float32)
    score = pltpu.VMEM((2 * tiles, KEY_BLOCK, qb), jnp.float32)
    mix = pltpu.VMEM((1, qb, width), jnp.bfloat16)
    return pl.pallas_call(
        functools.partial(_attn_kernel, qb=qb, tiles=tiles),
        grid=(b, width // tw, s // qb),
        in_specs=in_specs,
        out_specs=xspec,
        out_shape=jax.ShapeDtypeStruct(x.shape, jnp.float32),
        scratch_shapes=[pltpu.VMEM((2 * tiles, HEAD_DIM, qb), jnp.float32), stat, score,
                        pltpu.VMEM((2 * tiles, LANES_V7X, qb), jnp.float32), stat, stat, score,
                        mix, mix],
        compiler_params=pltpu.CompilerParams(
            dimension_semantics=("parallel", "parallel", "arbitrary"),
            vmem_limit_bytes=VMEM_LIMIT_BYTES_V7X),
        name="attn",
    )(*sb_args, *df_args, wo, x, final_gain)


def _rope_tables(first, length):
    f32 = np.float32
    inv = f32(1.0) / (f32(ROPE_THETA) ** (np.arange(0, HEAD_DIM, 2, dtype=f32) / f32(HEAD_DIM)))
    ang = np.arange(first, first + length, dtype=f32)[:, None] * inv[None, :]
    cos, sin = np.cos(ang).astype(f32), np.sin(ang).astype(f32)
    return (jnp.asarray(np.concatenate([cos, cos, cos, cos], axis=1)),
            jnp.asarray(np.concatenate([-sin, -sin, sin, sin], axis=1)))


def _forward(x, meta_tokens, norm_gain, w_in, w_out, lambda_q1, lambda_k1, lambda_q2, lambda_k2,
             subln_gain, final_norm_gain, *, qb, tm, tiles):
    assert qb == KEY_BLOCK and tm % KEY_BLOCK == 0
    b, s, d = x.shape
    w = w_in[0]
    half = HEAD_DIM // 2
    wqk = w[:, 4 * GROUP:6 * GROUP].reshape(d, 2 * GROUP // LANES_V7X, 2, 2, half)
    wqk = wqk.transpose(0, 1, 3, 2, 4).reshape(d, 2 * GROUP)
    wb = jnp.concatenate([w[:, :4 * GROUP], wqk, w[:, 6 * GROUP:]], axis=1).astype(jnp.bfloat16)
    wo = w_out[0].astype(jnp.bfloat16)
    gain = norm_gain[0][None, :]
    x2d = x.reshape(b * s, d)

    px = _project(x2d, gain, wb, *_rope_tables(N_META, s), tm, False)
    sbk_m, sbvt_m, dk_m, dvt_m = _project(meta_tokens, gain, wb, *_rope_tables(0, N_META),
                                          N_META, True)
    sbq, sbk, sbvt, sbg, dq, dk, dvt, dg = px
    rows3 = lambda a: a.reshape(b, s, GROUP)
    blocks = lambda a: a.reshape(b, s // KEY_BLOCK, GROUP, KEY_BLOCK)

    kb = np.arange(KEY_BLOCK)
    later = (kb[None, :] > kb[:, None]).astype(np.float32)
    tri = np.concatenate([later, np.ones((SUBLANES_BF16_V7X, KEY_BLOCK), np.float32)], axis=0)
    trim = np.zeros((N_META, LANES_V7X), np.float32)
    trim[:, :N_META] = later[:N_META, :N_META]
    tri, trim = jnp.asarray(tri, jnp.bfloat16), jnp.asarray(trim, jnp.bfloat16)

    return _attention(
        (rows3(sbq), rows3(sbk), blocks(sbvt), sbk_m, sbvt_m, rows3(sbg), tri, trim),
        (rows3(dq), rows3(dk), blocks(dvt), dk_m, dvt_m, rows3(dg),
         lambda_q1, lambda_k1, lambda_q2, lambda_k2, subln_gain),
        wo, x, final_norm_gain[None, :], qb, tiles)


def kernel(x, meta_tokens, norm_gain, w_in, w_out, lambda_q1, lambda_k1, lambda_q2, lambda_k2,
           subln_gain, final_norm_gain):
    return _forward(x, meta_tokens, norm_gain, w_in, w_out, lambda_q1, lambda_k1, lambda_q2,
                    lambda_k2, subln_gain, final_norm_gain, qb=KEY_BLOCK, tm=512, tiles=4)
```

```python
import functools

import numpy as np
import jax
import jax.numpy as jnp
from jax import lax
from jax.experimental import pallas as pl
from jax.experimental.pallas import tpu as pltpu

LANES_V7X = 128
SUBLANES_BF16_V7X = 16
VMEM_LIMIT_BYTES_V7X = 56 * 1024 * 1024

N_META = 16
CHUNK = 64
ROPE_THETA = 10000.0
RMS_EPS = 1e-6
HEAD_DIM = 64
GROUP = 512
KEY_BLOCK = 256
NEG_BIG = -1e30
LOG2_E = 1.4426950408889634
LOOKAHEAD = 1
EXP_UNDERFLOW = -104.0
LAMBDA_INIT = 0.8 - 0.6 * float(np.exp(-0.3 * 0))


def _rms(x, g):
    return x * lax.rsqrt(jnp.mean(x * x, axis=-1, keepdims=True) + RMS_EPS) * g


def _proj_kernel(x_ref, g_ref, w_ref, cos_ref, sin_ref, *out_refs, meta):
    u = _rms(x_ref[...], g_ref[...]).astype(jnp.bfloat16)

    def col(c):
        return jnp.dot(u, w_ref[:, c * GROUP:(c + 1) * GROUP], preferred_element_type=jnp.float32)

    def col_t(c, rows):
        v = col(c)
        if rows > v.shape[0]:
            v = _pad_rows(v, rows)
        return v.astype(jnp.bfloat16).T

    def silu(g):
        return g / (1.0 + jnp.exp(-g))

    def rope(t):
        cos = jnp.concatenate([cos_ref[...]] * (GROUP // LANES_V7X), axis=1)
        sin = jnp.concatenate([sin_ref[...]] * (GROUP // LANES_V7X), axis=1)
        parts = [pltpu.roll(t[:, j * LANES_V7X:(j + 1) * LANES_V7X], LANES_V7X // 2, 1)
                 for j in range(GROUP // LANES_V7X)]
        return t * cos + jnp.concatenate(parts, axis=1) * sin

    scale = HEAD_DIM ** -0.5
    if meta:
        sbk_ref, sbvt_ref, dk_ref, dvt_ref = out_refs
        sbk_ref[...] = col(1).astype(jnp.bfloat16)
        sbvt_ref[...] = col_t(2, LANES_V7X)
        dk_ref[...] = rope(col(5)).astype(jnp.bfloat16)
        dvt_ref[...] = col_t(6, LANES_V7X)
        return
    sbq_ref, sbk_ref, sbvt_ref, sbg_ref, dq_ref, dk_ref, dvt_ref, dg_ref = out_refs
    sbq_ref[...] = (col(0) * scale).astype(jnp.bfloat16)
    sbk_ref[...] = col(1).astype(jnp.bfloat16)
    sbg_ref[...] = silu(col(3))
    dq_ref[...] = (rope(col(4)) * scale).astype(jnp.bfloat16)
    dk_ref[...] = rope(col(5)).astype(jnp.bfloat16)
    dg_ref[...] = silu(col(7))
    for c, ref in ((2, sbvt_ref), (6, dvt_ref)):
        vt = col_t(c, 0)
        for j in range(ref.shape[0]):
            ref[j] = vt[:, j * KEY_BLOCK:(j + 1) * KEY_BLOCK]


def _project(x2d, gain, wb, cos, sin, tm, meta):
    rows, d = x2d.shape
    n_pos_tiles = cos.shape[0] // tm
    row_spec = lambda width: pl.BlockSpec((tm, width), lambda i: (i, 0))
    tab_spec = pl.BlockSpec((tm, LANES_V7X), lambda i: (i % n_pos_tiles, 0))
    bf, f32 = jnp.bfloat16, jnp.float32
    row_out = lambda dt: (row_spec(GROUP), jax.ShapeDtypeStruct((rows, GROUP), dt))
    if meta:
        vt_out = (pl.BlockSpec((GROUP, LANES_V7X), lambda i: (0, 0)),
                  jax.ShapeDtypeStruct((GROUP, LANES_V7X), bf))
        outs = [row_out(bf), vt_out, row_out(bf), vt_out]
    else:
        vt_out = (pl.BlockSpec((tm // KEY_BLOCK, GROUP, KEY_BLOCK), lambda i: (i, 0, 0)),
                  jax.ShapeDtypeStruct((rows // KEY_BLOCK, GROUP, KEY_BLOCK), bf))
        outs = [row_out(bf), row_out(bf), vt_out, row_out(f32),
                row_out(bf), row_out(bf), vt_out, row_out(f32)]
    return pl.pallas_call(
        functools.partial(_proj_kernel, meta=meta),
        grid=(rows // tm,),
        in_specs=[row_spec(d),
                  pl.BlockSpec((1, d), lambda i: (0, 0)),
                  pl.BlockSpec(wb.shape, lambda i: (0, 0)),
                  tab_spec, tab_spec],
        out_specs=[o[0] for o in outs],
        out_shape=[o[1] for o in outs],
        compiler_params=pltpu.CompilerParams(
            dimension_semantics=("parallel",), vmem_limit_bytes=VMEM_LIMIT_BYTES_V7X),
        name="proj_meta" if meta else "proj",
    )(x2d, gain, wb, cos, sin)


_NT = (((1,), (1,)), ((), ()))


def _masked_queries(q_ref, p, keep):
    q = q_ref[0, :, p * LANES_V7X:(p + 1) * LANES_V7X].astype(jnp.float32)
    return jnp.where(keep, q, 0.0).astype(jnp.bfloat16)


def _pad_rows(a, rows):
    return jnp.concatenate([a, jnp.zeros((rows - a.shape[0], a.shape[1]), a.dtype)], axis=0)


class _Phases:
    pass


def _sb_program(q_ref, k_ref, vt_ref, km_ref, vmt_ref, g_ref, tri_ref, trim_ref, o_ref,
                acc_ref, car_ref, z_ref, *, qb, tiles):
    i = pl.program_id(2)
    lane = lax.broadcasted_iota(jnp.int32, (1, LANES_V7X), 1)
    key = lax.broadcasted_iota(jnp.int32, (KEY_BLOCK, qb), 0)
    qry = lax.broadcasted_iota(jnp.int32, (KEY_BLOCK, qb), 1)
    strict = key < qry
    chains = [(p, hh) for p in range(tiles) for hh in range(2)]
    qms = [_masked_queries(q_ref, p, (lane // HEAD_DIM) == hh) for p, hh in chains]

    def log_sigmoids(z, mask):
        log_beta = jnp.minimum(z, 0.0) - jnp.log(1.0 + jnp.exp2(jnp.abs(z) * (-LOG2_E)))
        log_keep = log_beta - z
        if mask is not None:
            log_keep = jnp.where(mask, log_keep, 0.0)
        return log_beta, log_keep.astype(jnp.bfloat16)

    def scores(n, p, start):
        kblk = k_ref[0, pl.ds(start, KEY_BLOCK), p * LANES_V7X:(p + 1) * LANES_V7X]
        z_ref[n] = lax.dot_general(kblk, qms[n], _NT, preferred_element_type=jnp.float32)

    half = KEY_BLOCK // 2
    strict_half = strict[:half, :half]

    def split(a, diagonal):
        return [a[:half, :half], a[:half, half:], a[half:, half:]] if diagonal else [a]

    def join(parts, diagonal):
        if not diagonal:
            return parts[0]
        older = jnp.concatenate(parts[:2], axis=1)
        newer = jnp.concatenate([jnp.zeros_like(parts[2]), parts[2]], axis=1)
        return jnp.concatenate([older, newer], axis=0)

    def stages(this_start, next_start, vtblk, diagonal, exists=None):
        masks = [strict_half, None, strict_half] if diagonal else [None]
        log_betas, sums = {}, {}

        def stage1(n, p):
            if this_start is not None:
                scores(n, p, this_start)
            parts = [log_sigmoids(z, m) for z, m in zip(split(z_ref[n], diagonal), masks)]
            log_betas[n] = [lb for lb, _ in parts]
            log_keep = join([lk for _, lk in parts], diagonal)
            sums[n] = jnp.dot(tri_ref[...], log_keep, preferred_element_type=jnp.float32)
            if next_start is not None:
                scores(n, p, next_start)

        def stage2(n, p, hh):
            ws = []
            for lb, st, m in zip(log_betas.pop(n), split(sums[n][:KEY_BLOCK], diagonal), masks):
                w = jnp.exp(lb + st)
                ws.append((w if m is None else jnp.where(m, w, 0.0)).astype(jnp.bfloat16))
            tot = sums.pop(n)[KEY_BLOCK:KEY_BLOCK + 1]
            pv = jnp.dot(vtblk(p, hh), join(ws, diagonal), preferred_element_type=jnp.float32)
            if diagonal:
                acc_ref[n] = pv
                car_ref[n] = tot
            else:
                factor = jnp.exp(car_ref[n])
                if exists is not None:
                    factor, tot = factor * exists, tot * exists
                acc_ref[n] += pv * factor
                car_ref[n] += tot

        return ([functools.partial(stage1, n, p) for n, (p, hh) in enumerate(chains)],
                [functools.partial(stage2, n, p, hh) for n, (p, hh) in enumerate(chains)])

    def vt_at(j):
        return lambda p, hh: vt_ref[0, j, pl.ds(p * LANES_V7X + hh * HEAD_DIM, HEAD_DIM), :]

    def block_start(j):
        return pl.multiple_of(jnp.maximum(j, 0) * KEY_BLOCK, KEY_BLOCK)

    def prologue():
        for n, (p, hh) in enumerate(chains):
            scores(n, p, block_start(i))

    def older_blocks():
        def largest_carry():
            return jnp.max(functools.reduce(jnp.maximum,
                                            [car_ref[n] for n in range(len(chains))]))

        def live(state):
            t, carry_max = state
            return jnp.logical_and(t < i, carry_max > EXP_UNDERFLOW)

        def body(state):
            t, _ = state
            j = i - 1 - t
            first, second = stages(block_start(j), None, vt_at(j), False)
            for f in first + second:
                f()
            return t + 1, largest_carry()

        _, carry_max = lax.while_loop(live, body, (jnp.int32(1), largest_carry()))

        @pl.when(carry_max > EXP_UNDERFLOW)
        def _meta():
            zs = [lax.dot_general(km_ref[:, p * LANES_V7X:(p + 1) * LANES_V7X], qms[n], _NT,
                                  preferred_element_type=jnp.float32)
                  for n, (p, hh) in enumerate(chains)]
            log_betas, sticks = [], []
            for z in zs:
                log_beta, log_keep = log_sigmoids(z, None)
                sticks.append(jnp.dot(trim_ref[...], _pad_rows(log_keep, LANES_V7X),
                                      preferred_element_type=jnp.float32))
                log_betas.append(log_beta)
            ws = [_pad_rows(jnp.exp(log_betas[n] + sticks[n] + car_ref[n]).astype(jnp.bfloat16),
                            LANES_V7X) for n in range(len(chains))]
            for n, (p, hh) in enumerate(chains):
                acc_ref[n] += jnp.dot(vmt_ref[pl.ds(p * LANES_V7X + hh * HEAD_DIM, HEAD_DIM), :],
                                      ws[n], preferred_element_type=jnp.float32)

    def finalize():
        for p in range(tiles):
            o = jnp.concatenate([acc_ref[2 * p], acc_ref[2 * p + 1]], axis=0).T
            o_ref[0, :, p * LANES_V7X:(p + 1) * LANES_V7X] = (
                o * g_ref[0, :, p * LANES_V7X:(p + 1) * LANES_V7X]).astype(o_ref.dtype)

    ph = _Phases()
    ph.prologue = prologue
    ph.diagonal = stages(None, block_start(i - 1), vt_at(i), True)
    ph.older = stages(None, None, vt_at(jnp.maximum(i - 1, 0)), False,
                      exists=jnp.where(i > 0, 1.0, 0.0).astype(jnp.float32))
    ph.older_blocks = older_blocks
    ph.finalize = finalize
    return ph


def _diff_program(q_ref, k_ref, vt_ref, km_ref, vmt_ref, g_ref,
                  lq1_ref, lk1_ref, lq2_ref, lk2_ref, sub_ref, o_ref, acc_ref, m_ref, l_ref, z_ref,
                  *, qb, tiles):
    i = pl.program_id(2)
    lane = lax.broadcasted_iota(jnp.int32, (1, LANES_V7X), 1)
    key = lax.broadcasted_iota(jnp.int32, (KEY_BLOCK, qb), 0)
    qry = lax.broadcasted_iota(jnp.int32, (KEY_BLOCK, qb), 1)
    chunk_ok = (qry // CHUNK) >= (key // CHUNK)
    chains = [(p, c) for p in range(tiles) for c in range(2)]
    qms = [_masked_queries(q_ref, p, ((lane // (HEAD_DIM // 2)) % 2) == c) for p, c in chains]
    ones = jnp.ones((SUBLANES_BF16_V7X, KEY_BLOCK), jnp.bfloat16)

    def update(n, s, vt_rows, pad_to, first):
        m_blk = jnp.max(s, axis=0, keepdims=True)
        m_new = m_blk if first else jnp.maximum(m_ref[n], m_blk)
        pr = jnp.exp(s - m_new).astype(jnp.bfloat16)
        if pad_to is not None:
            pr = _pad_rows(pr, pad_to)
        pv = jnp.dot(vt_rows, pr, preferred_element_type=jnp.float32)
        num, den = pv[:LANES_V7X], pv[LANES_V7X:LANES_V7X + 1]
        if first:
            l_ref[n] = den
            acc_ref[n] = num
        else:
            alpha = jnp.exp(m_ref[n] - m_new)
            l_ref[n] = l_ref[n] * alpha + den
            acc_ref[n] = acc_ref[n] * alpha + num
        m_ref[n] = m_new

    def scores(n, p, start):
        kblk = k_ref[0, pl.ds(start, KEY_BLOCK), p * LANES_V7X:(p + 1) * LANES_V7X]
        z_ref[n] = lax.dot_general(kblk, qms[n], _NT, preferred_element_type=jnp.float32)

    def block_start(j):
        return pl.multiple_of(j * KEY_BLOCK, KEY_BLOCK)

    def block(j, next_start, mask, first):
        ahead = {}

        def next_scores(n):
            p = chains[n][0]
            kblk = k_ref[0, pl.ds(next_start, KEY_BLOCK), p * LANES_V7X:(p + 1) * LANES_V7X]
            ahead[n] = lax.dot_general(kblk, qms[n], _NT, preferred_element_type=jnp.float32)

        def chain(n, p):
            if next_start is not None:
                for m in (range(LOOKAHEAD) if n == 0 else ()):
                    next_scores(m)
                if n + LOOKAHEAD < len(chains):
                    next_scores(n + LOOKAHEAD)
            s = z_ref[n] if mask is None else jnp.where(mask, z_ref[n], NEG_BIG)
            vt_rows = jnp.concatenate([vt_ref[0, j, p * LANES_V7X:(p + 1) * LANES_V7X, :], ones],
                                      axis=0)
            update(n, s, vt_rows, None, first)
            if next_start is not None:
                z_ref[n] = ahead.pop(n)

        return [functools.partial(chain, n, p) for n, (p, c) in enumerate(chains)]

    meta_scores = {}

    def diagonal_scores(n, p):
        scores(n, p, pl.multiple_of(i * qb, qb))

    def score_meta():
        for n, (p, c) in enumerate(chains):
            meta_scores[n] = lax.dot_general(km_ref[:, p * LANES_V7X:(p + 1) * LANES_V7X], qms[n],
                                             _NT, preferred_element_type=jnp.float32)

    def meta(n, p):
        vt_rows = jnp.concatenate([vmt_ref[p * LANES_V7X:(p + 1) * LANES_V7X, :],
                                   ones[:, :LANES_V7X]], axis=0)
        update(n, meta_scores.pop(n), vt_rows, LANES_V7X, False)

    def older_blocks():
        def body(t, c):
            for f in block(t, block_start(t + 1), None, False):
                f()
            return c

        lax.fori_loop(0, i - 1, body, 0)

        @pl.when(i > 0)
        def _last():
            for f in block(i - 1, None, None, False):
                f()

    def finalize():
        lam = (jnp.exp(jnp.sum(lq1_ref[...] * lk1_ref[...], axis=-1, keepdims=True))
               - jnp.exp(jnp.sum(lq2_ref[...] * lk2_ref[...], axis=-1, keepdims=True))
               + LAMBDA_INIT)
        for p in range(tiles):
            ot = (acc_ref[2 * p] * (1.0 / l_ref[2 * p])
                  - acc_ref[2 * p + 1] * (lam / l_ref[2 * p + 1]))
            ot = ot * lax.rsqrt(jnp.mean(ot * ot, axis=0, keepdims=True) + RMS_EPS)
            o = ot.T * (sub_ref[...] * (1.0 - LAMBDA_INIT))
            o_ref[0, :, p * LANES_V7X:(p + 1) * LANES_V7X] = (
                o * g_ref[0, :, p * LANES_V7X:(p + 1) * LANES_V7X]).astype(o_ref.dtype)

    ph = _Phases()
    ph.diagonal_scores = [functools.partial(diagonal_scores, n, p)
                          for n, (p, c) in enumerate(chains)]
    ph.score_meta = score_meta
    ph.diagonal = block(i, block_start(0), chunk_ok, True)
    ph.meta = [functools.partial(meta, n, p) for n, (p, c) in enumerate(chains)]
    ph.older_blocks = older_blocks
    ph.finalize = finalize
    return ph


N_SB_IN, N_DF_IN = 8, 11


def _attn_kernel(*refs, qb, tiles):
    sb_in, df_in = refs[:N_SB_IN], refs[N_SB_IN:N_SB_IN + N_DF_IN]
    wo_ref, x_ref, fg_ref, y_ref = refs[N_SB_IN + N_DF_IN:N_SB_IN + N_DF_IN + 4]
    scratch = refs[N_SB_IN + N_DF_IN + 4:]
    mix_sb, mix_df = scratch[7:]
    sb = _sb_program(*sb_in, mix_sb, *scratch[:3], qb=qb, tiles=tiles)
    df = _diff_program(*df_in, mix_df, *scratch[3:7], qb=qb, tiles=tiles)
    sb.prologue()
    for first, other in zip(sb.diagonal[0], df.diagonal_scores):
        first()
        other()
    df.score_meta()
    for second, other in zip(sb.diagonal[1], df.diagonal):
        second()
        other()
    for first, other in zip(sb.older[0], df.meta):
        first()
        other()
    for second in sb.older[1]:
        second()
    df.older_blocks()
    sb.older_blocks()
    df.finalize()
    sb.finalize()
    half = mix_sb.shape[-1]
    h = (x_ref[0]
         + jnp.dot(mix_sb[0], wo_ref[:half, :], preferred_element_type=jnp.float32)
         + jnp.dot(mix_df[0], wo_ref[half:, :], preferred_element_type=jnp.float32))
    y_ref[0] = _rms(h, fg_ref[...])


def _attention(sb_args, df_args, wo, x, final_gain, qb, tiles):
    q = sb_args[0]
    b, s, width = q.shape
    tw = tiles * LANES_V7X
    assert tw == width, "the fused kernel handles every head of a batch element per grid step"
    qspec = pl.BlockSpec((1, qb, tw), lambda bi, p, i: (bi, i, p))
    kspec = pl.BlockSpec((1, s, tw), lambda bi, p, i: (bi, 0, p))
    vtspec = pl.BlockSpec((1, s // KEY_BLOCK, tw, KEY_BLOCK), lambda bi, p, i: (bi, 0, p, 0))
    kmspec = pl.BlockSpec((N_META, tw), lambda bi, p, i: (0, p))
    vmtspec = pl.BlockSpec((tw, LANES_V7X), lambda bi, p, i: (p, 0))
    full = lambda a: pl.BlockSpec(a.shape, lambda bi, p, i: (0,) * a.ndim)
    common = [qspec, kspec, vtspec, kmspec, vmtspec, qspec]
    xspec = pl.BlockSpec((1, qb, x.shape[-1]), lambda bi, p, i: (bi, i, 0))
    in_specs = (common + [full(a) for a in sb_args[6:]] + common + [full(a) for a in df_args[6:]]
                + [full(wo), xspec, full(final_gain)])
    assert len(sb_args) == N_SB_IN and len(df_args) == N_DF_IN
    stat = pltpu.VMEM((2 * tiles, 1, qb), jnp.float32)
    score = pltpu.VMEM((2 * tiles, KEY_BLOCK, qb), jnp.float32)
    mix = pltpu.VMEM((1, qb, width), jnp.bfloat16)
    return pl.pallas_call(
        functools.partial(_attn_kernel, qb=qb, tiles=tiles),
        grid=(b, width // tw, s // qb),
        in_specs=in_specs,
        out_specs=xspec,
        out_shape=jax.ShapeDtypeStruct(x.shape, jnp.float32),
        scratch_shapes=[pltpu.VMEM((2 * tiles, HEAD_DIM, qb), jnp.float32), stat, score,
                        pltpu.VMEM((2 * tiles, LANES_V7X, qb), jnp.float32), stat, stat, score,
                        mix, mix],
        compiler_params=pltpu.CompilerParams(
            dimension_semantics=("parallel", "parallel", "arbitrary"),
            vmem_limit_bytes=VMEM_LIMIT_BYTES_V7X),
        name="attn",
    )(*sb_args, *df_args, wo, x, final_gain)


def _rope_tables(first, length):
    f32 = np.float32
    inv = f32(1.0) / (f32(ROPE_THETA) ** (np.arange(0, HEAD_DIM, 2, dtype=f32) / f32(HEAD_DIM)))
    ang = np.arange(first, first + length, dtype=f32)[:, None] * inv[None, :]
    cos, sin = np.cos(ang).astype(f32), np.sin(ang).astype(f32)
    return (jnp.asarray(np.concatenate([cos, cos, cos, cos], axis=1)),
            jnp.asarray(np.concatenate([-sin, -sin, sin, sin], axis=1)))


def _forward(x, meta_tokens, norm_gain, w_in, w_out, lambda_q1, lambda_k1, lambda_q2, lambda_k2,
             subln_gain, final_norm_gain, *, qb, tm, tiles):
    assert qb == KEY_BLOCK and tm % KEY_BLOCK == 0
    b, s, d = x.shape
    w = w_in[0]
    half = HEAD_DIM // 2
    wqk = w[:, 4 * GROUP:6 * GROUP].reshape(d, 2 * GROUP // LANES_V7X, 2, 2, half)
    wqk = wqk.transpose(0, 1, 3, 2, 4).reshape(d, 2 * GROUP)
    wb = jnp.concatenate([w[:, :4 * GROUP], wqk, w[:, 6 * GROUP:]], axis=1).astype(jnp.bfloat16)
    wo = w_out[0].astype(jnp.bfloat16)
    gain = norm_gain[0][None, :]
    x2d = x.reshape(b * s, d)

    px = _project(x2d, gain, wb, *_rope_tables(N_META, s), tm, False)
    sbk_m, sbvt_m, dk_m, dvt_m = _project(meta_tokens, gain, wb, *_rope_tables(0, N_META),
                                          N_META, True)
    sbq, sbk, sbvt, sbg, dq, dk, dvt, dg = px
    rows3 = lambda a: a.reshape(b, s, GROUP)
    blocks = lambda a: a.reshape(b, s // KEY_BLOCK, GROUP, KEY_BLOCK)

    kb = np.arange(KEY_BLOCK)
    later = (kb[None, :] > kb[:, None]).astype(np.float32)
    tri = np.concatenate([later, np.ones((SUBLANES_BF16_V7X, KEY_BLOCK), np.float32)], axis=0)
    trim = np.zeros((N_META, LANES_V7X), np.float32)
    trim[:, :N_META] = later[:N_META, :N_META]
    tri, trim = jnp.asarray(tri, jnp.bfloat16), jnp.asarray(trim, jnp.bfloat16)

    return _attention(
        (rows3(sbq), rows3(sbk), blocks(sbvt), sbk_m, sbvt_m, rows3(sbg), tri, trim),
        (rows3(dq), rows3(dk), blocks(dvt), dk_m, dvt_m, rows3(dg),
         lambda_q1, lambda_k1, lambda_q2, lambda_k2, subln_gain),
        wo, x, final_norm_gain[None, :], qb, tiles)


def kernel(x, meta_tokens, norm_gain, w_in, w_out, lambda_q1, lambda_k1, lambda_q2, lambda_k2,
           subln_gain, final_norm_gain):
    return _forward(x, meta_tokens, norm_gain, w_in, w_out, lambda_q1, lambda_k1, lambda_q2,
                    lambda_k2, subln_gain, final_norm_gain, qb=KEY_BLOCK, tm=512, tiles=4)
```

```python
import functools

import numpy as np
import jax
import jax.numpy as jnp
from jax import lax
from jax.experimental import pallas as pl
from jax.experimental.pallas import tpu as pltpu

LANES_V7X = 128
SUBLANES_BF16_V7X = 16
VMEM_LIMIT_BYTES_V7X = 56 * 1024 * 1024

N_META = 16
CHUNK = 64
ROPE_THETA = 10000.0
RMS_EPS = 1e-6
HEAD_DIM = 64
GROUP = 512
KEY_BLOCK = 256
NEG_BIG = -1e30
LOG2_E = 1.4426950408889634
LOOKAHEAD = 1
EXP_UNDERFLOW = -104.0
LAMBDA_INIT = 0.8 - 0.6 * float(np.exp(-0.3 * 0))


def _rms(x, g):
    return x * lax.rsqrt(jnp.mean(x * x, axis=-1, keepdims=True) + RMS_EPS) * g


def _proj_kernel(x_ref, g_ref, w_ref, cos_ref, sin_ref, *out_refs, meta):
    u = _rms(x_ref[...], g_ref[...]).astype(jnp.bfloat16)

    def col(c):
        return jnp.dot(u, w_ref[:, c * GROUP:(c + 1) * GROUP], preferred_element_type=jnp.float32)

    def col_t(c, rows):
        v = col(c)
        if rows > v.shape[0]:
            v = _pad_rows(v, rows)
        return v.astype(jnp.bfloat16).T

    def silu(g):
        return g / (1.0 + jnp.exp(-g))

    def rope(t):
        half = HEAD_DIM // 2
        lane = lax.broadcasted_iota(jnp.int32, (1, LANES_V7X), 1)
        first_half = (lane // half) % 2 == 0
        cos = jnp.concatenate([cos_ref[...]] * (GROUP // LANES_V7X), axis=1)
        sin = jnp.concatenate([sin_ref[...]] * (GROUP // LANES_V7X), axis=1)
        parts = []
        for j in range(GROUP // LANES_V7X):
            tile = t[:, j * LANES_V7X:(j + 1) * LANES_V7X]
            parts.append(jnp.where(first_half, pltpu.roll(tile, LANES_V7X - half, 1),
                                   pltpu.roll(tile, half, 1)))
        return t * cos + jnp.concatenate(parts, axis=1) * sin

    scale = HEAD_DIM ** -0.5
    if meta:
        sbk_ref, sbvt_ref, dk_ref, dvt_ref = out_refs
        sbk_ref[...] = col(1).astype(jnp.bfloat16)
        sbvt_ref[...] = col_t(2, LANES_V7X)
        dk_ref[...] = rope(col(5)).astype(jnp.bfloat16)
        dvt_ref[...] = col_t(6, LANES_V7X)
        return
    sbq_ref, sbk_ref, sbvt_ref, sbg_ref, dq_ref, dk_ref, dvt_ref, dg_ref = out_refs
    sbq_ref[...] = (col(0) * scale).astype(jnp.bfloat16)
    sbk_ref[...] = col(1).astype(jnp.bfloat16)
    sbg_ref[...] = silu(col(3))
    dq_ref[...] = (rope(col(4)) * scale).astype(jnp.bfloat16)
    dk_ref[...] = rope(col(5)).astype(jnp.bfloat16)
    dg_ref[...] = silu(col(7))
    for c, ref in ((2, sbvt_ref), (6, dvt_ref)):
        vt = col_t(c, 0)
        for j in range(ref.shape[0]):
            ref[j] = vt[:, j * KEY_BLOCK:(j + 1) * KEY_BLOCK]


def _project(x2d, gain, wb, cos, sin, tm, meta):
    rows, d = x2d.shape
    n_pos_tiles = cos.shape[0] // tm
    row_spec = lambda width: pl.BlockSpec((tm, width), lambda i: (i, 0))
    tab_spec = pl.BlockSpec((tm, LANES_V7X), lambda i: (i % n_pos_tiles, 0))
    bf, f32 = jnp.bfloat16, jnp.float32
    row_out = lambda dt: (row_spec(GROUP), jax.ShapeDtypeStruct((rows, GROUP), dt))
    if meta:
        vt_out = (pl.BlockSpec((GROUP, LANES_V7X), lambda i: (0, 0)),
                  jax.ShapeDtypeStruct((GROUP, LANES_V7X), bf))
        outs = [row_out(bf), vt_out, row_out(bf), vt_out]
    else:
        vt_out = (pl.BlockSpec((tm // KEY_BLOCK, GROUP, KEY_BLOCK), lambda i: (i, 0, 0)),
                  jax.ShapeDtypeStruct((rows // KEY_BLOCK, GROUP, KEY_BLOCK), bf))
        outs = [row_out(bf), row_out(bf), vt_out, row_out(f32),
                row_out(bf), row_out(bf), vt_out, row_out(f32)]
    return pl.pallas_call(
        functools.partial(_proj_kernel, meta=meta),
        grid=(rows // tm,),
        in_specs=[row_spec(d),
                  pl.BlockSpec((1, d), lambda i: (0, 0)),
                  pl.BlockSpec(wb.shape, lambda i: (0, 0)),
                  tab_spec, tab_spec],
        out_specs=[o[0] for o in outs],
        out_shape=[o[1] for o in outs],
        compiler_params=pltpu.CompilerParams(
            dimension_semantics=("parallel",), vmem_limit_bytes=VMEM_LIMIT_BYTES_V7X),
        name="proj_meta" if meta else "proj",
    )(x2d, gain, wb, cos, sin)


_NT = (((1,), (1,)), ((), ()))


def _masked_queries(q_ref, p, keep):
    q = q_ref[0, :, p * LANES_V7X:(p + 1) * LANES_V7X].astype(jnp.float32)
    return jnp.where(keep, q, 0.0).astype(jnp.bfloat16)


def _pad_rows(a, rows):
    return jnp.concatenate([a, jnp.zeros((rows - a.shape[0], a.shape[1]), a.dtype)], axis=0)


class _Phases:
    pass


def _sb_program(q_ref, k_ref, vt_ref, km_ref, vmt_ref, g_ref, tri_ref, trim_ref, o_ref,
                acc_ref, car_ref, z_ref, *, qb, tiles):
    i = pl.program_id(2)
    lane = lax.broadcasted_iota(jnp.int32, (1, LANES_V7X), 1)
    key = lax.broadcasted_iota(jnp.int32, (KEY_BLOCK, qb), 0)
    qry = lax.broadcasted_iota(jnp.int32, (KEY_BLOCK, qb), 1)
    strict = key < qry
    chains = [(p, hh) for p in range(tiles) for hh in range(2)]
    qms = [_masked_queries(q_ref, p, (lane // HEAD_DIM) == hh) for p, hh in chains]

    def log_sigmoids(z, mask):
        log_beta = jnp.minimum(z, 0.0) - jnp.log(1.0 + jnp.exp2(jnp.abs(z) * (-LOG2_E)))
        log_keep = log_beta - z
        if mask is not None:
            log_keep = jnp.where(mask, log_keep, 0.0)
        return log_beta, log_keep.astype(jnp.bfloat16)

    def scores(n, p, start):
        kblk = k_ref[0, pl.ds(start, KEY_BLOCK), p * LANES_V7X:(p + 1) * LANES_V7X]
        z_ref[n] = lax.dot_general(kblk, qms[n], _NT, preferred_element_type=jnp.float32)

    half = KEY_BLOCK // 2
    strict_half = strict[:half, :half]

    def split(a, diagonal):
        return [a[:half, :half], a[:half, half:], a[half:, half:]] if diagonal else [a]

    def join(parts, diagonal):
        if not diagonal:
            return parts[0]
        older = jnp.concatenate(parts[:2], axis=1)
        newer = jnp.concatenate([jnp.zeros_like(parts[2]), parts[2]], axis=1)
        return jnp.concatenate([older, newer], axis=0)

    def stages(this_start, next_start, vtblk, diagonal, exists=None):
        masks = [strict_half, None, strict_half] if diagonal else [None]
        log_betas, sums = {}, {}

        def stage1(n, p):
            if this_start is not None:
                scores(n, p, this_start)
            parts = [log_sigmoids(z, m) for z, m in zip(split(z_ref[n], diagonal), masks)]
            log_betas[n] = [lb for lb, _ in parts]
            log_keep = join([lk for _, lk in parts], diagonal)
            sums[n] = jnp.dot(tri_ref[...], log_keep, preferred_element_type=jnp.float32)
            if next_start is not None:
                scores(n, p, next_start)

        def stage2(n, p, hh):
            ws = []
            for lb, st, m in zip(log_betas.pop(n), split(sums[n][:KEY_BLOCK], diagonal), masks):
                w = jnp.exp(lb + st)
                ws.append((w if m is None else jnp.where(m, w, 0.0)).astype(jnp.bfloat16))
            tot = sums.pop(n)[KEY_BLOCK:KEY_BLOCK + 1]
            pv = jnp.dot(vtblk(p, hh), join(ws, diagonal), preferred_element_type=jnp.float32)
            if diagonal:
                acc_ref[n] = pv
                car_ref[n] = tot
            else:
                factor = jnp.exp(car_ref[n])
                if exists is not None:
                    factor, tot = factor * exists, tot * exists
                acc_ref[n] += pv * factor
                car_ref[n] += tot

        return ([functools.partial(stage1, n, p) for n, (p, hh) in enumerate(chains)],
                [functools.partial(stage2, n, p, hh) for n, (p, hh) in enumerate(chains)])

    def vt_at(j):
        return lambda p, hh: vt_ref[0, j, pl.ds(p * LANES_V7X + hh * HEAD_DIM, HEAD_DIM), :]

    def block_start(j):
        return pl.multiple_of(jnp.maximum(j, 0) * KEY_BLOCK, KEY_BLOCK)

    def prologue():
        for n, (p, hh) in enumerate(chains):
            scores(n, p, block_start(i))

    def older_blocks():
        def largest_carry():
            return jnp.max(functools.reduce(jnp.maximum,
                                            [car_ref[n] for n in range(len(chains))]))

        def live(state):
            t, carry_max = state
            return jnp.logical_and(t < i, carry_max > EXP_UNDERFLOW)

        def body(state):
            t, _ = state
            j = i - 1 - t
            first, second = stages(block_start(j), None, vt_at(j), False)
            for f in first + second:
                f()
            return t + 1, largest_carry()

        _, carry_max = lax.while_loop(live, body, (jnp.int32(1), largest_carry()))

        @pl.when(carry_max > EXP_UNDERFLOW)
        def _meta():
            zs = [lax.dot_general(km_ref[:, p * LANES_V7X:(p + 1) * LANES_V7X], qms[n], _NT,
                                  preferred_element_type=jnp.float32)
                  for n, (p, hh) in enumerate(chains)]
            log_betas, sticks = [], []
            for z in zs:
                log_beta, log_keep = log_sigmoids(z, None)
                sticks.append(jnp.dot(trim_ref[...], _pad_rows(log_keep, LANES_V7X),
                                      preferred_element_type=jnp.float32))
                log_betas.append(log_beta)
            ws = [_pad_rows(jnp.exp(log_betas[n] + sticks[n] + car_ref[n]).astype(jnp.bfloat16),
                            LANES_V7X) for n in range(len(chains))]
            for n, (p, hh) in enumerate(chains):
                acc_ref[n] += jnp.dot(vmt_ref[pl.ds(p * LANES_V7X + hh * HEAD_DIM, HEAD_DIM), :],
                                      ws[n], preferred_element_type=jnp.float32)

    def finalize():
        for p in range(tiles):
            o = jnp.concatenate([acc_ref[2 * p], acc_ref[2 * p + 1]], axis=0).T
            o_ref[0, :, p * LANES_V7X:(p + 1) * LANES_V7X] = (
                o * g_ref[0, :, p * LANES_V7X:(p + 1) * LANES_V7X]).astype(o_ref.dtype)

    ph = _Phases()
    ph.prologue = prologue
    ph.diagonal = stages(None, block_start(i - 1), vt_at(i), True)
    ph.older = stages(None, None, vt_at(jnp.maximum(i - 1, 0)), False,
                      exists=jnp.where(i > 0, 1.0, 0.0).astype(jnp.float32))
    ph.older_blocks = older_blocks
    ph.finalize = finalize
    return ph


def _diff_program(q_ref, k_ref, vt_ref, km_ref, vmt_ref, g_ref,
                  lq1_ref, lk1_ref, lq2_ref, lk2_ref, sub_ref, o_ref, acc_ref, m_ref, l_ref, z_ref,
                  *, qb, tiles):
    i = pl.program_id(2)
    lane = lax.broadcasted_iota(jnp.int32, (1, LANES_V7X), 1)
    key = lax.broadcasted_iota(jnp.int32, (KEY_BLOCK, qb), 0)
    qry = lax.broadcasted_iota(jnp.int32, (KEY_BLOCK, qb), 1)
    chunk_ok = (qry // CHUNK) >= (key // CHUNK)
    chains = [(p, c) for p in range(tiles) for c in range(2)]
    qms = [_masked_queries(q_ref, p, (lane // HEAD_DIM) == c) for p, c in chains]
    ones = jnp.ones((SUBLANES_BF16_V7X, KEY_BLOCK), jnp.bfloat16)

    def update(n, s, vt_rows, pad_to, first):
        m_blk = jnp.max(s, axis=0, keepdims=True)
        m_new = m_blk if first else jnp.maximum(m_ref[n], m_blk)
        pr = jnp.exp(s - m_new).astype(jnp.bfloat16)
        if pad_to is not None:
            pr = _pad_rows(pr, pad_to)
        pv = jnp.dot(vt_rows, pr, preferred_element_type=jnp.float32)
        num, den = pv[:LANES_V7X], pv[LANES_V7X:LANES_V7X + 1]
        if first:
            l_ref[n] = den
            acc_ref[n] = num
        else:
            alpha = jnp.exp(m_ref[n] - m_new)
            l_ref[n] = l_ref[n] * alpha + den
            acc_ref[n] = acc_ref[n] * alpha + num
        m_ref[n] = m_new

    def scores(n, p, start):
        kblk = k_ref[0, pl.ds(start, KEY_BLOCK), p * LANES_V7X:(p + 1) * LANES_V7X]
        z_ref[n] = lax.dot_general(kblk, qms[n], _NT, preferred_element_type=jnp.float32)

    def block_start(j):
        return pl.multiple_of(j * KEY_BLOCK, KEY_BLOCK)

    def block(j, next_start, mask, first):
        ahead = {}

        def next_scores(n):
            p = chains[n][0]
            kblk = k_ref[0, pl.ds(next_start, KEY_BLOCK), p * LANES_V7X:(p + 1) * LANES_V7X]
            ahead[n] = lax.dot_general(kblk, qms[n], _NT, preferred_element_type=jnp.float32)

        def chain(n, p):
            if next_start is not None:
                for m in (range(LOOKAHEAD) if n == 0 else ()):
                    next_scores(m)
                if n + LOOKAHEAD < len(chains):
                    next_scores(n + LOOKAHEAD)
            s = z_ref[n] if mask is None else jnp.where(mask, z_ref[n], NEG_BIG)
            vt_rows = jnp.concatenate([vt_ref[0, j, p * LANES_V7X:(p + 1) * LANES_V7X, :], ones],
                                      axis=0)
            update(n, s, vt_rows, None, first)
            if next_start is not None:
                z_ref[n] = ahead.pop(n)

        return [functools.partial(chain, n, p) for n, (p, c) in enumerate(chains)]

    meta_scores = {}

    def diagonal_scores(n, p):
        scores(n, p, pl.multiple_of(i * qb, qb))

    def score_meta():
        for n, (p, c) in enumerate(chains):
            meta_scores[n] = lax.dot_general(km_ref[:, p * LANES_V7X:(p + 1) * LANES_V7X], qms[n],
                                             _NT, preferred_element_type=jnp.float32)

    def meta(n, p):
        vt_rows = jnp.concatenate([vmt_ref[p * LANES_V7X:(p + 1) * LANES_V7X, :],
                                   ones[:, :LANES_V7X]], axis=0)
        update(n, meta_scores.pop(n), vt_rows, LANES_V7X, False)

    def older_blocks():
        def body(t, c):
            for f in block(t, block_start(t + 1), None, False):
                f()
            return c

        lax.fori_loop(0, i - 1, body, 0)

        @pl.when(i > 0)
        def _last():
            for f in block(i - 1, None, None, False):
                f()

    def finalize():
        lam = (jnp.exp(jnp.sum(lq1_ref[...] * lk1_ref[...], axis=-1, keepdims=True))
               - jnp.exp(jnp.sum(lq2_ref[...] * lk2_ref[...], axis=-1, keepdims=True))
               + LAMBDA_INIT)
        for p in range(tiles):
            ot = (acc_ref[2 * p] * (1.0 / l_ref[2 * p])
                  - acc_ref[2 * p + 1] * (lam / l_ref[2 * p + 1]))
            ot = ot * lax.rsqrt(jnp.mean(ot * ot, axis=0, keepdims=True) + RMS_EPS)
            o = ot.T * (sub_ref[...] * (1.0 - LAMBDA_INIT))
            o_ref[0, :, p * LANES_V7X:(p + 1) * LANES_V7X] = (
                o * g_ref[0, :, p * LANES_V7X:(p + 1) * LANES_V7X]).astype(o_ref.dtype)

    ph = _Phases()
    ph.diagonal_scores = [functools.partial(diagonal_scores, n, p)
                          for n, (p, c) in enumerate(chains)]
    ph.score_meta = score_meta
    ph.diagonal = block(i, block_start(0), chunk_ok, True)
    ph.meta = [functools.partial(meta, n, p) for n, (p, c) in enumerate(chains)]
    ph.older_blocks = older_blocks
    ph.finalize = finalize
    return ph


N_SB_IN, N_DF_IN = 8, 11


def _attn_kernel(*refs, qb, tiles):
    sb_in, df_in = refs[:N_SB_IN], refs[N_SB_IN:N_SB_IN + N_DF_IN]
    wo_ref, x_ref, fg_ref, y_ref = refs[N_SB_IN + N_DF_IN:N_SB_IN + N_DF_IN + 4]
    scratch = refs[N_SB_IN + N_DF_IN + 4:]
    mix_sb, mix_df = scratch[7:]
    sb = _sb_program(*sb_in, mix_sb, *scratch[:3], qb=qb, tiles=tiles)
    df = _diff_program(*df_in, mix_df, *scratch[3:7], qb=qb, tiles=tiles)
    sb.prologue()
    for first, other in zip(sb.diagonal[0], df.diagonal_scores):
        first()
        other()
    df.score_meta()
    for second, other in zip(sb.diagonal[1], df.diagonal):
        second()
        other()
    for first, other in zip(sb.older[0], df.meta):
        first()
        other()
    for second in sb.older[1]:
        second()
    df.older_blocks()
    sb.older_blocks()
    df.finalize()
    sb.finalize()
    half = mix_sb.shape[-1]
    h = (x_ref[0]
         + jnp.dot(mix_sb[0], wo_ref[:half, :], preferred_element_type=jnp.float32)
         + jnp.dot(mix_df[0], wo_ref[half:, :], preferred_element_type=jnp.float32))
    y_ref[0] = _rms(h, fg_ref[...])


def _attention(sb_args, df_args, wo, x, final_gain, qb, tiles):
    q = sb_args[0]
    b, s, width = q.shape
    tw = tiles * LANES_V7X
    assert tw == width, "the fused kernel handles every head of a batch element per grid step"
    qspec = pl.BlockSpec((1, qb, tw), lambda bi, p, i: (bi, i, p))
    kspec = pl.BlockSpec((1, s, tw), lambda bi, p, i: (bi, 0, p))
    vtspec = pl.BlockSpec((1, s // KEY_BLOCK, tw, KEY_BLOCK), lambda bi, p, i: (bi, 0, p, 0))
    kmspec = pl.BlockSpec((N_META, tw), lambda bi, p, i: (0, p))
    vmtspec = pl.BlockSpec((tw, LANES_V7X), lambda bi, p, i: (p, 0))
    full = lambda a: pl.BlockSpec(a.shape, lambda bi, p, i: (0,) * a.ndim)
    common = [qspec, kspec, vtspec, kmspec, vmtspec, qspec]
    xspec = pl.BlockSpec((1, qb, x.shape[-1]), lambda bi, p, i: (bi, i, 0))
    in_specs = (common + [full(a) for a in sb_args[6:]] + common + [full(a) for a in df_args[6:]]
                + [full(wo), xspec, full(final_gain)])
    assert len(sb_args) == N_SB_IN and len(df_args) == N_DF_IN
    stat = pltpu.VMEM((2 * tiles, 1, qb), jnp.float32)
    score = pltpu.VMEM((2 * tiles, KEY_BLOCK, qb), jnp.float32)
    mix = pltpu.VMEM((1, qb, width), jnp.bfloat16)
    return pl.pallas_call(
        functools.partial(_attn_kernel, qb=qb, tiles=tiles),
        grid=(b, width // tw, s // qb),
        in_specs=in_specs,
        out_specs=xspec,
        out_shape=jax.ShapeDtypeStruct(x.shape, jnp.float32),
        scratch_shapes=[pltpu.VMEM((2 * tiles, HEAD_DIM, qb), jnp.float32), stat, score,
                        pltpu.VMEM((2 * tiles, LANES_V7X, qb), jnp.float32), stat, stat, score,
                        mix, mix],
        compiler_params=pltpu.CompilerParams(
            dimension_semantics=("parallel", "parallel", "arbitrary"),
            vmem_limit_bytes=VMEM_LIMIT_BYTES_V7X),
        name="attn",
    )(*sb_args, *df_args, wo, x, final_gain)


def _rope_tables(first, length):
    f32 = np.float32
    inv = f32(1.0) / (f32(ROPE_THETA) ** (np.arange(0, HEAD_DIM, 2, dtype=f32) / f32(HEAD_DIM)))
    ang = np.arange(first, first + length, dtype=f32)[:, None] * inv[None, :]
    cos, sin = np.cos(ang).astype(f32), np.sin(ang).astype(f32)
    return (jnp.asarray(np.concatenate([cos, cos, cos, cos], axis=1)),
            jnp.asarray(np.concatenate([-sin, sin, -sin, sin], axis=1)))


def _forward(x, meta_tokens, norm_gain, w_in, w_out, lambda_q1, lambda_k1, lambda_q2, lambda_k2,
             subln_gain, final_norm_gain, *, qb, tm, tiles):
    assert qb == KEY_BLOCK and tm % KEY_BLOCK == 0
    b, s, d = x.shape
    wb = w_in[0].astype(jnp.bfloat16)
    wo = w_out[0].astype(jnp.bfloat16)
    gain = norm_gain[0][None, :]
    x2d = x.reshape(b * s, d)

    px = _project(x2d, gain, wb, *_rope_tables(N_META, s), tm, False)
    sbk_m, sbvt_m, dk_m, dvt_m = _project(meta_tokens, gain, wb, *_rope_tables(0, N_META),
                                          N_META, True)
    sbq, sbk, sbvt, sbg, dq, dk, dvt, dg = px
    rows3 = lambda a: a.reshape(b, s, GROUP)
    blocks = lambda a: a.reshape(b, s // KEY_BLOCK, GROUP, KEY_BLOCK)

    kb = np.arange(KEY_BLOCK)
    later = (kb[None, :] > kb[:, None]).astype(np.float32)
    tri = np.concatenate([later, np.ones((SUBLANES_BF16_V7X, KEY_BLOCK), np.float32)], axis=0)
    trim = np.zeros((N_META, LANES_V7X), np.float32)
    trim[:, :N_META] = later[:N_META, :N_META]
    tri, trim = jnp.asarray(tri, jnp.bfloat16), jnp.asarray(trim, jnp.bfloat16)

    return _attention(
        (rows3(sbq), rows3(sbk), blocks(sbvt), sbk_m, sbvt_m, rows3(sbg), tri, trim),
        (rows3(dq), rows3(dk), blocks(dvt), dk_m, dvt_m, rows3(dg),
         lambda_q1, lambda_k1, lambda_q2, lambda_k2, subln_gain),
        wo, x, final_norm_gain[None, :], qb, tiles)


def kernel(x, meta_tokens, norm_gain, w_in, w_out, lambda_q1, lambda_k1, lambda_q2, lambda_k2,
           subln_gain, final_norm_gain):
    return _forward(x, meta_tokens, norm_gain, w_in, w_out, lambda_q1, lambda_k1, lambda_q2,
                    lambda_k2, subln_gain, final_norm_gain, qb=KEY_BLOCK, tm=512, tiles=4)
```

```python
import functools

import numpy as np
import jax
import jax.numpy as jnp
from jax import lax
from jax.experimental import pallas as pl
from jax.experimental.pallas import tpu as pltpu

LANES_V7X = 128
SUBLANES_BF16_V7X = 16
VMEM_LIMIT_BYTES_V7X = 56 * 1024 * 1024

N_META = 16
CHUNK = 64
ROPE_THETA = 10000.0
RMS_EPS = 1e-6
HEAD_DIM = 64
GROUP = 512
KEY_BLOCK = 256
NEG_BIG = -1e30
LOG2_E = 1.4426950408889634
LOOKAHEAD = 1
EXP_UNDERFLOW = -104.0
LAMBDA_INIT = 0.8 - 0.6 * float(np.exp(-0.3 * 0))


def _rms(x, g):
    return x * lax.rsqrt(jnp.mean(x * x, axis=-1, keepdims=True) + RMS_EPS) * g


def _proj_kernel(x_ref, g_ref, w_ref, cos_ref, sin_ref, *out_refs, meta):
    u = _rms(x_ref[...], g_ref[...]).astype(jnp.bfloat16)

    def col(c):
        return jnp.dot(u, w_ref[:, c * GROUP:(c + 1) * GROUP], preferred_element_type=jnp.float32)

    def col_t(c, rows):
        v = col(c)
        if rows > v.shape[0]:
            v = _pad_rows(v, rows)
        return v.astype(jnp.bfloat16).T

    def silu(g):
        return g / (1.0 + jnp.exp(-g))

    def rope(t):
        half = HEAD_DIM // 2
        lane = lax.broadcasted_iota(jnp.int32, (1, LANES_V7X), 1)
        first_half = (lane // half) % 2 == 0
        cos = jnp.concatenate([cos_ref[...]] * (GROUP // LANES_V7X), axis=1)
        sin = jnp.concatenate([sin_ref[...]] * (GROUP // LANES_V7X), axis=1)
        parts = []
        for j in range(GROUP // LANES_V7X):
            tile = t[:, j * LANES_V7X:(j + 1) * LANES_V7X]
            parts.append(jnp.where(first_half, pltpu.roll(tile, LANES_V7X - half, 1),
                                   pltpu.roll(tile, half, 1)))
        return t * cos + jnp.concatenate(parts, axis=1) * sin

    scale = HEAD_DIM ** -0.5
    if meta:
        sbk_ref, sbvt_ref, dk_ref, dvt_ref = out_refs
        sbk_ref[...] = col(1).astype(jnp.bfloat16)
        sbvt_ref[...] = col_t(2, LANES_V7X)
        dk_ref[...] = rope(col(5)).astype(jnp.bfloat16)
        dvt_ref[...] = col_t(6, LANES_V7X)
        return
    sbq_ref, sbk_ref, sbvt_ref, sbg_ref, dq_ref, dk_ref, dvt_ref, dg_ref = out_refs
    sbq_ref[...] = (col(0) * scale).astype(jnp.bfloat16)
    sbk_ref[...] = col(1).astype(jnp.bfloat16)
    sbg_ref[...] = silu(col(3))
    dq_ref[...] = (rope(col(4)) * scale).astype(jnp.bfloat16)
    dk_ref[...] = rope(col(5)).astype(jnp.bfloat16)
    dg_ref[...] = silu(col(7))
    for c, ref in ((2, sbvt_ref), (6, dvt_ref)):
        vt = col_t(c, 0)
        for j in range(ref.shape[0]):
            ref[j] = vt[:, j * KEY_BLOCK:(j + 1) * KEY_BLOCK]


def _project(x2d, gain, wb, cos, sin, tm, meta):
    rows, d = x2d.shape
    n_pos_tiles = cos.shape[0] // tm
    row_spec = lambda width: pl.BlockSpec((tm, width), lambda i: (i, 0))
    tab_spec = pl.BlockSpec((tm, LANES_V7X), lambda i: (i % n_pos_tiles, 0))
    bf, f32 = jnp.bfloat16, jnp.float32
    row_out = lambda dt: (row_spec(GROUP), jax.ShapeDtypeStruct((rows, GROUP), dt))
    if meta:
        vt_out = (pl.BlockSpec((GROUP, LANES_V7X), lambda i: (0, 0)),
                  jax.ShapeDtypeStruct((GROUP, LANES_V7X), bf))
        outs = [row_out(bf), vt_out, row_out(bf), vt_out]
    else:
        vt_out = (pl.BlockSpec((tm // KEY_BLOCK, GROUP, KEY_BLOCK), lambda i: (i, 0, 0)),
                  jax.ShapeDtypeStruct((rows // KEY_BLOCK, GROUP, KEY_BLOCK), bf))
        outs = [row_out(bf), row_out(bf), vt_out, row_out(f32),
                row_out(bf), row_out(bf), vt_out, row_out(f32)]
    return pl.pallas_call(
        functools.partial(_proj_kernel, meta=meta),
        grid=(rows // tm,),
        in_specs=[row_spec(d),
                  pl.BlockSpec((1, d), lambda i: (0, 0)),
                  pl.BlockSpec(wb.shape, lambda i: (0, 0), pipeline_mode=pl.Buffered(1)),
                  tab_spec, tab_spec],
        out_specs=[o[0] for o in outs],
        out_shape=[o[1] for o in outs],
        compiler_params=pltpu.CompilerParams(
            dimension_semantics=("parallel",), vmem_limit_bytes=VMEM_LIMIT_BYTES_V7X),
        name="proj_meta" if meta else "proj",
    )(x2d, gain, wb, cos, sin)


_NT = (((1,), (1,)), ((), ()))


def _masked_queries(q_ref, p, keep):
    q = q_ref[0, :, p * LANES_V7X:(p + 1) * LANES_V7X].astype(jnp.float32)
    return jnp.where(keep, q, 0.0).astype(jnp.bfloat16)


def _pad_rows(a, rows):
    return jnp.concatenate([a, jnp.zeros((rows - a.shape[0], a.shape[1]), a.dtype)], axis=0)


class _Phases:
    pass


def _sb_program(q_ref, k_ref, vt_ref, km_ref, vmt_ref, g_ref, tri_ref, trim_ref, o_ref,
                acc_ref, car_ref, z_ref, *, qb, tiles):
    i = pl.program_id(2)
    lane = lax.broadcasted_iota(jnp.int32, (1, LANES_V7X), 1)
    key = lax.broadcasted_iota(jnp.int32, (KEY_BLOCK, qb), 0)
    qry = lax.broadcasted_iota(jnp.int32, (KEY_BLOCK, qb), 1)
    strict = key < qry
    chains = [(p, hh) for p in range(tiles) for hh in range(2)]
    qms = [_masked_queries(q_ref, p, (lane // HEAD_DIM) == hh) for p, hh in chains]

    def log_sigmoids(z, mask):
        log_beta = jnp.minimum(z, 0.0) - jnp.log(1.0 + jnp.exp2(jnp.abs(z) * (-LOG2_E)))
        log_keep = log_beta - z
        if mask is not None:
            log_keep = jnp.where(mask, log_keep, 0.0)
        return log_beta, log_keep.astype(jnp.bfloat16)

    def scores(n, p, start):
        kblk = k_ref[0, pl.ds(start, KEY_BLOCK), p * LANES_V7X:(p + 1) * LANES_V7X]
        z_ref[n] = lax.dot_general(kblk, qms[n], _NT, preferred_element_type=jnp.float32)

    half = KEY_BLOCK // 2
    strict_half = strict[:half, :half]

    def split(a, diagonal):
        return [a[:half, :half], a[:half, half:], a[half:, half:]] if diagonal else [a]

    def join(parts, diagonal):
        if not diagonal:
            return parts[0]
        older = jnp.concatenate(parts[:2], axis=1)
        newer = jnp.concatenate([jnp.zeros_like(parts[2]), parts[2]], axis=1)
        return jnp.concatenate([older, newer], axis=0)

    def stages(this_start, next_start, vtblk, diagonal, exists=None):
        masks = [strict_half, None, strict_half] if diagonal else [None]
        log_betas, sums = {}, {}

        def stage1(n, p):
            if this_start is not None:
                scores(n, p, this_start)
            parts = [log_sigmoids(z, m) for z, m in zip(split(z_ref[n], diagonal), masks)]
            log_betas[n] = [lb for lb, _ in parts]
            log_keep = join([lk for _, lk in parts], diagonal)
            sums[n] = jnp.dot(tri_ref[...], log_keep, preferred_element_type=jnp.float32)
            if next_start is not None:
                scores(n, p, next_start)

        def stage2(n, p, hh):
            ws = []
            for lb, st, m in zip(log_betas.pop(n), split(sums[n][:KEY_BLOCK], diagonal), masks):
                w = jnp.exp(lb + st)
                ws.append((w if m is None else jnp.where(m, w, 0.0)).astype(jnp.bfloat16))
            tot = sums.pop(n)[KEY_BLOCK:KEY_BLOCK + 1]
            pv = jnp.dot(vtblk(p, hh), join(ws, diagonal), preferred_element_type=jnp.float32)
            if diagonal:
                acc_ref[n] = pv
                car_ref[n] = tot
            else:
                factor = jnp.exp(car_ref[n])
                if exists is not None:
                    factor, tot = factor * exists, tot * exists
                acc_ref[n] += pv * factor
                car_ref[n] += tot

        return ([functools.partial(stage1, n, p) for n, (p, hh) in enumerate(chains)],
                [functools.partial(stage2, n, p, hh) for n, (p, hh) in enumerate(chains)])

    def vt_at(j):
        return lambda p, hh: vt_ref[0, j, pl.ds(p * LANES_V7X + hh * HEAD_DIM, HEAD_DIM), :]

    def block_start(j):
        return pl.multiple_of(jnp.maximum(j, 0) * KEY_BLOCK, KEY_BLOCK)

    def prologue():
        for n, (p, hh) in enumerate(chains):
            scores(n, p, block_start(i))

    def older_blocks():
        def largest_carry():
            return jnp.max(functools.reduce(jnp.maximum,
                                            [car_ref[n] for n in range(len(chains))]))

        def live(state):
            t, carry_max = state
            return jnp.logical_and(t < i, carry_max > EXP_UNDERFLOW)

        def body(state):
            t, _ = state
            j = i - 1 - t
            first, second = stages(block_start(j), None, vt_at(j), False)
            for f in first + second:
                f()
            return t + 1, largest_carry()

        _, carry_max = lax.while_loop(live, body, (jnp.int32(1), largest_carry()))

        @pl.when(carry_max > EXP_UNDERFLOW)
        def _meta():
            zs = [lax.dot_general(km_ref[:, p * LANES_V7X:(p + 1) * LANES_V7X], qms[n], _NT,
                                  preferred_element_type=jnp.float32)
                  for n, (p, hh) in enumerate(chains)]
            log_betas, sticks = [], []
            for z in zs:
                log_beta, log_keep = log_sigmoids(z, None)
                sticks.append(jnp.dot(trim_ref[...], _pad_rows(log_keep, LANES_V7X),
                                      preferred_element_type=jnp.float32))
                log_betas.append(log_beta)
            ws = [_pad_rows(jnp.exp(log_betas[n] + sticks[n] + car_ref[n]).astype(jnp.bfloat16),
                            LANES_V7X) for n in range(len(chains))]
            for n, (p, hh) in enumerate(chains):
                acc_ref[n] += jnp.dot(vmt_ref[pl.ds(p * LANES_V7X + hh * HEAD_DIM, HEAD_DIM), :],
                                      ws[n], preferred_element_type=jnp.float32)

    def finalize():
        for p in range(tiles):
            o = jnp.concatenate([acc_ref[2 * p], acc_ref[2 * p + 1]], axis=0).T
            o_ref[0, :, p * LANES_V7X:(p + 1) * LANES_V7X] = (
                o * g_ref[0, :, p * LANES_V7X:(p + 1) * LANES_V7X]).astype(o_ref.dtype)

    ph = _Phases()
    ph.prologue = prologue
    ph.diagonal = stages(None, block_start(i - 1), vt_at(i), True)
    ph.older = stages(None, None, vt_at(jnp.maximum(i - 1, 0)), False,
                      exists=jnp.where(i > 0, 1.0, 0.0).astype(jnp.float32))
    ph.older_blocks = older_blocks
    ph.finalize = finalize
    return ph


def _diff_program(q_ref, k_ref, vt_ref, km_ref, vmt_ref, g_ref,
                  lq1_ref, lk1_ref, lq2_ref, lk2_ref, sub_ref, o_ref, acc_ref, m_ref, l_ref, z_ref,
                  *, qb, tiles):
    i = pl.program_id(2)
    lane = lax.broadcasted_iota(jnp.int32, (1, LANES_V7X), 1)
    key = lax.broadcasted_iota(jnp.int32, (KEY_BLOCK, qb), 0)
    qry = lax.broadcasted_iota(jnp.int32, (KEY_BLOCK, qb), 1)
    chunk_ok = (qry // CHUNK) >= (key // CHUNK)
    chains = [(p, c) for p in range(tiles) for c in range(2)]
    qms = [_masked_queries(q_ref, p, (lane // HEAD_DIM) == c) for p, c in chains]
    ones = jnp.ones((SUBLANES_BF16_V7X, KEY_BLOCK), jnp.bfloat16)

    def update(n, s, vt_rows, pad_to, first):
        m_blk = jnp.max(s, axis=0, keepdims=True)
        m_new = m_blk if first else jnp.maximum(m_ref[n], m_blk)
        pr = jnp.exp(s - m_new).astype(jnp.bfloat16)
        if pad_to is not None:
            pr = _pad_rows(pr, pad_to)
        pv = jnp.dot(vt_rows, pr, preferred_element_type=jnp.float32)
        num, den = pv[:LANES_V7X], pv[LANES_V7X:LANES_V7X + 1]
        if first:
            l_ref[n] = den
            acc_ref[n] = num
        else:
            alpha = jnp.exp(m_ref[n] - m_new)
            l_ref[n] = l_ref[n] * alpha + den
            acc_ref[n] = acc_ref[n] * alpha + num
        m_ref[n] = m_new

    def scores(n, p, start):
        kblk = k_ref[0, pl.ds(start, KEY_BLOCK), p * LANES_V7X:(p + 1) * LANES_V7X]
        z_ref[n] = lax.dot_general(kblk, qms[n], _NT, preferred_element_type=jnp.float32)

    def block_start(j):
        return pl.multiple_of(j * KEY_BLOCK, KEY_BLOCK)

    def block(j, next_start, mask, first):
        ahead = {}

        def next_scores(n):
            p = chains[n][0]
            kblk = k_ref[0, pl.ds(next_start, KEY_BLOCK), p * LANES_V7X:(p + 1) * LANES_V7X]
            ahead[n] = lax.dot_general(kblk, qms[n], _NT, preferred_element_type=jnp.float32)

        def chain(n, p):
            if next_start is not None:
                for m in (range(LOOKAHEAD) if n == 0 else ()):
                    next_scores(m)
                if n + LOOKAHEAD < len(chains):
                    next_scores(n + LOOKAHEAD)
            s = z_ref[n] if mask is None else jnp.where(mask, z_ref[n], NEG_BIG)
            vt_rows = jnp.concatenate([vt_ref[0, j, p * LANES_V7X:(p + 1) * LANES_V7X, :], ones],
                                      axis=0)
            update(n, s, vt_rows, None, first)
            if next_start is not None:
                z_ref[n] = ahead.pop(n)

        return [functools.partial(chain, n, p) for n, (p, c) in enumerate(chains)]

    meta_scores = {}

    def diagonal_scores(n, p):
        scores(n, p, pl.multiple_of(i * qb, qb))

    def score_meta():
        for n, (p, c) in enumerate(chains):
            meta_scores[n] = lax.dot_general(km_ref[:, p * LANES_V7X:(p + 1) * LANES_V7X], qms[n],
                                             _NT, preferred_element_type=jnp.float32)

    def meta(n, p):
        vt_rows = jnp.concatenate([vmt_ref[p * LANES_V7X:(p + 1) * LANES_V7X, :],
                                   ones[:, :LANES_V7X]], axis=0)
        update(n, meta_scores.pop(n), vt_rows, LANES_V7X, False)

    def older_blocks():
        def body(t, c):
            for f in block(t, block_start(t + 1), None, False):
                f()
            return c

        lax.fori_loop(0, i - 1, body, 0)

        @pl.when(i > 0)
        def _last():
            for f in block(i - 1, None, None, False):
                f()

    def finalize():
        lam = (jnp.exp(jnp.sum(lq1_ref[...] * lk1_ref[...], axis=-1, keepdims=True))
               - jnp.exp(jnp.sum(lq2_ref[...] * lk2_ref[...], axis=-1, keepdims=True))
               + LAMBDA_INIT)
        for p in range(tiles):
            ot = (acc_ref[2 * p] * (1.0 / l_ref[2 * p])
                  - acc_ref[2 * p + 1] * (lam / l_ref[2 * p + 1]))
            ot = ot * lax.rsqrt(jnp.mean(ot * ot, axis=0, keepdims=True) + RMS_EPS)
            o = ot.T * (sub_ref[...] * (1.0 - LAMBDA_INIT))
            o_ref[0, :, p * LANES_V7X:(p + 1) * LANES_V7X] = (
                o * g_ref[0, :, p * LANES_V7X:(p + 1) * LANES_V7X]).astype(o_ref.dtype)

    ph = _Phases()
    ph.diagonal_scores = [functools.partial(diagonal_scores, n, p)
                          for n, (p, c) in enumerate(chains)]
    ph.score_meta = score_meta
    ph.diagonal = block(i, block_start(0), chunk_ok, True)
    ph.meta = [functools.partial(meta, n, p) for n, (p, c) in enumerate(chains)]
    ph.older_blocks = older_blocks
    ph.finalize = finalize
    return ph


N_SB_IN, N_DF_IN = 8, 11


def _attn_kernel(*refs, qb, tiles):
    sb_in, df_in = refs[:N_SB_IN], refs[N_SB_IN:N_SB_IN + N_DF_IN]
    wo_ref, x_ref, fg_ref, y_ref = refs[N_SB_IN + N_DF_IN:N_SB_IN + N_DF_IN + 4]
    scratch = refs[N_SB_IN + N_DF_IN + 4:]
    mix_sb, mix_df = scratch[7:]
    sb = _sb_program(*sb_in, mix_sb, *scratch[:3], qb=qb, tiles=tiles)
    df = _diff_program(*df_in, mix_df, *scratch[3:7], qb=qb, tiles=tiles)
    sb.prologue()
    for first, other in zip(sb.diagonal[0], df.diagonal_scores):
        first()
        other()
    df.score_meta()
    for second, other in zip(sb.diagonal[1], df.diagonal):
        second()
        other()
    for first, other in zip(sb.older[0], df.meta):
        first()
        other()
    for second in sb.older[1]:
        second()
    df.older_blocks()
    sb.older_blocks()
    df.finalize()
    sb.finalize()
    half = mix_sb.shape[-1]
    h = (x_ref[0]
         + jnp.dot(mix_sb[0], wo_ref[:half, :], preferred_element_type=jnp.float32)
         + jnp.dot(mix_df[0], wo_ref[half:, :], preferred_element_type=jnp.float32))
    y_ref[0] = _rms(h, fg_ref[...])


def _attention(sb_args, df_args, wo, x, final_gain, qb, tiles):
    q = sb_args[0]
    b, s, width = q.shape
    tw = tiles * LANES_V7X
    assert tw == width, "the fused kernel handles every head of a batch element per grid step"
    qspec = pl.BlockSpec((1, qb, tw), lambda bi, p, i: (bi, i, p))
    kspec = pl.BlockSpec((1, s, tw), lambda bi, p, i: (bi, 0, p))
    vtspec = pl.BlockSpec((1, s // KEY_BLOCK, tw, KEY_BLOCK), lambda bi, p, i: (bi, 0, p, 0))
    kmspec = pl.BlockSpec((N_META, tw), lambda bi, p, i: (0, p))
    vmtspec = pl.BlockSpec((tw, LANES_V7X), lambda bi, p, i: (p, 0))
    full = lambda a: pl.BlockSpec(a.shape, lambda bi, p, i: (0,) * a.ndim)
    common = [qspec, kspec, vtspec, kmspec, vmtspec, qspec]
    xspec = pl.BlockSpec((1, qb, x.shape[-1]), lambda bi, p, i: (bi, i, 0))
    in_specs = (common + [full(a) for a in sb_args[6:]] + common + [full(a) for a in df_args[6:]]
                + [full(wo), xspec, full(final_gain)])
    assert len(sb_args) == N_SB_IN and len(df_args) == N_DF_IN
    stat = pltpu.VMEM((2 * tiles, 1, qb), jnp.float32)
    score = pltpu.VMEM((2 * tiles, KEY_BLOCK, qb), jnp.float32)
    mix = pltpu.VMEM((1, qb, width), jnp.bfloat16)
    return pl.pallas_call(
        functools.partial(_attn_kernel, qb=qb, tiles=tiles),
        grid=(b, width // tw, s // qb),
        in_specs=in_specs,
        out_specs=xspec,
        out_shape=jax.ShapeDtypeStruct(x.shape, jnp.float32),
        scratch_shapes=[pltpu.VMEM((2 * tiles, HEAD_DIM, qb), jnp.float32), stat, score,
                        pltpu.VMEM((2 * tiles, LANES_V7X, qb), jnp.float32), stat, stat, score,
                        mix, mix],
        compiler_params=pltpu.CompilerParams(
            dimension_semantics=("parallel", "parallel", "arbitrary"),
            vmem_limit_bytes=VMEM_LIMIT_BYTES_V7X),
        name="attn",
    )(*sb_args, *df_args, wo, x, final_gain)


def _rope_tables(first, length):
    f32 = np.float32
    inv = f32(1.0) / (f32(ROPE_THETA) ** (np.arange(0, HEAD_DIM, 2, dtype=f32) / f32(HEAD_DIM)))
    ang = np.arange(first, first + length, dtype=f32)[:, None] * inv[None, :]
    cos, sin = np.cos(ang).astype(f32), np.sin(ang).astype(f32)
    return (jnp.asarray(np.concatenate([cos, cos, cos, cos], axis=1)),
            jnp.asarray(np.concatenate([-sin, sin, -sin, sin], axis=1)))


def _forward(x, meta_tokens, norm_gain, w_in, w_out, lambda_q1, lambda_k1, lambda_q2, lambda_k2,
             subln_gain, final_norm_gain, *, qb, tm, tiles):
    assert qb == KEY_BLOCK and tm % KEY_BLOCK == 0
    b, s, d = x.shape
    wb = w_in[0].astype(jnp.bfloat16)
    wo = w_out[0].astype(jnp.bfloat16)
    gain = norm_gain[0][None, :]
    x2d = x.reshape(b * s, d)

    px = _project(x2d, gain, wb, *_rope_tables(N_META, s), tm, False)
    sbk_m, sbvt_m, dk_m, dvt_m = _project(meta_tokens, gain, wb, *_rope_tables(0, N_META),
                                          N_META, True)
    sbq, sbk, sbvt, sbg, dq, dk, dvt, dg = px
    rows3 = lambda a: a.reshape(b, s, GROUP)
    blocks = lambda a: a.reshape(b, s // KEY_BLOCK, GROUP, KEY_BLOCK)

    kb = np.arange(KEY_BLOCK)
    later = (kb[None, :] > kb[:, None]).astype(np.float32)
    tri = np.concatenate([later, np.ones((SUBLANES_BF16_V7X, KEY_BLOCK), np.float32)], axis=0)
    trim = np.zeros((N_META, LANES_V7X), np.float32)
    trim[:, :N_META] = later[:N_META, :N_META]
    tri, trim = jnp.asarray(tri, jnp.bfloat16), jnp.asarray(trim, jnp.bfloat16)

    return _attention(
        (rows3(sbq), rows3(sbk), blocks(sbvt), sbk_m, sbvt_m, rows3(sbg), tri, trim),
        (rows3(dq), rows3(dk), blocks(dvt), dk_m, dvt_m, rows3(dg),
         lambda_q1, lambda_k1, lambda_q2, lambda_k2, subln_gain),
        wo, x, final_norm_gain[None, :], qb, tiles)


def kernel(x, meta_tokens, norm_gain, w_in, w_out, lambda_q1, lambda_k1, lambda_q2, lambda_k2,
           subln_gain, final_norm_gain):
    return _forward(x, meta_tokens, norm_gain, w_in, w_out, lambda_q1, lambda_k1, lambda_q2,
                    lambda_k2, subln_gain, final_norm_gain, qb=KEY_BLOCK, tm=1024, tiles=4)
```

```python
import functools

import numpy as np
import jax
import jax.numpy as jnp
from jax import lax
from jax.experimental import pallas as pl
from jax.experimental.pallas import tpu as pltpu

LANES_V7X = 128
SUBLANES_BF16_V7X = 16
VMEM_LIMIT_BYTES_V7X = 56 * 1024 * 1024

N_META = 16
CHUNK = 64
ROPE_THETA = 10000.0
RMS_EPS = 1e-6
HEAD_DIM = 64
GROUP = 512
KEY_BLOCK = 256
NEG_BIG = -1e30
LOG2_E = 1.4426950408889634
LOOKAHEAD = 1
EXP_UNDERFLOW = -104.0
LAMBDA_INIT = 0.8 - 0.6 * float(np.exp(-0.3 * 0))


def _rms(x, g):
    return x * lax.rsqrt(jnp.mean(x * x, axis=-1, keepdims=True) + RMS_EPS) * g


def _proj_kernel(x_ref, g_ref, w_ref, cos_ref, sin_ref, *out_refs, meta):
    u = _rms(x_ref[...], g_ref[...]).astype(jnp.bfloat16)

    def col(c):
        return jnp.dot(u, w_ref[:, c * GROUP:(c + 1) * GROUP], preferred_element_type=jnp.float32)

    def col_t(c, rows):
        v = col(c)
        if rows > v.shape[0]:
            v = _pad_rows(v, rows)
        return v.astype(jnp.bfloat16).T

    def silu(g):
        return g / (1.0 + jnp.exp(-g))

    def rope(t):
        half = HEAD_DIM // 2
        lane = lax.broadcasted_iota(jnp.int32, (1, LANES_V7X), 1)
        first_half = (lane // half) % 2 == 0
        cos = jnp.concatenate([cos_ref[...]] * (GROUP // LANES_V7X), axis=1)
        sin = jnp.concatenate([sin_ref[...]] * (GROUP // LANES_V7X), axis=1)
        parts = []
        for j in range(GROUP // LANES_V7X):
            tile = t[:, j * LANES_V7X:(j + 1) * LANES_V7X]
            parts.append(jnp.where(first_half, pltpu.roll(tile, LANES_V7X - half, 1),
                                   pltpu.roll(tile, half, 1)))
        return t * cos + jnp.concatenate(parts, axis=1) * sin

    scale = HEAD_DIM ** -0.5
    if meta:
        sbk_ref, sbvt_ref, dk_ref, dvt_ref = out_refs
        sbk_ref[...] = col(1).astype(jnp.bfloat16)
        sbvt_ref[...] = col_t(2, LANES_V7X)
        dk_ref[...] = rope(col(5)).astype(jnp.bfloat16)
        dvt_ref[...] = col_t(6, LANES_V7X)
        return
    sbq_ref, sbk_ref, sbvt_ref, sbg_ref, dq_ref, dk_ref, dvt_ref, dg_ref = out_refs
    sbq_ref[...] = (col(0) * scale).astype(jnp.bfloat16)
    sbk_ref[...] = col(1).astype(jnp.bfloat16)
    sbg_ref[...] = silu(col(3))
    dq_ref[...] = (rope(col(4)) * scale).astype(jnp.bfloat16)
    dk_ref[...] = rope(col(5)).astype(jnp.bfloat16)
    dg_ref[...] = silu(col(7))
    for c, ref in ((2, sbvt_ref), (6, dvt_ref)):
        vt = col_t(c, 0)
        for j in range(ref.shape[0]):
            ref[j] = vt[:, j * KEY_BLOCK:(j + 1) * KEY_BLOCK]


def _project(x2d, gain, wb, cos, sin, tm, meta):
    rows, d = x2d.shape
    n_pos_tiles = cos.shape[0] // tm
    row_spec = lambda width: pl.BlockSpec((tm, width), lambda i: (i, 0))
    tab_spec = pl.BlockSpec((tm, LANES_V7X), lambda i: (i % n_pos_tiles, 0))
    bf, f32 = jnp.bfloat16, jnp.float32
    row_out = lambda dt: (row_spec(GROUP), jax.ShapeDtypeStruct((rows, GROUP), dt))
    if meta:
        vt_out = (pl.BlockSpec((GROUP, LANES_V7X), lambda i: (0, 0)),
                  jax.ShapeDtypeStruct((GROUP, LANES_V7X), bf))
        outs = [row_out(bf), vt_out, row_out(bf), vt_out]
    else:
        vt_out = (pl.BlockSpec((tm // KEY_BLOCK, GROUP, KEY_BLOCK), lambda i: (i, 0, 0)),
                  jax.ShapeDtypeStruct((rows // KEY_BLOCK, GROUP, KEY_BLOCK), bf))
        outs = [row_out(bf), row_out(bf), vt_out, row_out(f32),
                row_out(bf), row_out(bf), vt_out, row_out(f32)]
    return pl.pallas_call(
        functools.partial(_proj_kernel, meta=meta),
        grid=(rows // tm,),
        in_specs=[row_spec(d),
                  pl.BlockSpec((1, d), lambda i: (0, 0)),
                  pl.BlockSpec(wb.shape, lambda i: (0, 0), pipeline_mode=pl.Buffered(1)),
                  tab_spec, tab_spec],
        out_specs=[o[0] for o in outs],
        out_shape=[o[1] for o in outs],
        compiler_params=pltpu.CompilerParams(
            dimension_semantics=("parallel",), vmem_limit_bytes=VMEM_LIMIT_BYTES_V7X),
        name="proj_meta" if meta else "proj",
    )(x2d, gain, wb, cos, sin)


_NT = (((1,), (1,)), ((), ()))


def _masked_queries(q_ref, p, keep):
    q = q_ref[0, :, p * LANES_V7X:(p + 1) * LANES_V7X].astype(jnp.float32)
    return jnp.where(keep, q, 0.0).astype(jnp.bfloat16)


def _pad_rows(a, rows):
    return jnp.concatenate([a, jnp.zeros((rows - a.shape[0], a.shape[1]), a.dtype)], axis=0)


class _Phases:
    pass


def _sb_program(q_ref, k_ref, vt_ref, km_ref, vmt_ref, g_ref, tri_ref, trim_ref, o_ref,
                acc_ref, car_ref, z_ref, *, qb, tiles):
    i = pl.program_id(2)
    lane = lax.broadcasted_iota(jnp.int32, (1, LANES_V7X), 1)
    key = lax.broadcasted_iota(jnp.int32, (KEY_BLOCK, qb), 0)
    qry = lax.broadcasted_iota(jnp.int32, (KEY_BLOCK, qb), 1)
    strict = key < qry
    chains = [(p, hh) for p in range(tiles) for hh in range(2)]
    qms = [_masked_queries(q_ref, p, (lane // HEAD_DIM) == hh) for p, hh in chains]

    def log_sigmoids(z, mask):
        log_beta = jnp.minimum(z, 0.0) - jnp.log(1.0 + jnp.exp2(jnp.abs(z) * (-LOG2_E)))
        log_keep = log_beta - z
        if mask is not None:
            log_keep = jnp.where(mask, log_keep, 0.0)
        return log_beta, log_keep.astype(jnp.bfloat16)

    def scores(n, p, start):
        kblk = k_ref[0, pl.ds(start, KEY_BLOCK), p * LANES_V7X:(p + 1) * LANES_V7X]
        z_ref[n] = lax.dot_general(kblk, qms[n], _NT, preferred_element_type=jnp.float32)

    half = KEY_BLOCK // 2
    strict_half = strict[:half, :half]

    def split(a, diagonal):
        return [a[:half, :half], a[:half, half:], a[half:, half:]] if diagonal else [a]

    def join(parts, diagonal):
        if not diagonal:
            return parts[0]
        older = jnp.concatenate(parts[:2], axis=1)
        newer = jnp.concatenate([jnp.zeros_like(parts[2]), parts[2]], axis=1)
        return jnp.concatenate([older, newer], axis=0)

    def stages(this_start, next_start, vtblk, diagonal, exists=None):
        masks = [strict_half, None, strict_half] if diagonal else [None]
        log_betas, sums = {}, {}

        def stage1(n, p):
            if this_start is not None:
                scores(n, p, this_start)
            parts = [log_sigmoids(z, m) for z, m in zip(split(z_ref[n], diagonal), masks)]
            log_betas[n] = [lb for lb, _ in parts]
            log_keep = join([lk for _, lk in parts], diagonal)
            sums[n] = jnp.dot(tri_ref[...], log_keep, preferred_element_type=jnp.float32)
            if next_start is not None:
                scores(n, p, next_start)

        def stage2(n, p, hh):
            ws = []
            for lb, st, m in zip(log_betas.pop(n), split(sums[n][:KEY_BLOCK], diagonal), masks):
                w = jnp.exp(lb + st)
                ws.append((w if m is None else jnp.where(m, w, 0.0)).astype(jnp.bfloat16))
            tot = sums.pop(n)[KEY_BLOCK:KEY_BLOCK + 1]
            pv = jnp.dot(vtblk(p, hh), join(ws, diagonal), preferred_element_type=jnp.float32)
            if diagonal:
                acc_ref[n] = pv
                car_ref[n] = tot
            else:
                factor = jnp.exp(car_ref[n])
                if exists is not None:
                    factor, tot = factor * exists, tot * exists
                acc_ref[n] += pv * factor
                car_ref[n] += tot

        return ([functools.partial(stage1, n, p) for n, (p, hh) in enumerate(chains)],
                [functools.partial(stage2, n, p, hh) for n, (p, hh) in enumerate(chains)])

    def vt_at(j):
        return lambda p, hh: vt_ref[0, j, pl.ds(p * LANES_V7X + hh * HEAD_DIM, HEAD_DIM), :]

    def block_start(j):
        return pl.multiple_of(jnp.maximum(j, 0) * KEY_BLOCK, KEY_BLOCK)

    def prologue():
        for n, (p, hh) in enumerate(chains):
            scores(n, p, block_start(i))

    def older_blocks():
        def largest_carry():
            return jnp.max(functools.reduce(jnp.maximum,
                                            [car_ref[n] for n in range(len(chains))]))

        def live(state):
            t, carry_max = state
            return jnp.logical_and(t < i, carry_max > EXP_UNDERFLOW)

        def body(state):
            t, _ = state
            j = i - 1 - t
            first, second = stages(block_start(j), None, vt_at(j), False)
            for f in first + second:
                f()
            return t + 1, largest_carry()

        _, carry_max = lax.while_loop(live, body, (jnp.int32(1), largest_carry()))

        @pl.when(carry_max > EXP_UNDERFLOW)
        def _meta():
            zs = [lax.dot_general(km_ref[:, p * LANES_V7X:(p + 1) * LANES_V7X], qms[n], _NT,
                                  preferred_element_type=jnp.float32)
                  for n, (p, hh) in enumerate(chains)]
            log_betas, sticks = [], []
            for z in zs:
                log_beta, log_keep = log_sigmoids(z, None)
                sticks.append(jnp.dot(trim_ref[...], _pad_rows(log_keep, LANES_V7X),
                                      preferred_element_type=jnp.float32))
                log_betas.append(log_beta)
            ws = [_pad_rows(jnp.exp(log_betas[n] + sticks[n] + car_ref[n]).astype(jnp.bfloat16),
                            LANES_V7X) for n in range(len(chains))]
            for n, (p, hh) in enumerate(chains):
                acc_ref[n] += jnp.dot(vmt_ref[pl.ds(p * LANES_V7X + hh * HEAD_DIM, HEAD_DIM), :],
                                      ws[n], preferred_element_type=jnp.float32)

    def finalize():
        for p in range(tiles):
            o = jnp.concatenate([acc_ref[2 * p], acc_ref[2 * p + 1]], axis=0).T
            o_ref[0, :, p * LANES_V7X:(p + 1) * LANES_V7X] = (
                o * g_ref[0, :, p * LANES_V7X:(p + 1) * LANES_V7X]).astype(o_ref.dtype)

    ph = _Phases()
    ph.prologue = prologue
    ph.diagonal = stages(None, block_start(i - 1), vt_at(i), True)
    ph.older = stages(None, None, vt_at(jnp.maximum(i - 1, 0)), False,
                      exists=jnp.where(i > 0, 1.0, 0.0).astype(jnp.float32))
    ph.older_blocks = older_blocks
    ph.finalize = finalize
    return ph


def _diff_program(q_ref, k_ref, vt_ref, km_ref, vmt_ref, g_ref,
                  lq1_ref, lk1_ref, lq2_ref, lk2_ref, sub_ref, o_ref, acc_ref, m_ref, l_ref, z_ref,
                  *, qb, tiles):
    i = pl.program_id(2)
    lane = lax.broadcasted_iota(jnp.int32, (1, LANES_V7X), 1)
    key = lax.broadcasted_iota(jnp.int32, (KEY_BLOCK, qb), 0)
    qry = lax.broadcasted_iota(jnp.int32, (KEY_BLOCK, qb), 1)
    chunk_ok = (qry // CHUNK) >= (key // CHUNK)
    chains = [(p, c) for p in range(tiles) for c in range(2)]
    qms = [_masked_queries(q_ref, p, (lane // HEAD_DIM) == c) for p, c in chains]
    ones = jnp.ones((SUBLANES_BF16_V7X, KEY_BLOCK), jnp.bfloat16)

    def update(n, s, vt_rows, pad_to, first):
        m_blk = jnp.max(s, axis=0, keepdims=True)
        m_new = m_blk if first else jnp.maximum(m_ref[n], m_blk)
        pr = jnp.exp(s - m_new).astype(jnp.bfloat16)
        if pad_to is not None:
            pr = _pad_rows(pr, pad_to)
        pv = jnp.dot(vt_rows, pr, preferred_element_type=jnp.float32)
        num, den = pv[:LANES_V7X], pv[LANES_V7X:LANES_V7X + 1]
        if first:
            l_ref[n] = den
            acc_ref[n] = num
        else:
            alpha = jnp.exp(m_ref[n] - m_new)
            l_ref[n] = l_ref[n] * alpha + den
            acc_ref[n] = acc_ref[n] * alpha + num
        m_ref[n] = m_new

    def scores(n, p, start):
        kblk = k_ref[0, pl.ds(start, KEY_BLOCK), p * LANES_V7X:(p + 1) * LANES_V7X]
        z_ref[n] = lax.dot_general(kblk, qms[n], _NT, preferred_element_type=jnp.float32)

    def block_start(j):
        return pl.multiple_of(j * KEY_BLOCK, KEY_BLOCK)

    def block(j, next_start, mask, first):
        ahead = {}

        def next_scores(n):
            p = chains[n][0]
            kblk = k_ref[0, pl.ds(next_start, KEY_BLOCK), p * LANES_V7X:(p + 1) * LANES_V7X]
            ahead[n] = lax.dot_general(kblk, qms[n], _NT, preferred_element_type=jnp.float32)

        def chain(n, p):
            if next_start is not None:
                for m in (range(LOOKAHEAD) if n == 0 else ()):
                    next_scores(m)
                if n + LOOKAHEAD < len(chains):
                    next_scores(n + LOOKAHEAD)
            s = z_ref[n] if mask is None else jnp.where(mask, z_ref[n], NEG_BIG)
            vt_rows = jnp.concatenate([vt_ref[0, j, p * LANES_V7X:(p + 1) * LANES_V7X, :], ones],
                                      axis=0)
            update(n, s, vt_rows, None, first)
            if next_start is not None:
                z_ref[n] = ahead.pop(n)

        return [functools.partial(chain, n, p) for n, (p, c) in enumerate(chains)]

    meta_scores = {}

    def diagonal_scores(n, p):
        scores(n, p, pl.multiple_of(i * qb, qb))

    def score_meta():
        for n, (p, c) in enumerate(chains):
            meta_scores[n] = lax.dot_general(km_ref[:, p * LANES_V7X:(p + 1) * LANES_V7X], qms[n],
                                             _NT, preferred_element_type=jnp.float32)

    def meta(n, p):
        vt_rows = jnp.concatenate([vmt_ref[p * LANES_V7X:(p + 1) * LANES_V7X, :],
                                   ones[:, :LANES_V7X]], axis=0)
        update(n, meta_scores.pop(n), vt_rows, LANES_V7X, False)

    def older_blocks():
        def body(t, c):
            for f in block(t, block_start(t + 1), None, False):
                f()
            return c

        lax.fori_loop(0, i - 1, body, 0)

        @pl.when(i > 0)
        def _last():
            for f in block(i - 1, None, None, False):
                f()

    def finalize():
        lam = (jnp.exp(jnp.sum(lq1_ref[...] * lk1_ref[...], axis=-1, keepdims=True))
               - jnp.exp(jnp.sum(lq2_ref[...] * lk2_ref[...], axis=-1, keepdims=True))
               + LAMBDA_INIT)
        for p in range(tiles):
            ot = (acc_ref[2 * p] * (1.0 / l_ref[2 * p])
                  - acc_ref[2 * p + 1] * (lam / l_ref[2 * p + 1]))
            ot = ot * lax.rsqrt(jnp.mean(ot * ot, axis=0, keepdims=True) + RMS_EPS)
            o = ot.T * (sub_ref[...] * (1.0 - LAMBDA_INIT))
            o_ref[0, :, p * LANES_V7X:(p + 1) * LANES_V7X] = (
                o * g_ref[0, :, p * LANES_V7X:(p + 1) * LANES_V7X]).astype(o_ref.dtype)

    ph = _Phases()
    ph.diagonal_scores = [functools.partial(diagonal_scores, n, p)
                          for n, (p, c) in enumerate(chains)]
    ph.score_meta = score_meta
    ph.diagonal = block(i, block_start(0), chunk_ok, True)
    ph.meta = [functools.partial(meta, n, p) for n, (p, c) in enumerate(chains)]
    ph.older_blocks = older_blocks
    ph.finalize = finalize
    return ph


N_SB_IN, N_DF_IN = 8, 11


def _attn_kernel(*refs, qb, tiles):
    sb_in, df_in = refs[:N_SB_IN], refs[N_SB_IN:N_SB_IN + N_DF_IN]
    wo_ref, x_ref, fg_ref, y_ref = refs[N_SB_IN + N_DF_IN:N_SB_IN + N_DF_IN + 4]
    scratch = refs[N_SB_IN + N_DF_IN + 4:]
    mix_sb, mix_df = scratch[7:]
    sb = _sb_program(*sb_in, mix_sb, *scratch[:3], qb=qb, tiles=tiles)
    df = _diff_program(*df_in, mix_df, *scratch[3:7], qb=qb, tiles=tiles)
    sb.prologue()
    for first in sb.diagonal[0]:
        first()
    for second, other in zip(sb.diagonal[1], df.diagonal_scores):
        second()
        other()
    df.score_meta()
    for first, other in zip(sb.older[0], df.diagonal):
        first()
        other()
    for second, other in zip(sb.older[1], df.meta):
        second()
        other()
    df.older_blocks()
    sb.older_blocks()
    df.finalize()
    sb.finalize()
    half = mix_sb.shape[-1]
    h = (x_ref[0]
         + jnp.dot(mix_sb[0], wo_ref[:half, :], preferred_element_type=jnp.float32)
         + jnp.dot(mix_df[0], wo_ref[half:, :], preferred_element_type=jnp.float32))
    y_ref[0] = _rms(h, fg_ref[...])


def _attention(sb_args, df_args, wo, x, final_gain, qb, tiles):
    q = sb_args[0]
    b, s, width = q.shape
    tw = tiles * LANES_V7X
    assert tw == width, "the fused kernel handles every head of a batch element per grid step"
    qspec = pl.BlockSpec((1, qb, tw), lambda bi, p, i: (bi, i, p))
    kspec = pl.BlockSpec((1, s, tw), lambda bi, p, i: (bi, 0, p))
    vtspec = pl.BlockSpec((1, s // KEY_BLOCK, tw, KEY_BLOCK), lambda bi, p, i: (bi, 0, p, 0))
    kmspec = pl.BlockSpec((N_META, tw), lambda bi, p, i: (0, p))
    vmtspec = pl.BlockSpec((tw, LANES_V7X), lambda bi, p, i: (p, 0))
    full = lambda a: pl.BlockSpec(a.shape, lambda bi, p, i: (0,) * a.ndim)
    common = [qspec, kspec, vtspec, kmspec, vmtspec, qspec]
    xspec = pl.BlockSpec((1, qb, x.shape[-1]), lambda bi, p, i: (bi, i, 0))
    in_specs = (common + [full(a) for a in sb_args[6:]] + common + [full(a) for a in df_args[6:]]
                + [full(wo), xspec, full(final_gain)])
    assert len(sb_args) == N_SB_IN and len(df_args) == N_DF_IN
    stat = pltpu.VMEM((2 * tiles, 1, qb), jnp.float32)
    score = pltpu.VMEM((2 * tiles, KEY_BLOCK, qb), jnp.float32)
    mix = pltpu.VMEM((1, qb, width), jnp.bfloat16)
    return pl.pallas_call(
        functools.partial(_attn_kernel, qb=qb, tiles=tiles),
        grid=(b, width // tw, s // qb),
        in_specs=in_specs,
        out_specs=xspec,
        out_shape=jax.ShapeDtypeStruct(x.shape, jnp.float32),
        scratch_shapes=[pltpu.VMEM((2 * tiles, HEAD_DIM, qb), jnp.float32), stat, score,
                        pltpu.VMEM((2 * tiles, LANES_V7X, qb), jnp.float32), stat, stat, score,
                        mix, mix],
        compiler_params=pltpu.CompilerParams(
            dimension_semantics=("parallel", "parallel", "arbitrary"),
            vmem_limit_bytes=VMEM_LIMIT_BYTES_V7X),
        name="attn",
    )(*sb_args, *df_args, wo, x, final_gain)


def _rope_tables(first, length):
    f32 = np.float32
    inv = f32(1.0) / (f32(ROPE_THETA) ** (np.arange(0, HEAD_DIM, 2, dtype=f32) / f32(HEAD_DIM)))
    ang = np.arange(first, first + length, dtype=f32)[:, None] * inv[None, :]
    cos, sin = np.cos(ang).astype(f32), np.sin(ang).astype(f32)
    return (jnp.asarray(np.concatenate([cos, cos, cos, cos], axis=1)),
            jnp.asarray(np.concatenate([-sin, sin, -sin, sin], axis=1)))


def _forward(x, meta_tokens, norm_gain, w_in, w_out, lambda_q1, lambda_k1, lambda_q2, lambda_k2,
             subln_gain, final_norm_gain, *, qb, tm, tiles):
    assert qb == KEY_BLOCK and tm % KEY_BLOCK == 0
    b, s, d = x.shape
    wb = w_in[0].astype(jnp.bfloat16)
    wo = w_out[0].astype(jnp.bfloat16)
    gain = norm_gain[0][None, :]
    x2d = x.reshape(b * s, d)

    px = _project(x2d, gain, wb, *_rope_tables(N_META, s), tm, False)
    sbk_m, sbvt_m, dk_m, dvt_m = _project(meta_tokens, gain, wb, *_rope_tables(0, N_META),
                                          N_META, True)
    sbq, sbk, sbvt, sbg, dq, dk, dvt, dg = px
    rows3 = lambda a: a.reshape(b, s, GROUP)
    blocks = lambda a: a.reshape(b, s // KEY_BLOCK, GROUP, KEY_BLOCK)

    kb = np.arange(KEY_BLOCK)
    later = (kb[None, :] > kb[:, None]).astype(np.float32)
    tri = np.concatenate([later, np.ones((SUBLANES_BF16_V7X, KEY_BLOCK), np.float32)], axis=0)
    trim = np.zeros((N_META, LANES_V7X), np.float32)
    trim[:, :N_META] = later[:N_META, :N_META]
    tri, trim = jnp.asarray(tri, jnp.bfloat16), jnp.asarray(trim, jnp.bfloat16)

    return _attention(
        (rows3(sbq), rows3(sbk), blocks(sbvt), sbk_m, sbvt_m, rows3(sbg), tri, trim),
        (rows3(dq), rows3(dk), blocks(dvt), dk_m, dvt_m, rows3(dg),
         lambda_q1, lambda_k1, lambda_q2, lambda_k2, subln_gain),
        wo, x, final_norm_gain[None, :], qb, tiles)


def kernel(x, meta_tokens, norm_gain, w_in, w_out, lambda_q1, lambda_k1, lambda_q2, lambda_k2,
           subln_gain, final_norm_gain):
    return _forward(x, meta_tokens, norm_gain, w_in, w_out, lambda_q1, lambda_k1, lambda_q2,
                    lambda_k2, subln_gain, final_norm_gain, qb=KEY_BLOCK, tm=1024, tiles=4)
```

```python
import functools

import numpy as np
import jax
import jax.numpy as jnp
from jax import lax
from jax.experimental import pallas as pl
from jax.experimental.pallas import tpu as pltpu

LANES_V7X = 128
SUBLANES_BF16_V7X = 16
VMEM_LIMIT_BYTES_V7X = 56 * 1024 * 1024

N_META = 16
CHUNK = 64
ROPE_THETA = 10000.0
RMS_EPS = 1e-6
HEAD_DIM = 64
GROUP = 512
KEY_BLOCK = 256
NEG_BIG = -1e30
LOG2_E = 1.4426950408889634
LOOKAHEAD = 1
EXP_UNDERFLOW = -104.0
LAMBDA_INIT = 0.8 - 0.6 * float(np.exp(-0.3 * 0))


def _rms(x, g):
    return x * lax.rsqrt(jnp.mean(x * x, axis=-1, keepdims=True) + RMS_EPS) * g


def _proj_kernel(x_ref, g_ref, w_ref, cos_ref, sin_ref, *out_refs, meta):
    u = _rms(x_ref[...], g_ref[...]).astype(jnp.bfloat16)

    def col(c):
        w = w_ref[:, c * GROUP:(c + 1) * GROUP].astype(jnp.bfloat16)
        return jnp.dot(u, w, preferred_element_type=jnp.float32)

    def col_t(c, rows):
        v = col(c)
        if rows > v.shape[0]:
            v = _pad_rows(v, rows)
        return v.astype(jnp.bfloat16).T

    def silu(g):
        return g / (1.0 + jnp.exp(-g))

    def rope(t):
        half = HEAD_DIM // 2
        lane = lax.broadcasted_iota(jnp.int32, (1, LANES_V7X), 1)
        first_half = (lane // half) % 2 == 0
        cos = jnp.concatenate([cos_ref[...]] * (GROUP // LANES_V7X), axis=1)
        sin = jnp.concatenate([sin_ref[...]] * (GROUP // LANES_V7X), axis=1)
        parts = []
        for j in range(GROUP // LANES_V7X):
            tile = t[:, j * LANES_V7X:(j + 1) * LANES_V7X]
            parts.append(jnp.where(first_half, pltpu.roll(tile, LANES_V7X - half, 1),
                                   pltpu.roll(tile, half, 1)))
        return t * cos + jnp.concatenate(parts, axis=1) * sin

    scale = HEAD_DIM ** -0.5
    if meta:
        sbk_ref, sbvt_ref, dk_ref, dvt_ref = out_refs
        sbk_ref[...] = col(1).astype(jnp.bfloat16)
        sbvt_ref[...] = col_t(2, LANES_V7X)
        dk_ref[...] = rope(col(5)).astype(jnp.bfloat16)
        dvt_ref[...] = col_t(6, LANES_V7X)
        return
    sbq_ref, sbk_ref, sbvt_ref, sbg_ref, dq_ref, dk_ref, dvt_ref, dg_ref = out_refs
    sbq_ref[...] = (col(0) * scale).astype(jnp.bfloat16)
    sbk_ref[...] = col(1).astype(jnp.bfloat16)
    sbg_ref[...] = silu(col(3))
    dq_ref[...] = (rope(col(4)) * scale).astype(jnp.bfloat16)
    dk_ref[...] = rope(col(5)).astype(jnp.bfloat16)
    dg_ref[...] = silu(col(7))
    for c, ref in ((2, sbvt_ref), (6, dvt_ref)):
        vt = col_t(c, 0)
        for j in range(ref.shape[0]):
            ref[j] = vt[:, j * KEY_BLOCK:(j + 1) * KEY_BLOCK]


def _project(x2d, gain, wb, cos, sin, tm, meta):
    rows, d = x2d.shape
    n_pos_tiles = cos.shape[0] // tm
    row_spec = lambda width: pl.BlockSpec((tm, width), lambda i: (i, 0))
    tab_spec = pl.BlockSpec((tm, LANES_V7X), lambda i: (i % n_pos_tiles, 0))
    bf, f32 = jnp.bfloat16, jnp.float32
    row_out = lambda dt: (row_spec(GROUP), jax.ShapeDtypeStruct((rows, GROUP), dt))
    if meta:
        vt_out = (pl.BlockSpec((GROUP, LANES_V7X), lambda i: (0, 0)),
                  jax.ShapeDtypeStruct((GROUP, LANES_V7X), bf))
        outs = [row_out(bf), vt_out, row_out(bf), vt_out]
    else:
        vt_out = (pl.BlockSpec((tm // KEY_BLOCK, GROUP, KEY_BLOCK), lambda i: (i, 0, 0)),
                  jax.ShapeDtypeStruct((rows // KEY_BLOCK, GROUP, KEY_BLOCK), bf))
        outs = [row_out(bf), row_out(bf), vt_out, row_out(f32),
                row_out(bf), row_out(bf), vt_out, row_out(f32)]
    return pl.pallas_call(
        functools.partial(_proj_kernel, meta=meta),
        grid=(rows // tm,),
        in_specs=[row_spec(d),
                  pl.BlockSpec((1, d), lambda i: (0, 0)),
                  pl.BlockSpec(wb.shape, lambda i: (0, 0), pipeline_mode=pl.Buffered(1)),
                  tab_spec, tab_spec],
        out_specs=[o[0] for o in outs],
        out_shape=[o[1] for o in outs],
        compiler_params=pltpu.CompilerParams(
            dimension_semantics=("parallel",), vmem_limit_bytes=VMEM_LIMIT_BYTES_V7X),
        name="proj_meta" if meta else "proj",
    )(x2d, gain, wb, cos, sin)


_NT = (((1,), (1,)), ((), ()))


def _masked_queries(q_ref, p, keep):
    q = q_ref[0, :, p * LANES_V7X:(p + 1) * LANES_V7X].astype(jnp.float32)
    return jnp.where(keep, q, 0.0).astype(jnp.bfloat16)


def _pad_rows(a, rows):
    return jnp.concatenate([a, jnp.zeros((rows - a.shape[0], a.shape[1]), a.dtype)], axis=0)


class _Phases:
    pass


def _sb_program(q_ref, k_ref, vt_ref, km_ref, vmt_ref, g_ref, tri_ref, trim_ref, o_ref,
                acc_ref, car_ref, z_ref, *, qb, tiles):
    i = pl.program_id(2)
    lane = lax.broadcasted_iota(jnp.int32, (1, LANES_V7X), 1)
    key = lax.broadcasted_iota(jnp.int32, (KEY_BLOCK, qb), 0)
    qry = lax.broadcasted_iota(jnp.int32, (KEY_BLOCK, qb), 1)
    strict = key < qry
    chains = [(p, hh) for p in range(tiles) for hh in range(2)]
    qms = [_masked_queries(q_ref, p, (lane // HEAD_DIM) == hh) for p, hh in chains]

    def log_sigmoids(z, mask):
        log_beta = jnp.minimum(z, 0.0) - jnp.log(1.0 + jnp.exp2(jnp.abs(z) * (-LOG2_E)))
        log_keep = log_beta - z
        if mask is not None:
            log_keep = jnp.where(mask, log_keep, 0.0)
        return log_beta, log_keep.astype(jnp.bfloat16)

    def scores(n, p, start):
        kblk = k_ref[0, pl.ds(start, KEY_BLOCK), p * LANES_V7X:(p + 1) * LANES_V7X]
        z_ref[n] = lax.dot_general(kblk, qms[n], _NT, preferred_element_type=jnp.float32)

    half = KEY_BLOCK // 2
    strict_half = strict[:half, :half]

    def split(a, diagonal):
        return [a[:half, :half], a[:half, half:], a[half:, half:]] if diagonal else [a]

    def join(parts, diagonal):
        if not diagonal:
            return parts[0]
        older = jnp.concatenate(parts[:2], axis=1)
        newer = jnp.concatenate([jnp.zeros_like(parts[2]), parts[2]], axis=1)
        return jnp.concatenate([older, newer], axis=0)

    def stages(this_start, next_start, vtblk, diagonal, exists=None):
        masks = [strict_half, None, strict_half] if diagonal else [None]
        log_betas, sums = {}, {}

        def stage1(n, p):
            if this_start is not None:
                scores(n, p, this_start)
            parts = [log_sigmoids(z, m) for z, m in zip(split(z_ref[n], diagonal), masks)]
            log_betas[n] = [lb for lb, _ in parts]
            log_keep = join([lk for _, lk in parts], diagonal)
            sums[n] = jnp.dot(tri_ref[...], log_keep, preferred_element_type=jnp.float32)
            if next_start is not None:
                scores(n, p, next_start)

        def stage2(n, p, hh):
            ws = []
            for lb, st, m in zip(log_betas.pop(n), split(sums[n][:KEY_BLOCK], diagonal), masks):
                w = jnp.exp(lb + st)
                ws.append((w if m is None else jnp.where(m, w, 0.0)).astype(jnp.bfloat16))
            tot = sums.pop(n)[KEY_BLOCK:KEY_BLOCK + 1]
            pv = jnp.dot(vtblk(p, hh), join(ws, diagonal), preferred_element_type=jnp.float32)
            if diagonal:
                acc_ref[n] = pv
                car_ref[n] = tot
            else:
                factor = jnp.exp(car_ref[n])
                if exists is not None:
                    factor, tot = factor * exists, tot * exists
                acc_ref[n] += pv * factor
                car_ref[n] += tot

        return ([functools.partial(stage1, n, p) for n, (p, hh) in enumerate(chains)],
                [functools.partial(stage2, n, p, hh) for n, (p, hh) in enumerate(chains)])

    def vt_at(j):
        return lambda p, hh: vt_ref[0, j, pl.ds(p * LANES_V7X + hh * HEAD_DIM, HEAD_DIM), :]

    def block_start(j):
        return pl.multiple_of(jnp.maximum(j, 0) * KEY_BLOCK, KEY_BLOCK)

    def prologue():
        for n, (p, hh) in enumerate(chains):
            scores(n, p, block_start(i))

    def older_blocks():
        def largest_carry():
            return jnp.max(functools.reduce(jnp.maximum,
                                            [car_ref[n] for n in range(len(chains))]))

        def live(state):
            t, carry_max = state
            return jnp.logical_and(t < i, carry_max > EXP_UNDERFLOW)

        def body(state):
            t, _ = state
            j = i - 1 - t
            first, second = stages(block_start(j), None, vt_at(j), False)
            for f in first + second:
                f()
            return t + 1, largest_carry()

        _, carry_max = lax.while_loop(live, body, (jnp.int32(1), largest_carry()))

        @pl.when(carry_max > EXP_UNDERFLOW)
        def _meta():
            zs = [lax.dot_general(km_ref[:, p * LANES_V7X:(p + 1) * LANES_V7X], qms[n], _NT,
                                  preferred_element_type=jnp.float32)
                  for n, (p, hh) in enumerate(chains)]
            log_betas, sticks = [], []
            for z in zs:
                log_beta, log_keep = log_sigmoids(z, None)
                sticks.append(jnp.dot(trim_ref[...], _pad_rows(log_keep, LANES_V7X),
                                      preferred_element_type=jnp.float32))
                log_betas.append(log_beta)
            ws = [_pad_rows(jnp.exp(log_betas[n] + sticks[n] + car_ref[n]).astype(jnp.bfloat16),
                            LANES_V7X) for n in range(len(chains))]
            for n, (p, hh) in enumerate(chains):
                acc_ref[n] += jnp.dot(vmt_ref[pl.ds(p * LANES_V7X + hh * HEAD_DIM, HEAD_DIM), :],
                                      ws[n], preferred_element_type=jnp.float32)

    def finalize():
        for p in range(tiles):
            o = jnp.concatenate([acc_ref[2 * p], acc_ref[2 * p + 1]], axis=0).T
            o_ref[0, :, p * LANES_V7X:(p + 1) * LANES_V7X] = (
                o * g_ref[0, :, p * LANES_V7X:(p + 1) * LANES_V7X]).astype(o_ref.dtype)

    ph = _Phases()
    ph.prologue = prologue
    ph.diagonal = stages(None, block_start(i - 1), vt_at(i), True)
    ph.older = stages(None, None, vt_at(jnp.maximum(i - 1, 0)), False,
                      exists=jnp.where(i > 0, 1.0, 0.0).astype(jnp.float32))
    ph.older_blocks = older_blocks
    ph.finalize = finalize
    return ph


def _diff_program(q_ref, k_ref, vt_ref, km_ref, vmt_ref, g_ref,
                  lq1_ref, lk1_ref, lq2_ref, lk2_ref, sub_ref, o_ref, acc_ref, m_ref, l_ref, z_ref,
                  *, qb, tiles):
    i = pl.program_id(2)
    lane = lax.broadcasted_iota(jnp.int32, (1, LANES_V7X), 1)
    key = lax.broadcasted_iota(jnp.int32, (KEY_BLOCK, qb), 0)
    qry = lax.broadcasted_iota(jnp.int32, (KEY_BLOCK, qb), 1)
    chunk_ok = (qry // CHUNK) >= (key // CHUNK)
    chains = [(p, c) for p in range(tiles) for c in range(2)]
    qms = [_masked_queries(q_ref, p, (lane // HEAD_DIM) == c) for p, c in chains]
    ones = jnp.ones((SUBLANES_BF16_V7X, KEY_BLOCK), jnp.bfloat16)

    def update(n, s, vt_rows, pad_to, first):
        m_blk = jnp.max(s, axis=0, keepdims=True)
        m_new = m_blk if first else jnp.maximum(m_ref[n], m_blk)
        pr = jnp.exp(s - m_new).astype(jnp.bfloat16)
        if pad_to is not None:
            pr = _pad_rows(pr, pad_to)
        pv = jnp.dot(vt_rows, pr, preferred_element_type=jnp.float32)
        num, den = pv[:LANES_V7X], pv[LANES_V7X:LANES_V7X + 1]
        if first:
            l_ref[n] = den
            acc_ref[n] = num
        else:
            alpha = jnp.exp(m_ref[n] - m_new)
            l_ref[n] = l_ref[n] * alpha + den
            acc_ref[n] = acc_ref[n] * alpha + num
        m_ref[n] = m_new

    def scores(n, p, start):
        kblk = k_ref[0, pl.ds(start, KEY_BLOCK), p * LANES_V7X:(p + 1) * LANES_V7X]
        z_ref[n] = lax.dot_general(kblk, qms[n], _NT, preferred_element_type=jnp.float32)

    def block_start(j):
        return pl.multiple_of(j * KEY_BLOCK, KEY_BLOCK)

    def block(j, next_start, mask, first):
        ahead = {}

        def next_scores(n):
            p = chains[n][0]
            kblk = k_ref[0, pl.ds(next_start, KEY_BLOCK), p * LANES_V7X:(p + 1) * LANES_V7X]
            ahead[n] = lax.dot_general(kblk, qms[n], _NT, preferred_element_type=jnp.float32)

        def chain(n, p):
            if next_start is not None:
                for m in (range(LOOKAHEAD) if n == 0 else ()):
                    next_scores(m)
                if n + LOOKAHEAD < len(chains):
                    next_scores(n + LOOKAHEAD)
            s = z_ref[n] if mask is None else jnp.where(mask, z_ref[n], NEG_BIG)
            vt_rows = jnp.concatenate([vt_ref[0, j, p * LANES_V7X:(p + 1) * LANES_V7X, :], ones],
                                      axis=0)
            update(n, s, vt_rows, None, first)
            if next_start is not None:
                z_ref[n] = ahead.pop(n)

        return [functools.partial(chain, n, p) for n, (p, c) in enumerate(chains)]

    meta_scores = {}

    def diagonal_scores(n, p):
        scores(n, p, pl.multiple_of(i * qb, qb))

    def score_meta():
        for n, (p, c) in enumerate(chains):
            meta_scores[n] = lax.dot_general(km_ref[:, p * LANES_V7X:(p + 1) * LANES_V7X], qms[n],
                                             _NT, preferred_element_type=jnp.float32)

    def meta(n, p):
        vt_rows = jnp.concatenate([vmt_ref[p * LANES_V7X:(p + 1) * LANES_V7X, :],
                                   ones[:, :LANES_V7X]], axis=0)
        update(n, meta_scores.pop(n), vt_rows, LANES_V7X, False)

    def older_blocks():
        def body(t, c):
            for f in (block(2 * t, block_start(2 * t + 1), None, False)
                      + block(2 * t + 1, block_start(2 * t + 2), None, False)):
                f()
            return c

        prefetching = jnp.maximum(i - 1, 0)
        lax.fori_loop(0, prefetching // 2, body, 0)

        @pl.when(prefetching % 2 == 1)
        def _odd():
            for f in block(i - 2, block_start(i - 1), None, False):
                f()

        @pl.when(i > 0)
        def _last():
            for f in block(i - 1, None, None, False):
                f()

    def finalize():
        lam = (jnp.exp(jnp.sum(lq1_ref[...] * lk1_ref[...], axis=-1, keepdims=True))
               - jnp.exp(jnp.sum(lq2_ref[...] * lk2_ref[...], axis=-1, keepdims=True))
               + LAMBDA_INIT)
        for p in range(tiles):
            ot = (acc_ref[2 * p] * (1.0 / l_ref[2 * p])
                  - acc_ref[2 * p + 1] * (lam / l_ref[2 * p + 1]))
            ot = ot * lax.rsqrt(jnp.mean(ot * ot, axis=0, keepdims=True) + RMS_EPS)
            o = ot.T * (sub_ref[...] * (1.0 - LAMBDA_INIT))
            o_ref[0, :, p * LANES_V7X:(p + 1) * LANES_V7X] = (
                o * g_ref[0, :, p * LANES_V7X:(p + 1) * LANES_V7X]).astype(o_ref.dtype)

    ph = _Phases()
    ph.diagonal_scores = [functools.partial(diagonal_scores, n, p)
                          for n, (p, c) in enumerate(chains)]
    ph.score_meta = score_meta
    ph.diagonal = block(i, block_start(0), chunk_ok, True)
    ph.meta = [functools.partial(meta, n, p) for n, (p, c) in enumerate(chains)]
    ph.older_blocks = older_blocks
    ph.finalize = finalize
    return ph


N_SB_IN, N_DF_IN = 8, 11


def _attn_kernel(*refs, qb, tiles):
    sb_in, df_in = refs[:N_SB_IN], refs[N_SB_IN:N_SB_IN + N_DF_IN]
    wo_ref, x_ref, fg_ref, y_ref = refs[N_SB_IN + N_DF_IN:N_SB_IN + N_DF_IN + 4]
    scratch = refs[N_SB_IN + N_DF_IN + 4:]
    mix_sb, mix_df = scratch[7:]
    sb = _sb_program(*sb_in, mix_sb, *scratch[:3], qb=qb, tiles=tiles)
    df = _diff_program(*df_in, mix_df, *scratch[3:7], qb=qb, tiles=tiles)
    sb.prologue()
    for first in sb.diagonal[0]:
        first()
    for second, other in zip(sb.diagonal[1], df.diagonal_scores):
        second()
        other()
    df.score_meta()
    for first, other in zip(sb.older[0], df.diagonal):
        first()
        other()
    for second, other in zip(sb.older[1], df.meta):
        second()
        other()
    df.older_blocks()
    sb.older_blocks()
    df.finalize()
    sb.finalize()
    half = mix_sb.shape[-1]
    h = (x_ref[0]
         + jnp.dot(mix_sb[0], wo_ref[:half, :], preferred_element_type=jnp.float32)
         + jnp.dot(mix_df[0], wo_ref[half:, :], preferred_element_type=jnp.float32))
    y_ref[0] = _rms(h, fg_ref[...])


def _attention(sb_args, df_args, wo, x, final_gain, qb, tiles):
    q = sb_args[0]
    b, s, width = q.shape
    tw = tiles * LANES_V7X
    assert tw == width, "the fused kernel handles every head of a batch element per grid step"
    qspec = pl.BlockSpec((1, qb, tw), lambda bi, p, i: (bi, i, p))
    kspec = pl.BlockSpec((1, s, tw), lambda bi, p, i: (bi, 0, p))
    vtspec = pl.BlockSpec((1, s // KEY_BLOCK, tw, KEY_BLOCK), lambda bi, p, i: (bi, 0, p, 0))
    kmspec = pl.BlockSpec((N_META, tw), lambda bi, p, i: (0, p))
    vmtspec = pl.BlockSpec((tw, LANES_V7X), lambda bi, p, i: (p, 0))
    full = lambda a: pl.BlockSpec(a.shape, lambda bi, p, i: (0,) * a.ndim)
    common = [qspec, kspec, vtspec, kmspec, vmtspec, qspec]
    xspec = pl.BlockSpec((1, qb, x.shape[-1]), lambda bi, p, i: (bi, i, 0))
    in_specs = (common + [full(a) for a in sb_args[6:]] + common + [full(a) for a in df_args[6:]]
                + [full(wo), xspec, full(final_gain)])
    assert len(sb_args) == N_SB_IN and len(df_args) == N_DF_IN
    stat = pltpu.VMEM((2 * tiles, 1, qb), jnp.float32)
    score = pltpu.VMEM((2 * tiles, KEY_BLOCK, qb), jnp.float32)
    mix = pltpu.VMEM((1, qb, width), jnp.bfloat16)
    return pl.pallas_call(
        functools.partial(_attn_kernel, qb=qb, tiles=tiles),
        grid=(b, width // tw, s // qb),
        in_specs=in_specs,
        out_specs=xspec,
        out_shape=jax.ShapeDtypeStruct(x.shape, jnp.float32),
        scratch_shapes=[pltpu.VMEM((2 * tiles, HEAD_DIM, qb), jnp.float32), stat, score,
                        pltpu.VMEM((2 * tiles, LANES_V7X, qb), jnp.float32), stat, stat, score,
                        mix, mix],
        compiler_params=pltpu.CompilerParams(
            dimension_semantics=("parallel", "parallel", "arbitrary"),
            vmem_limit_bytes=VMEM_LIMIT_BYTES_V7X),
        name="attn",
    )(*sb_args, *df_args, wo, x, final_gain)


def _rope_tables(first, length):
    f32 = np.float32
    inv = f32(1.0) / (f32(ROPE_THETA) ** (np.arange(0, HEAD_DIM, 2, dtype=f32) / f32(HEAD_DIM)))
    ang = np.arange(first, first + length, dtype=f32)[:, None] * inv[None, :]
    cos, sin = np.cos(ang).astype(f32), np.sin(ang).astype(f32)
    return (jnp.asarray(np.concatenate([cos, cos, cos, cos], axis=1)),
            jnp.asarray(np.concatenate([-sin, sin, -sin, sin], axis=1)))


def _forward(x, meta_tokens, norm_gain, w_in, w_out, lambda_q1, lambda_k1, lambda_q2, lambda_k2,
             subln_gain, final_norm_gain, *, qb, tm, tiles):
    assert qb == KEY_BLOCK and tm % KEY_BLOCK == 0
    b, s, d = x.shape
    wb = w_in[0]
    wo = w_out[0].astype(jnp.bfloat16)
    gain = norm_gain[0][None, :]
    x2d = x.reshape(b * s, d)

    px = _project(x2d, gain, wb, *_rope_tables(N_META, s), tm, False)
    sbk_m, sbvt_m, dk_m, dvt_m = _project(meta_tokens, gain, wb, *_rope_tables(0, N_META),
                                          N_META, True)
    sbq, sbk, sbvt, sbg, dq, dk, dvt, dg = px
    rows3 = lambda a: a.reshape(b, s, GROUP)
    blocks = lambda a: a.reshape(b, s // KEY_BLOCK, GROUP, KEY_BLOCK)

    kb = np.arange(KEY_BLOCK)
    later = (kb[None, :] > kb[:, None]).astype(np.float32)
    tri = np.concatenate([later, np.ones((SUBLANES_BF16_V7X, KEY_BLOCK), np.float32)], axis=0)
    trim = np.zeros((N_META, LANES_V7X), np.float32)
    trim[:, :N_META] = later[:N_META, :N_META]
    tri, trim = jnp.asarray(tri, jnp.bfloat16), jnp.asarray(trim, jnp.bfloat16)

    return _attention(
        (rows3(sbq), rows3(sbk), blocks(sbvt), sbk_m, sbvt_m, rows3(sbg), tri, trim),
        (rows3(dq), rows3(dk), blocks(dvt), dk_m, dvt_m, rows3(dg),
         lambda_q1, lambda_k1, lambda_q2, lambda_k2, subln_gain),
        wo, x, final_norm_gain[None, :], qb, tiles)


def kernel(x, meta_tokens, norm_gain, w_in, w_out, lambda_q1, lambda_k1, lambda_q2, lambda_k2,
           subln_gain, final_norm_gain):
    return _forward(x, meta_tokens, norm_gain, w_in, w_out, lambda_q1, lambda_k1, lambda_q2,
                    lambda_k2, subln_gain, final_norm_gain, qb=KEY_BLOCK, tm=1024, tiles=4)
```

```python
import functools

import numpy as np
import jax
import jax.numpy as jnp
from jax import lax
from jax.experimental import pallas as pl
from jax.experimental.pallas import tpu as pltpu

LANES_V7X = 128
SUBLANES_BF16_V7X = 16
VMEM_LIMIT_BYTES_V7X = 56 * 1024 * 1024

N_META = 16
CHUNK = 64
ROPE_THETA = 10000.0
RMS_EPS = 1e-6
HEAD_DIM = 64
GROUP = 512
KEY_BLOCK = 256
NEG_BIG = -1e30
LOG2_E = 1.4426950408889634
LOOKAHEAD = 1
EXP_UNDERFLOW = -104.0
LAMBDA_INIT = 0.8 - 0.6 * float(np.exp(-0.3 * 0))


def _rms(x, g):
    return x * lax.rsqrt(jnp.mean(x * x, axis=-1, keepdims=True) + RMS_EPS) * g


def _proj_kernel(x_ref, xm_ref, g_ref, w_ref, cos_ref, sin_ref, cosm_ref, sinm_ref, *out_refs):
    scale = HEAD_DIM ** -0.5

    def silu(g):
        return g / (1.0 + jnp.exp(-g))

    def projector(rows_ref, cos_ref, sin_ref):
        u = _rms(rows_ref[...], g_ref[...]).astype(jnp.bfloat16)

        def col(c):
            w = w_ref[:, c * GROUP:(c + 1) * GROUP].astype(jnp.bfloat16)
            return jnp.dot(u, w, preferred_element_type=jnp.float32)

        def col_t(c, rows):
            v = col(c)
            if rows > v.shape[0]:
                v = _pad_rows(v, rows)
            return v.astype(jnp.bfloat16).T

        def rope(t):
            half = HEAD_DIM // 2
            lane = lax.broadcasted_iota(jnp.int32, (1, LANES_V7X), 1)
            first_half = (lane // half) % 2 == 0
            cos = jnp.concatenate([cos_ref[...]] * (GROUP // LANES_V7X), axis=1)
            sin = jnp.concatenate([sin_ref[...]] * (GROUP // LANES_V7X), axis=1)
            parts = []
            for j in range(GROUP // LANES_V7X):
                tile = t[:, j * LANES_V7X:(j + 1) * LANES_V7X]
                parts.append(jnp.where(first_half, pltpu.roll(tile, LANES_V7X - half, 1),
                                       pltpu.roll(tile, half, 1)))
            return t * cos + jnp.concatenate(parts, axis=1) * sin

        return col, col_t, rope

    (sbq_ref, sbk_ref, sbvt_ref, sbg_ref, dq_ref, dk_ref, dvt_ref, dg_ref,
     sbkm_ref, sbvtm_ref, dkm_ref, dvtm_ref) = out_refs

    @pl.when(pl.program_id(0) == 0)
    def _meta():
        col, col_t, rope = projector(xm_ref, cosm_ref, sinm_ref)
        sbkm_ref[...] = col(1).astype(jnp.bfloat16)
        sbvtm_ref[...] = col_t(2, LANES_V7X)
        dkm_ref[...] = rope(col(5)).astype(jnp.bfloat16)
        dvtm_ref[...] = col_t(6, LANES_V7X)

    col, col_t, rope = projector(x_ref, cos_ref, sin_ref)
    sbq_ref[...] = (col(0) * scale).astype(jnp.bfloat16)
    sbk_ref[...] = col(1).astype(jnp.bfloat16)
    sbg_ref[...] = silu(col(3))
    dq_ref[...] = (rope(col(4)) * scale).astype(jnp.bfloat16)
    dk_ref[...] = rope(col(5)).astype(jnp.bfloat16)
    dg_ref[...] = silu(col(7))
    for c, ref in ((2, sbvt_ref), (6, dvt_ref)):
        vt = col_t(c, 0)
        for j in range(ref.shape[0]):
            ref[j] = vt[:, j * KEY_BLOCK:(j + 1) * KEY_BLOCK]


def _project(x2d, meta_tokens, gain, wb, tables, meta_tables, tm):
    rows, d = x2d.shape
    n_pos_tiles = tables[0].shape[0] // tm
    row_spec = lambda width: pl.BlockSpec((tm, width), lambda i: (i, 0))
    tab_spec = pl.BlockSpec((tm, LANES_V7X), lambda i: (i % n_pos_tiles, 0))
    whole = lambda shape: pl.BlockSpec(shape, lambda i: (0,) * len(shape))
    bf, f32 = jnp.bfloat16, jnp.float32
    row_out = lambda dt: (row_spec(GROUP), jax.ShapeDtypeStruct((rows, GROUP), dt))
    vt_out = (pl.BlockSpec((tm // KEY_BLOCK, GROUP, KEY_BLOCK), lambda i: (i, 0, 0)),
              jax.ShapeDtypeStruct((rows // KEY_BLOCK, GROUP, KEY_BLOCK), bf))
    meta_out = lambda shape: (whole(shape), jax.ShapeDtypeStruct(shape, bf))
    outs = [row_out(bf), row_out(bf), vt_out, row_out(f32),
            row_out(bf), row_out(bf), vt_out, row_out(f32),
            meta_out((N_META, GROUP)), meta_out((GROUP, LANES_V7X)),
            meta_out((N_META, GROUP)), meta_out((GROUP, LANES_V7X))]
    return pl.pallas_call(
        _proj_kernel,
        grid=(rows // tm,),
        in_specs=[row_spec(d), whole(meta_tokens.shape), whole((1, d)),
                  pl.BlockSpec(wb.shape, lambda i: (0, 0), pipeline_mode=pl.Buffered(1)),
                  tab_spec, tab_spec,
                  whole(meta_tables[0].shape), whole(meta_tables[1].shape)],
        out_specs=[o[0] for o in outs],
        out_shape=[o[1] for o in outs],
        compiler_params=pltpu.CompilerParams(
            dimension_semantics=("arbitrary",), vmem_limit_bytes=VMEM_LIMIT_BYTES_V7X),
        name="proj",
    )(x2d, meta_tokens, gain, wb, *tables, *meta_tables)


_NT = (((1,), (1,)), ((), ()))


def _masked_queries(q_ref, p, keep):
    q = q_ref[0, :, p * LANES_V7X:(p + 1) * LANES_V7X].astype(jnp.float32)
    return jnp.where(keep, q, 0.0).astype(jnp.bfloat16)


def _pad_rows(a, rows):
    return jnp.concatenate([a, jnp.zeros((rows - a.shape[0], a.shape[1]), a.dtype)], axis=0)


class _Phases:
    pass


def _sb_program(q_ref, k_ref, vt_ref, km_ref, vmt_ref, g_ref, tri_ref, trim_ref, o_ref,
                acc_ref, car_ref, z_ref, *, qb, tiles):
    i = pl.program_id(2)
    lane = lax.broadcasted_iota(jnp.int32, (1, LANES_V7X), 1)
    key = lax.broadcasted_iota(jnp.int32, (KEY_BLOCK, qb), 0)
    qry = lax.broadcasted_iota(jnp.int32, (KEY_BLOCK, qb), 1)
    strict = key < qry
    chains = [(p, hh) for p in range(tiles) for hh in range(2)]
    qms = [_masked_queries(q_ref, p, (lane // HEAD_DIM) == hh) for p, hh in chains]

    def log_sigmoids(z, mask):
        log_beta = jnp.minimum(z, 0.0) - jnp.log(1.0 + jnp.exp2(jnp.abs(z) * (-LOG2_E)))
        log_keep = log_beta - z
        if mask is not None:
            log_keep = jnp.where(mask, log_keep, 0.0)
        return log_beta, log_keep.astype(jnp.bfloat16)

    def scores(n, p, start):
        kblk = k_ref[0, pl.ds(start, KEY_BLOCK), p * LANES_V7X:(p + 1) * LANES_V7X]
        z_ref[n] = lax.dot_general(kblk, qms[n], _NT, preferred_element_type=jnp.float32)

    half = KEY_BLOCK // 2
    strict_half = strict[:half, :half]

    def split(a, diagonal):
        return [a[:half, :half], a[:half, half:], a[half:, half:]] if diagonal else [a]

    def join(parts, diagonal):
        if not diagonal:
            return parts[0]
        older = jnp.concatenate(parts[:2], axis=1)
        newer = jnp.concatenate([jnp.zeros_like(parts[2]), parts[2]], axis=1)
        return jnp.concatenate([older, newer], axis=0)

    def stages(this_start, next_start, vtblk, diagonal, exists=None):
        masks = [strict_half, None, strict_half] if diagonal else [None]
        log_betas, sums = {}, {}

        def stage1(n, p):
            if this_start is not None:
                scores(n, p, this_start)
            parts = [log_sigmoids(z, m) for z, m in zip(split(z_ref[n], diagonal), masks)]
            log_betas[n] = [lb for lb, _ in parts]
            log_keep = join([lk for _, lk in parts], diagonal)
            sums[n] = jnp.dot(tri_ref[...], log_keep, preferred_element_type=jnp.float32)
            if next_start is not None:
                scores(n, p, next_start)

        def stage2(n, p, hh):
            ws = []
            for lb, st, m in zip(log_betas.pop(n), split(sums[n][:KEY_BLOCK], diagonal), masks):
                w = jnp.exp(lb + st)
                ws.append((w if m is None else jnp.where(m, w, 0.0)).astype(jnp.bfloat16))
            tot = sums.pop(n)[KEY_BLOCK:KEY_BLOCK + 1]
            pv = jnp.dot(vtblk(p, hh), join(ws, diagonal), preferred_element_type=jnp.float32)
            if diagonal:
                acc_ref[n] = pv
                car_ref[n] = tot
            else:
                factor = jnp.exp(car_ref[n])
                if exists is not None:
                    factor, tot = factor * exists, tot * exists
                acc_ref[n] += pv * factor
                car_ref[n] += tot

        return ([functools.partial(stage1, n, p) for n, (p, hh) in enumerate(chains)],
                [functools.partial(stage2, n, p, hh) for n, (p, hh) in enumerate(chains)])

    def vt_at(j):
        return lambda p, hh: vt_ref[0, j, pl.ds(p * LANES_V7X + hh * HEAD_DIM, HEAD_DIM), :]

    def block_start(j):
        return pl.multiple_of(jnp.maximum(j, 0) * KEY_BLOCK, KEY_BLOCK)

    def prologue():
        for n, (p, hh) in enumerate(chains):
            scores(n, p, block_start(i))

    def older_blocks():
        def largest_carry():
            return jnp.max(functools.reduce(jnp.maximum,
                                            [car_ref[n] for n in range(len(chains))]))

        def live(state):
            t, carry_max = state
            return jnp.logical_and(t < i, carry_max > EXP_UNDERFLOW)

        def body(state):
            t, _ = state
            j = i - 1 - t
            first, second = stages(block_start(j), None, vt_at(j), False)
            for f in first + second:
                f()
            return t + 1, largest_carry()

        _, carry_max = lax.while_loop(live, body, (jnp.int32(1), largest_carry()))

        @pl.when(carry_max > EXP_UNDERFLOW)
        def _meta():
            zs = [lax.dot_general(km_ref[:, p * LANES_V7X:(p + 1) * LANES_V7X], qms[n], _NT,
                                  preferred_element_type=jnp.float32)
                  for n, (p, hh) in enumerate(chains)]
            log_betas, sticks = [], []
            for z in zs:
                log_beta, log_keep = log_sigmoids(z, None)
                sticks.append(jnp.dot(trim_ref[...], _pad_rows(log_keep, LANES_V7X),
                                      preferred_element_type=jnp.float32))
                log_betas.append(log_beta)
            ws = [_pad_rows(jnp.exp(log_betas[n] + sticks[n] + car_ref[n]).astype(jnp.bfloat16),
                            LANES_V7X) for n in range(len(chains))]
            for n, (p, hh) in enumerate(chains):
                acc_ref[n] += jnp.dot(vmt_ref[pl.ds(p * LANES_V7X + hh * HEAD_DIM, HEAD_DIM), :],
                                      ws[n], preferred_element_type=jnp.float32)

    def finalize():
        for p in range(tiles):
            o = jnp.concatenate([acc_ref[2 * p], acc_ref[2 * p + 1]], axis=0).T
            o_ref[0, :, p * LANES_V7X:(p + 1) * LANES_V7X] = (
                o * g_ref[0, :, p * LANES_V7X:(p + 1) * LANES_V7X]).astype(o_ref.dtype)

    ph = _Phases()
    ph.prologue = prologue
    ph.diagonal = stages(None, block_start(i - 1), vt_at(i), True)
    ph.older = stages(None, None, vt_at(jnp.maximum(i - 1, 0)), False,
                      exists=jnp.where(i > 0, 1.0, 0.0).astype(jnp.float32))
    ph.older_blocks = older_blocks
    ph.finalize = finalize
    return ph


def _diff_program(q_ref, k_ref, vt_ref, km_ref, vmt_ref, g_ref,
                  lq1_ref, lk1_ref, lq2_ref, lk2_ref, sub_ref, o_ref, acc_ref, m_ref, l_ref, z_ref,
                  *, qb, tiles):
    i = pl.program_id(2)
    lane = lax.broadcasted_iota(jnp.int32, (1, LANES_V7X), 1)
    key = lax.broadcasted_iota(jnp.int32, (KEY_BLOCK, qb), 0)
    qry = lax.broadcasted_iota(jnp.int32, (KEY_BLOCK, qb), 1)
    chunk_ok = (qry // CHUNK) >= (key // CHUNK)
    chains = [(p, c) for p in range(tiles) for c in range(2)]
    qms = [_masked_queries(q_ref, p, (lane // HEAD_DIM) == c) for p, c in chains]
    ones = jnp.ones((SUBLANES_BF16_V7X, KEY_BLOCK), jnp.bfloat16)

    def update(n, s, vt_rows, pad_to, first):
        m_blk = jnp.max(s, axis=0, keepdims=True)
        m_new = m_blk if first else jnp.maximum(m_ref[n], m_blk)
        pr = jnp.exp(s - m_new).astype(jnp.bfloat16)
        if pad_to is not None:
            pr = _pad_rows(pr, pad_to)
        pv = jnp.dot(vt_rows, pr, preferred_element_type=jnp.float32)
        num, den = pv[:LANES_V7X], pv[LANES_V7X:LANES_V7X + 1]
        if first:
            l_ref[n] = den
            acc_ref[n] = num
        else:
            alpha = jnp.exp(m_ref[n] - m_new)
            l_ref[n] = l_ref[n] * alpha + den
            acc_ref[n] = acc_ref[n] * alpha + num
        m_ref[n] = m_new

    def scores(n, p, start):
        kblk = k_ref[0, pl.ds(start, KEY_BLOCK), p * LANES_V7X:(p + 1) * LANES_V7X]
        z_ref[n] = lax.dot_general(kblk, qms[n], _NT, preferred_element_type=jnp.float32)

    def block_start(j):
        return pl.multiple_of(j * KEY_BLOCK, KEY_BLOCK)

    def block(j, next_start, mask, first):
        ahead = {}

        def next_scores(n):
            p = chains[n][0]
            kblk = k_ref[0, pl.ds(next_start, KEY_BLOCK), p * LANES_V7X:(p + 1) * LANES_V7X]
            ahead[n] = lax.dot_general(kblk, qms[n], _NT, preferred_element_type=jnp.float32)

        def chain(n, p):
            if next_start is not None:
                for m in (range(LOOKAHEAD) if n == 0 else ()):
                    next_scores(m)
                if n + LOOKAHEAD < len(chains):
                    next_scores(n + LOOKAHEAD)
            s = z_ref[n] if mask is None else jnp.where(mask, z_ref[n], NEG_BIG)
            vt_rows = jnp.concatenate([vt_ref[0, j, p * LANES_V7X:(p + 1) * LANES_V7X, :], ones],
                                      axis=0)
            update(n, s, vt_rows, None, first)
            if next_start is not None:
                z_ref[n] = ahead.pop(n)

        return [functools.partial(chain, n, p) for n, (p, c) in enumerate(chains)]

    meta_scores = {}

    def diagonal_scores(n, p):
        scores(n, p, pl.multiple_of(i * qb, qb))

    def score_meta():
        for n, (p, c) in enumerate(chains):
            meta_scores[n] = lax.dot_general(km_ref[:, p * LANES_V7X:(p + 1) * LANES_V7X], qms[n],
                                             _NT, preferred_element_type=jnp.float32)

    def meta(n, p):
        vt_rows = jnp.concatenate([vmt_ref[p * LANES_V7X:(p + 1) * LANES_V7X, :],
                                   ones[:, :LANES_V7X]], axis=0)
        update(n, meta_scores.pop(n), vt_rows, LANES_V7X, False)

    def older_blocks():
        def body(t, c):
            for f in (block(2 * t, block_start(2 * t + 1), None, False)
                      + block(2 * t + 1, block_start(2 * t + 2), None, False)):
                f()
            return c

        prefetching = jnp.maximum(i - 1, 0)
        lax.fori_loop(0, prefetching // 2, body, 0)

        @pl.when(prefetching % 2 == 1)
        def _odd():
            for f in block(i - 2, block_start(i - 1), None, False):
                f()

        @pl.when(i > 0)
        def _last():
            for f in block(i - 1, None, None, False):
                f()

    def finalize():
        lam = (jnp.exp(jnp.sum(lq1_ref[...] * lk1_ref[...], axis=-1, keepdims=True))
               - jnp.exp(jnp.sum(lq2_ref[...] * lk2_ref[...], axis=-1, keepdims=True))
               + LAMBDA_INIT)
        for p in range(tiles):
            ot = (acc_ref[2 * p] * (1.0 / l_ref[2 * p])
                  - acc_ref[2 * p + 1] * (lam / l_ref[2 * p + 1]))
            ot = ot * lax.rsqrt(jnp.mean(ot * ot, axis=0, keepdims=True) + RMS_EPS)
            o = ot.T * (sub_ref[...] * (1.0 - LAMBDA_INIT))
            o_ref[0, :, p * LANES_V7X:(p + 1) * LANES_V7X] = (
                o * g_ref[0, :, p * LANES_V7X:(p + 1) * LANES_V7X]).astype(o_ref.dtype)

    ph = _Phases()
    ph.diagonal_scores = [functools.partial(diagonal_scores, n, p)
                          for n, (p, c) in enumerate(chains)]
    ph.score_meta = score_meta
    ph.diagonal = block(i, block_start(0), chunk_ok, True)
    ph.meta = [functools.partial(meta, n, p) for n, (p, c) in enumerate(chains)]
    ph.older_blocks = older_blocks
    ph.finalize = finalize
    return ph


N_SB_IN, N_DF_IN = 8, 11


def _attn_kernel(*refs, qb, tiles):
    sb_in, df_in = refs[:N_SB_IN], refs[N_SB_IN:N_SB_IN + N_DF_IN]
    wo_ref, x_ref, fg_ref, y_ref = refs[N_SB_IN + N_DF_IN:N_SB_IN + N_DF_IN + 4]
    scratch = refs[N_SB_IN + N_DF_IN + 4:]
    mix_sb, mix_df = scratch[7:]
    sb = _sb_program(*sb_in, mix_sb, *scratch[:3], qb=qb, tiles=tiles)
    df = _diff_program(*df_in, mix_df, *scratch[3:7], qb=qb, tiles=tiles)
    sb.prologue()
    for first in sb.diagonal[0]:
        first()
    for second, other in zip(sb.diagonal[1], df.diagonal_scores):
        second()
        other()
    df.score_meta()
    for first, other in zip(sb.older[0], df.diagonal):
        first()
        other()
    for second, other in zip(sb.older[1], df.meta):
        second()
        other()
    df.older_blocks()
    sb.older_blocks()
    df.finalize()
    sb.finalize()
    half = mix_sb.shape[-1]
    h = (x_ref[0]
         + jnp.dot(mix_sb[0], wo_ref[:half, :], preferred_element_type=jnp.float32)
         + jnp.dot(mix_df[0], wo_ref[half:, :], preferred_element_type=jnp.float32))
    y_ref[0] = _rms(h, fg_ref[...])


def _attention(sb_args, df_args, wo, x, final_gain, qb, tiles):
    q = sb_args[0]
    b, s, width = q.shape
    tw = tiles * LANES_V7X
    assert tw == width, "the fused kernel handles every head of a batch element per grid step"
    qspec = pl.BlockSpec((1, qb, tw), lambda bi, p, i: (bi, i, p))
    kspec = pl.BlockSpec((1, s, tw), lambda bi, p, i: (bi, 0, p))
    vtspec = pl.BlockSpec((1, s // KEY_BLOCK, tw, KEY_BLOCK), lambda bi, p, i: (bi, 0, p, 0))
    kmspec = pl.BlockSpec((N_META, tw), lambda bi, p, i: (0, p))
    vmtspec = pl.BlockSpec((tw, LANES_V7X), lambda bi, p, i: (p, 0))
    full = lambda a: pl.BlockSpec(a.shape, lambda bi, p, i: (0,) * a.ndim)
    common = [qspec, kspec, vtspec, kmspec, vmtspec, qspec]
    xspec = pl.BlockSpec((1, qb, x.shape[-1]), lambda bi, p, i: (bi, i, 0))
    in_specs = (common + [full(a) for a in sb_args[6:]] + common + [full(a) for a in df_args[6:]]
                + [full(wo), xspec, full(final_gain)])
    assert len(sb_args) == N_SB_IN and len(df_args) == N_DF_IN
    stat = pltpu.VMEM((2 * tiles, 1, qb), jnp.float32)
    score = pltpu.VMEM((2 * tiles, KEY_BLOCK, qb), jnp.float32)
    mix = pltpu.VMEM((1, qb, width), jnp.bfloat16)
    return pl.pallas_call(
        functools.partial(_attn_kernel, qb=qb, tiles=tiles),
        grid=(b, width // tw, s // qb),
        in_specs=in_specs,
        out_specs=xspec,
        out_shape=jax.ShapeDtypeStruct(x.shape, jnp.float32),
        scratch_shapes=[pltpu.VMEM((2 * tiles, HEAD_DIM, qb), jnp.float32), stat, score,
                        pltpu.VMEM((2 * tiles, LANES_V7X, qb), jnp.float32), stat, stat, score,
                        mix, mix],
        compiler_params=pltpu.CompilerParams(
            dimension_semantics=("parallel", "parallel", "arbitrary"),
            vmem_limit_bytes=VMEM_LIMIT_BYTES_V7X),
        name="attn",
    )(*sb_args, *df_args, wo, x, final_gain)


def _rope_tables(first, length):
    f32 = np.float32
    inv = f32(1.0) / (f32(ROPE_THETA) ** (np.arange(0, HEAD_DIM, 2, dtype=f32) / f32(HEAD_DIM)))
    ang = np.arange(first, first + length, dtype=f32)[:, None] * inv[None, :]
    cos, sin = np.cos(ang).astype(f32), np.sin(ang).astype(f32)
    return (jnp.asarray(np.concatenate([cos, cos, cos, cos], axis=1)),
            jnp.asarray(np.concatenate([-sin, sin, -sin, sin], axis=1)))


def _forward(x, meta_tokens, norm_gain, w_in, w_out, lambda_q1, lambda_k1, lambda_q2, lambda_k2,
             subln_gain, final_norm_gain, *, qb, tm, tiles):
    assert qb == KEY_BLOCK and tm % KEY_BLOCK == 0
    b, s, d = x.shape
    wb = w_in[0]
    wo = w_out[0].astype(jnp.bfloat16)
    gain = norm_gain[0][None, :]
    x2d = x.reshape(b * s, d)

    (sbq, sbk, sbvt, sbg, dq, dk, dvt, dg, sbk_m, sbvt_m, dk_m, dvt_m) = _project(
        x2d, meta_tokens, gain, wb, _rope_tables(N_META, s), _rope_tables(0, N_META), tm)
    rows3 = lambda a: a.reshape(b, s, GROUP)
    blocks = lambda a: a.reshape(b, s // KEY_BLOCK, GROUP, KEY_BLOCK)

    kb = np.arange(KEY_BLOCK)
    later = (kb[None, :] > kb[:, None]).astype(np.float32)
    tri = np.concatenate([later, np.ones((SUBLANES_BF16_V7X, KEY_BLOCK), np.float32)], axis=0)
    trim = np.zeros((N_META, LANES_V7X), np.float32)
    trim[:, :N_META] = later[:N_META, :N_META]
    tri, trim = jnp.asarray(tri, jnp.bfloat16), jnp.asarray(trim, jnp.bfloat16)

    return _attention(
        (rows3(sbq), rows3(sbk), blocks(sbvt), sbk_m, sbvt_m, rows3(sbg), tri, trim),
        (rows3(dq), rows3(dk), blocks(dvt), dk_m, dvt_m, rows3(dg),
         lambda_q1, lambda_k1, lambda_q2, lambda_k2, subln_gain),
        wo, x, final_norm_gain[None, :], qb, tiles)


def kernel(x, meta_tokens, norm_gain, w_in, w_out, lambda_q1, lambda_k1, lambda_q2, lambda_k2,
           subln_gain, final_norm_gain):
    return _forward(x, meta_tokens, norm_gain, w_in, w_out, lambda_q1, lambda_k1, lambda_q2,
                    lambda_k2, subln_gain, final_norm_gain, qb=KEY_BLOCK, tm=1024, tiles=4)
```

```python
import functools

import numpy as np
import jax
import jax.numpy as jnp
from jax import lax
from jax.experimental import pallas as pl
from jax.experimental.pallas import tpu as pltpu

LANES_V7X = 128
SUBLANES_BF16_V7X = 16
VMEM_LIMIT_BYTES_V7X = 56 * 1024 * 1024

N_META = 16
CHUNK = 64
ROPE_THETA = 10000.0
RMS_EPS = 1e-6
HEAD_DIM = 64
GROUP = 512
KEY_BLOCK = 256
NEG_BIG = -1e30
LOG2_E = 1.4426950408889634
LOOKAHEAD = 1
EXP_UNDERFLOW = -104.0
LAMBDA_INIT = 0.8 - 0.6 * float(np.exp(-0.3 * 0))


def _rms(x, g):
    return x * lax.rsqrt(jnp.mean(x * x, axis=-1, keepdims=True) + RMS_EPS) * g


def _proj_kernel(x_ref, xm_ref, g_ref, w_ref, cos_ref, sin_ref, cosm_ref, sinm_ref, *out_refs):
    scale = HEAD_DIM ** -0.5

    def silu(g):
        return g / (1.0 + jnp.exp(-g))

    def projector(rows_ref, cos_ref, sin_ref):
        u = _rms(rows_ref[...], g_ref[...]).astype(jnp.bfloat16)

        def col(c):
            w = w_ref[:, c * GROUP:(c + 1) * GROUP].astype(jnp.bfloat16)
            return jnp.dot(u, w, preferred_element_type=jnp.float32)

        def col_t(c, rows):
            v = col(c)
            if rows > v.shape[0]:
                v = _pad_rows(v, rows)
            return v.astype(jnp.bfloat16).T

        def rope(t):
            half = HEAD_DIM // 2
            lane = lax.broadcasted_iota(jnp.int32, (1, LANES_V7X), 1)
            first_half = (lane // half) % 2 == 0
            cos = jnp.concatenate([cos_ref[...]] * (GROUP // LANES_V7X), axis=1)
            sin = jnp.concatenate([sin_ref[...]] * (GROUP // LANES_V7X), axis=1)
            parts = []
            for j in range(GROUP // LANES_V7X):
                tile = t[:, j * LANES_V7X:(j + 1) * LANES_V7X]
                parts.append(jnp.where(first_half, pltpu.roll(tile, LANES_V7X - half, 1),
                                       pltpu.roll(tile, half, 1)))
            return t * cos + jnp.concatenate(parts, axis=1) * sin

        return col, col_t, rope

    (sbq_ref, sbk_ref, sbvt_ref, sbg_ref, dq_ref, dk_ref, dvt_ref, dg_ref,
     sbkm_ref, sbvtm_ref, dkm_ref, dvtm_ref) = out_refs

    @pl.when(pl.program_id(0) == 0)
    def _meta():
        col, col_t, rope = projector(xm_ref, cosm_ref, sinm_ref)
        sbkm_ref[...] = col(1).astype(jnp.bfloat16)
        sbvtm_ref[...] = col_t(2, LANES_V7X)
        dkm_ref[...] = rope(col(5)).astype(jnp.bfloat16)
        dvtm_ref[...] = col_t(6, LANES_V7X)

    col, col_t, rope = projector(x_ref, cos_ref, sin_ref)
    sbq_ref[...] = (col(0) * scale).astype(jnp.bfloat16)
    sbk_ref[...] = col(1).astype(jnp.bfloat16)
    sbg_ref[...] = silu(col(3))
    dq_ref[...] = (rope(col(4)) * scale).astype(jnp.bfloat16)
    dk_ref[...] = rope(col(5)).astype(jnp.bfloat16)
    dg_ref[...] = silu(col(7))
    for c, ref in ((2, sbvt_ref), (6, dvt_ref)):
        vt = col_t(c, 0)
        for j in range(ref.shape[0]):
            ref[j] = vt[:, j * KEY_BLOCK:(j + 1) * KEY_BLOCK]


def _project(x2d, meta_tokens, gain, wb, tables, meta_tables, tm):
    rows, d = x2d.shape
    n_pos_tiles = tables[0].shape[0] // tm
    row_spec = lambda width: pl.BlockSpec((tm, width), lambda i: (i, 0))
    tab_spec = pl.BlockSpec((tm, LANES_V7X), lambda i: (i % n_pos_tiles, 0))
    whole = lambda shape: pl.BlockSpec(shape, lambda i: (0,) * len(shape))
    bf, f32 = jnp.bfloat16, jnp.float32
    row_out = lambda dt: (row_spec(GROUP), jax.ShapeDtypeStruct((rows, GROUP), dt))
    vt_out = (pl.BlockSpec((tm // KEY_BLOCK, GROUP, KEY_BLOCK), lambda i: (i, 0, 0)),
              jax.ShapeDtypeStruct((rows // KEY_BLOCK, GROUP, KEY_BLOCK), bf))
    meta_out = lambda shape: (whole(shape), jax.ShapeDtypeStruct(shape, bf))
    outs = [row_out(bf), row_out(bf), vt_out, row_out(f32),
            row_out(bf), row_out(bf), vt_out, row_out(f32),
            meta_out((N_META, GROUP)), meta_out((GROUP, LANES_V7X)),
            meta_out((N_META, GROUP)), meta_out((GROUP, LANES_V7X))]
    return pl.pallas_call(
        _proj_kernel,
        grid=(rows // tm,),
        in_specs=[row_spec(d), whole(meta_tokens.shape), whole((1, d)),
                  pl.BlockSpec(wb.shape, lambda i: (0, 0), pipeline_mode=pl.Buffered(1)),
                  tab_spec, tab_spec,
                  whole(meta_tables[0].shape), whole(meta_tables[1].shape)],
        out_specs=[o[0] for o in outs],
        out_shape=[o[1] for o in outs],
        compiler_params=pltpu.CompilerParams(
            dimension_semantics=("arbitrary",), vmem_limit_bytes=VMEM_LIMIT_BYTES_V7X),
        name="proj",
    )(x2d, meta_tokens, gain, wb, *tables, *meta_tables)


_NT = (((1,), (1,)), ((), ()))


def _masked_queries(q_ref, p, keep):
    q = q_ref[0, :, p * LANES_V7X:(p + 1) * LANES_V7X].astype(jnp.float32)
    return jnp.where(keep, q, 0.0).astype(jnp.bfloat16)


def _pad_rows(a, rows):
    return jnp.concatenate([a, jnp.zeros((rows - a.shape[0], a.shape[1]), a.dtype)], axis=0)


class _Phases:
    pass


def _sb_program(q_ref, k_ref, vt_ref, km_ref, vmt_ref, g_ref, tri_ref, trim_ref, o_ref,
                acc_ref, car_ref, z_ref, *, qb, tiles):
    i = pl.program_id(2)
    lane = lax.broadcasted_iota(jnp.int32, (1, LANES_V7X), 1)
    key = lax.broadcasted_iota(jnp.int32, (KEY_BLOCK, qb), 0)
    qry = lax.broadcasted_iota(jnp.int32, (KEY_BLOCK, qb), 1)
    strict = key < qry
    chains = [(p, hh) for p in range(tiles) for hh in range(2)]
    qms = [_masked_queries(q_ref, p, (lane // HEAD_DIM) == hh) for p, hh in chains]

    def log_sigmoids(z, mask):
        log_beta = jnp.minimum(z, 0.0) - jnp.log(1.0 + jnp.exp2(jnp.abs(z) * (-LOG2_E)))
        log_keep = log_beta - z
        if mask is not None:
            log_keep = jnp.where(mask, log_keep, 0.0)
        return log_beta, log_keep.astype(jnp.bfloat16)

    def scores(n, p, start):
        kblk = k_ref[0, pl.ds(start, KEY_BLOCK), p * LANES_V7X:(p + 1) * LANES_V7X]
        z_ref[n] = lax.dot_general(kblk, qms[n], _NT, preferred_element_type=jnp.float32)

    half = KEY_BLOCK // 2
    strict_half = strict[:half, :half]

    def split(a, diagonal):
        return [a[:half, :half], a[:half, half:], a[half:, half:]] if diagonal else [a]

    def join(parts, diagonal):
        if not diagonal:
            return parts[0]
        older = jnp.concatenate(parts[:2], axis=1)
        newer = jnp.concatenate([jnp.zeros_like(parts[2]), parts[2]], axis=1)
        return jnp.concatenate([older, newer], axis=0)

    def stages(this_start, next_start, vtblk, diagonal, exists=None):
        masks = [strict_half, None, strict_half] if diagonal else [None]
        log_betas, sums = {}, {}

        def stage1(n, p):
            if this_start is not None:
                scores(n, p, this_start)
            parts = [log_sigmoids(z, m) for z, m in zip(split(z_ref[n], diagonal), masks)]
            log_betas[n] = [lb for lb, _ in parts]
            log_keep = join([lk for _, lk in parts], diagonal)
            sums[n] = jnp.dot(tri_ref[...], log_keep, preferred_element_type=jnp.float32)
            if next_start is not None:
                scores(n, p, next_start)

        def stage2(n, p, hh):
            ws = []
            for lb, st, m in zip(log_betas.pop(n), split(sums[n][:KEY_BLOCK], diagonal), masks):
                w = jnp.exp(lb + st)
                ws.append((w if m is None else jnp.where(m, w, 0.0)).astype(jnp.bfloat16))
            tot = sums.pop(n)[KEY_BLOCK:KEY_BLOCK + 1]
            pv = jnp.dot(vtblk(p, hh), join(ws, diagonal), preferred_element_type=jnp.float32)
            if diagonal:
                acc_ref[n] = pv
                car_ref[n] = tot
            else:
                factor = jnp.exp(car_ref[n])
                if exists is not None:
                    factor, tot = factor * exists, tot * exists
                acc_ref[n] += pv * factor
                car_ref[n] += tot

        return ([functools.partial(stage1, n, p) for n, (p, hh) in enumerate(chains)],
                [functools.partial(stage2, n, p, hh) for n, (p, hh) in enumerate(chains)])

    def vt_at(j):
        return lambda p, hh: vt_ref[0, j, pl.ds(p * LANES_V7X + hh * HEAD_DIM, HEAD_DIM), :]

    def block_start(j):
        return pl.multiple_of(jnp.maximum(j, 0) * KEY_BLOCK, KEY_BLOCK)

    def prologue():
        for n, (p, hh) in enumerate(chains):
            scores(n, p, block_start(i))

    def older_blocks():
        def largest_carry():
            return jnp.max(functools.reduce(jnp.maximum,
                                            [car_ref[n] for n in range(len(chains))]))

        def live(state):
            t, carry_max = state
            return jnp.logical_and(t < i, carry_max > EXP_UNDERFLOW)

        def body(state):
            t, _ = state
            j = i - 1 - t
            first, second = stages(block_start(j), None, vt_at(j), False)
            for f in first + second:
                f()
            return t + 1, largest_carry()

        _, carry_max = lax.while_loop(live, body, (jnp.int32(1), largest_carry()))

        @pl.when(carry_max > EXP_UNDERFLOW)
        def _meta():
            zs = [lax.dot_general(km_ref[:, p * LANES_V7X:(p + 1) * LANES_V7X], qms[n], _NT,
                                  preferred_element_type=jnp.float32)
                  for n, (p, hh) in enumerate(chains)]
            log_betas, sticks = [], []
            for z in zs:
                log_beta, log_keep = log_sigmoids(z, None)
                sticks.append(jnp.dot(trim_ref[...], _pad_rows(log_keep, LANES_V7X),
                                      preferred_element_type=jnp.float32))
                log_betas.append(log_beta)
            ws = [_pad_rows(jnp.exp(log_betas[n] + sticks[n] + car_ref[n]).astype(jnp.bfloat16),
                            LANES_V7X) for n in range(len(chains))]
            for n, (p, hh) in enumerate(chains):
                acc_ref[n] += jnp.dot(vmt_ref[pl.ds(p * LANES_V7X + hh * HEAD_DIM, HEAD_DIM), :],
                                      ws[n], preferred_element_type=jnp.float32)

    def finalize():
        for p in range(tiles):
            o = jnp.concatenate([acc_ref[2 * p], acc_ref[2 * p + 1]], axis=0).T
            o_ref[0, :, p * LANES_V7X:(p + 1) * LANES_V7X] = (
                o * g_ref[0, :, p * LANES_V7X:(p + 1) * LANES_V7X]).astype(o_ref.dtype)

    ph = _Phases()
    ph.prologue = prologue
    ph.diagonal = stages(None, block_start(i - 1), vt_at(i), True)
    ph.older = stages(None, None, vt_at(jnp.maximum(i - 1, 0)), False,
                      exists=jnp.where(i > 0, 1.0, 0.0).astype(jnp.float32))
    ph.older_blocks = older_blocks
    ph.finalize = finalize
    return ph


def _diff_program(q_ref, k_ref, vt_ref, km_ref, vmt_ref, g_ref,
                  lq1_ref, lk1_ref, lq2_ref, lk2_ref, sub_ref, o_ref, acc_ref, m_ref, l_ref, z_ref,
                  *, qb, tiles):
    i = pl.program_id(2)
    lane = lax.broadcasted_iota(jnp.int32, (1, LANES_V7X), 1)
    key = lax.broadcasted_iota(jnp.int32, (KEY_BLOCK, qb), 0)
    qry = lax.broadcasted_iota(jnp.int32, (KEY_BLOCK, qb), 1)
    chunk_ok = (qry // CHUNK) >= (key // CHUNK)
    chains = [(p, c) for p in range(tiles) for c in range(2)]
    qms = [_masked_queries(q_ref, p, (lane // HEAD_DIM) == c) for p, c in chains]
    ones = jnp.ones((SUBLANES_BF16_V7X, KEY_BLOCK), jnp.bfloat16)

    def update(n, s, vt_rows, pad_to, first):
        if first:
            m_new, pr = diagonal_probabilities(s)
        else:
            m_new = jnp.maximum(m_ref[n], jnp.max(s, axis=0, keepdims=True))
            pr = jnp.exp(s - m_new).astype(jnp.bfloat16)
        if pad_to is not None:
            pr = _pad_rows(pr, pad_to)
        pv = jnp.dot(vt_rows, pr, preferred_element_type=jnp.float32)
        num, den = pv[:LANES_V7X], pv[LANES_V7X:LANES_V7X + 1]
        if first:
            l_ref[n] = den
            acc_ref[n] = num
        else:
            alpha = jnp.exp(m_ref[n] - m_new)
            l_ref[n] = l_ref[n] * alpha + den
            acc_ref[n] = acc_ref[n] * alpha + num
        m_ref[n] = m_new

    def diagonal_probabilities(z):
        half = KEY_BLOCK // 2
        chunk_half = chunk_ok[:half, :half]
        older_early = jnp.where(chunk_half, z[:half, :half], NEG_BIG)
        older_late = z[:half, half:]
        newer_late = jnp.where(chunk_half, z[half:, half:], NEG_BIG)
        m_early = jnp.max(older_early, axis=0, keepdims=True)
        m_late = jnp.maximum(jnp.max(older_late, axis=0, keepdims=True),
                             jnp.max(newer_late, axis=0, keepdims=True))
        pe = lambda part, m: jnp.exp(part - m).astype(jnp.bfloat16)
        newer = jnp.concatenate([jnp.zeros((half, half), jnp.bfloat16), pe(newer_late, m_late)],
                                axis=1)
        older = jnp.concatenate([pe(older_early, m_early), pe(older_late, m_late)], axis=1)
        return (jnp.concatenate([m_early, m_late], axis=1),
                jnp.concatenate([older, newer], axis=0))

    def scores(n, p, start):
        kblk = k_ref[0, pl.ds(start, KEY_BLOCK), p * LANES_V7X:(p + 1) * LANES_V7X]
        z_ref[n] = lax.dot_general(kblk, qms[n], _NT, preferred_element_type=jnp.float32)

    def block_start(j):
        return pl.multiple_of(j * KEY_BLOCK, KEY_BLOCK)

    def block(j, next_start, first):
        ahead = {}

        def next_scores(n):
            p = chains[n][0]
            kblk = k_ref[0, pl.ds(next_start, KEY_BLOCK), p * LANES_V7X:(p + 1) * LANES_V7X]
            ahead[n] = lax.dot_general(kblk, qms[n], _NT, preferred_element_type=jnp.float32)

        def chain(n, p):
            if next_start is not None:
                for m in (range(LOOKAHEAD) if n == 0 else ()):
                    next_scores(m)
                if n + LOOKAHEAD < len(chains):
                    next_scores(n + LOOKAHEAD)
            s = z_ref[n]
            vt_rows = jnp.concatenate([vt_ref[0, j, p * LANES_V7X:(p + 1) * LANES_V7X, :], ones],
                                      axis=0)
            update(n, s, vt_rows, None, first)
            if next_start is not None:
                z_ref[n] = ahead.pop(n)

        return [functools.partial(chain, n, p) for n, (p, c) in enumerate(chains)]

    meta_scores = {}

    def diagonal_scores(n, p):
        scores(n, p, pl.multiple_of(i * qb, qb))

    def score_meta():
        for n, (p, c) in enumerate(chains):
            meta_scores[n] = lax.dot_general(km_ref[:, p * LANES_V7X:(p + 1) * LANES_V7X], qms[n],
                                             _NT, preferred_element_type=jnp.float32)

    def meta(n, p):
        vt_rows = jnp.concatenate([vmt_ref[p * LANES_V7X:(p + 1) * LANES_V7X, :],
                                   ones[:, :LANES_V7X]], axis=0)
        update(n, meta_scores.pop(n), vt_rows, LANES_V7X, False)

    def older_blocks():
        def body(t, c):
            for f in (block(2 * t, block_start(2 * t + 1), False)
                      + block(2 * t + 1, block_start(2 * t + 2), False)):
                f()
            return c

        prefetching = jnp.maximum(i - 1, 0)
        lax.fori_loop(0, prefetching // 2, body, 0)

        @pl.when(prefetching % 2 == 1)
        def _odd():
            for f in block(i - 2, block_start(i - 1), False):
                f()

        @pl.when(i > 0)
        def _last():
            for f in block(i - 1, None, False):
                f()

    def finalize():
        lam = (jnp.exp(jnp.sum(lq1_ref[...] * lk1_ref[...], axis=-1, keepdims=True))
               - jnp.exp(jnp.sum(lq2_ref[...] * lk2_ref[...], axis=-1, keepdims=True))
               + LAMBDA_INIT)
        for p in range(tiles):
            ot = (acc_ref[2 * p] * (1.0 / l_ref[2 * p])
                  - acc_ref[2 * p + 1] * (lam / l_ref[2 * p + 1]))
            ot = ot * lax.rsqrt(jnp.mean(ot * ot, axis=0, keepdims=True) + RMS_EPS)
            o = ot.T * (sub_ref[...] * (1.0 - LAMBDA_INIT))
            o_ref[0, :, p * LANES_V7X:(p + 1) * LANES_V7X] = (
                o * g_ref[0, :, p * LANES_V7X:(p + 1) * LANES_V7X]).astype(o_ref.dtype)

    ph = _Phases()
    ph.diagonal_scores = [functools.partial(diagonal_scores, n, p)
                          for n, (p, c) in enumerate(chains)]
    ph.score_meta = score_meta
    ph.diagonal = block(i, block_start(0), True)
    ph.meta = [functools.partial(meta, n, p) for n, (p, c) in enumerate(chains)]
    ph.older_blocks = older_blocks
    ph.finalize = finalize
    return ph


N_SB_IN, N_DF_IN = 8, 11


def _attn_kernel(*refs, qb, tiles):
    sb_in, df_in = refs[:N_SB_IN], refs[N_SB_IN:N_SB_IN + N_DF_IN]
    wo_ref, x_ref, fg_ref, y_ref = refs[N_SB_IN + N_DF_IN:N_SB_IN + N_DF_IN + 4]
    scratch = refs[N_SB_IN + N_DF_IN + 4:]
    mix_sb, mix_df = scratch[7:]
    sb = _sb_program(*sb_in, mix_sb, *scratch[:3], qb=qb, tiles=tiles)
    df = _diff_program(*df_in, mix_df, *scratch[3:7], qb=qb, tiles=tiles)
    sb.prologue()
    for first in sb.diagonal[0]:
        first()
    for second, other in zip(sb.diagonal[1], df.diagonal_scores):
        second()
        other()
    df.score_meta()
    for first, other in zip(sb.older[0], df.diagonal):
        first()
        other()
    for second, other in zip(sb.older[1], df.meta):
        second()
        other()
    df.older_blocks()
    sb.older_blocks()
    df.finalize()
    sb.finalize()
    half = mix_sb.shape[-1]
    h = (x_ref[0]
         + jnp.dot(mix_sb[0], wo_ref[:half, :], preferred_element_type=jnp.float32)
         + jnp.dot(mix_df[0], wo_ref[half:, :], preferred_element_type=jnp.float32))
    y_ref[0] = _rms(h, fg_ref[...])


def _attention(sb_args, df_args, wo, x, final_gain, qb, tiles):
    q = sb_args[0]
    b, s, width = q.shape
    tw = tiles * LANES_V7X
    assert tw == width, "the fused kernel handles every head of a batch element per grid step"
    qspec = pl.BlockSpec((1, qb, tw), lambda bi, p, i: (bi, i, p))
    kspec = pl.BlockSpec((1, s, tw), lambda bi, p, i: (bi, 0, p))
    vtspec = pl.BlockSpec((1, s // KEY_BLOCK, tw, KEY_BLOCK), lambda bi, p, i: (bi, 0, p, 0))
    kmspec = pl.BlockSpec((N_META, tw), lambda bi, p, i: (0, p))
    vmtspec = pl.BlockSpec((tw, LANES_V7X), lambda bi, p, i: (p, 0))
    full = lambda a: pl.BlockSpec(a.shape, lambda bi, p, i: (0,) * a.ndim)
    common = [qspec, kspec, vtspec, kmspec, vmtspec, qspec]
    xspec = pl.BlockSpec((1, qb, x.shape[-1]), lambda bi, p, i: (bi, i, 0))
    in_specs = (common + [full(a) for a in sb_args[6:]] + common + [full(a) for a in df_args[6:]]
                + [full(wo), xspec, full(final_gain)])
    assert len(sb_args) == N_SB_IN and len(df_args) == N_DF_IN
    stat = pltpu.VMEM((2 * tiles, 1, qb), jnp.float32)
    score = pltpu.VMEM((2 * tiles, KEY_BLOCK, qb), jnp.float32)
    mix = pltpu.VMEM((1, qb, width), jnp.bfloat16)
    return pl.pallas_call(
        functools.partial(_attn_kernel, qb=qb, tiles=tiles),
        grid=(b, width // tw, s // qb),
        in_specs=in_specs,
        out_specs=xspec,
        out_shape=jax.ShapeDtypeStruct(x.shape, jnp.float32),
        scratch_shapes=[pltpu.VMEM((2 * tiles, HEAD_DIM, qb), jnp.float32), stat, score,
                        pltpu.VMEM((2 * tiles, LANES_V7X, qb), jnp.float32), stat, stat, score,
                        mix, mix],
        compiler_params=pltpu.CompilerParams(
            dimension_semantics=("parallel", "parallel", "arbitrary"),
            vmem_limit_bytes=VMEM_LIMIT_BYTES_V7X),
        name="attn",
    )(*sb_args, *df_args, wo, x, final_gain)


def _rope_tables(first, length):
    f32 = np.float32
    inv = f32(1.0) / (f32(ROPE_THETA) ** (np.arange(0, HEAD_DIM, 2, dtype=f32) / f32(HEAD_DIM)))
    ang = np.arange(first, first + length, dtype=f32)[:, None] * inv[None, :]
    cos, sin = np.cos(ang).astype(f32), np.sin(ang).astype(f32)
    return (jnp.asarray(np.concatenate([cos, cos, cos, cos], axis=1)),
            jnp.asarray(np.concatenate([-sin, sin, -sin, sin], axis=1)))


def _forward(x, meta_tokens, norm_gain, w_in, w_out, lambda_q1, lambda_k1, lambda_q2, lambda_k2,
             subln_gain, final_norm_gain, *, qb, tm, tiles):
    assert qb == KEY_BLOCK and tm % KEY_BLOCK == 0
    b, s, d = x.shape
    wb = w_in[0]
    wo = w_out[0].astype(jnp.bfloat16)
    gain = norm_gain[0][None, :]
    x2d = x.reshape(b * s, d)

    (sbq, sbk, sbvt, sbg, dq, dk, dvt, dg, sbk_m, sbvt_m, dk_m, dvt_m) = _project(
        x2d, meta_tokens, gain, wb, _rope_tables(N_META, s), _rope_tables(0, N_META), tm)
    rows3 = lambda a: a.reshape(b, s, GROUP)
    blocks = lambda a: a.reshape(b, s // KEY_BLOCK, GROUP, KEY_BLOCK)

    kb = np.arange(KEY_BLOCK)
    later = (kb[None, :] > kb[:, None]).astype(np.float32)
    tri = np.concatenate([later, np.ones((SUBLANES_BF16_V7X, KEY_BLOCK), np.float32)], axis=0)
    trim = np.zeros((N_META, LANES_V7X), np.float32)
    trim[:, :N_META] = later[:N_META, :N_META]
    tri, trim = jnp.asarray(tri, jnp.bfloat16), jnp.asarray(trim, jnp.bfloat16)

    return _attention(
        (rows3(sbq), rows3(sbk), blocks(sbvt), sbk_m, sbvt_m, rows3(sbg), tri, trim),
        (rows3(dq), rows3(dk), blocks(dvt), dk_m, dvt_m, rows3(dg),
         lambda_q1, lambda_k1, lambda_q2, lambda_k2, subln_gain),
        wo, x, final_norm_gain[None, :], qb, tiles)


def kernel(x, meta_tokens, norm_gain, w_in, w_out, lambda_q1, lambda_k1, lambda_q2, lambda_k2,
           subln_gain, final_norm_gain):
    return _forward(x, meta_tokens, norm_gain, w_in, w_out, lambda_q1, lambda_k1, lambda_q2,
                    lambda_k2, subln_gain, final_norm_gain, qb=KEY_BLOCK, tm=1024, tiles=4)
```

```python
import functools

import numpy as np
import jax
import jax.numpy as jnp
from jax import lax
from jax.experimental import pallas as pl
from jax.experimental.pallas import tpu as pltpu

LANES_V7X = 128
SUBLANES_BF16_V7X = 16
VMEM_LIMIT_BYTES_V7X = 56 * 1024 * 1024

N_META = 16
CHUNK = 64
ROPE_THETA = 10000.0
RMS_EPS = 1e-6
HEAD_DIM = 64
GROUP = 512
KEY_BLOCK = 256
NEG_BIG = -1e30
LOG2_E = 1.4426950408889634
LOOKAHEAD = 1
EXP_UNDERFLOW = -104.0
LAMBDA_INIT = 0.8 - 0.6 * float(np.exp(-0.3 * 0))


def _rms(x, g):
    return x * lax.rsqrt(jnp.mean(x * x, axis=-1, keepdims=True) + RMS_EPS) * g


def _proj_kernel(x_ref, xm_ref, g_ref, w_ref, cos_ref, sin_ref, cosm_ref, sinm_ref, *out_refs):
    scale = HEAD_DIM ** -0.5

    def silu(g):
        return g / (1.0 + jnp.exp(-g))

    def projector(rows_ref, cos_ref, sin_ref):
        u = _rms(rows_ref[...], g_ref[...]).astype(jnp.bfloat16)

        def col(c):
            w = w_ref[:, c * GROUP:(c + 1) * GROUP].astype(jnp.bfloat16)
            return jnp.dot(u, w, preferred_element_type=jnp.float32)

        def col_t(c, rows):
            v = col(c)
            if rows > v.shape[0]:
                v = _pad_rows(v, rows)
            return v.astype(jnp.bfloat16).T

        def rope(t):
            half = HEAD_DIM // 2
            lane = lax.broadcasted_iota(jnp.int32, (1, LANES_V7X), 1)
            first_half = (lane // half) % 2 == 0
            cos = jnp.concatenate([cos_ref[...]] * (GROUP // LANES_V7X), axis=1)
            sin = jnp.concatenate([sin_ref[...]] * (GROUP // LANES_V7X), axis=1)
            parts = []
            for j in range(GROUP // LANES_V7X):
                tile = t[:, j * LANES_V7X:(j + 1) * LANES_V7X]
                parts.append(jnp.where(first_half, pltpu.roll(tile, LANES_V7X - half, 1),
                                       pltpu.roll(tile, half, 1)))
            return t * cos + jnp.concatenate(parts, axis=1) * sin

        return col, col_t, rope

    (sbq_ref, sbk_ref, sbvt_ref, sbg_ref, dq_ref, dk_ref, dvt_ref, dg_ref,
     sbkm_ref, sbvtm_ref, dkm_ref, dvtm_ref) = out_refs

    @pl.when(pl.program_id(0) == 0)
    def _meta():
        col, col_t, rope = projector(xm_ref, cosm_ref, sinm_ref)
        sbkm_ref[...] = col(1).astype(jnp.bfloat16)
        sbvtm_ref[...] = col_t(2, LANES_V7X)
        dkm_ref[...] = rope(col(5)).astype(jnp.bfloat16)
        dvtm_ref[...] = col_t(6, LANES_V7X)

    col, col_t, rope = projector(x_ref, cos_ref, sin_ref)
    sbq_ref[...] = (col(0) * scale).astype(jnp.bfloat16)
    sbk_ref[...] = col(1).astype(jnp.bfloat16)
    sbg_ref[...] = silu(col(3))
    dq_ref[...] = (rope(col(4)) * scale).astype(jnp.bfloat16)
    dk_ref[...] = rope(col(5)).astype(jnp.bfloat16)
    dg_ref[...] = silu(col(7))
    for c, ref in ((2, sbvt_ref), (6, dvt_ref)):
        vt = col_t(c, 0)
        for j in range(ref.shape[0]):
            ref[j] = vt[:, j * KEY_BLOCK:(j + 1) * KEY_BLOCK]


def _project(x2d, meta_tokens, gain, wb, tables, meta_tables, tm):
    rows, d = x2d.shape
    n_pos_tiles = tables[0].shape[0] // tm
    row_spec = lambda width: pl.BlockSpec((tm, width), lambda i: (i, 0))
    tab_spec = pl.BlockSpec((tm, LANES_V7X), lambda i: (i % n_pos_tiles, 0))
    whole = lambda shape: pl.BlockSpec(shape, lambda i: (0,) * len(shape))
    bf, f32 = jnp.bfloat16, jnp.float32
    row_out = lambda dt: (row_spec(GROUP), jax.ShapeDtypeStruct((rows, GROUP), dt))
    vt_out = (pl.BlockSpec((tm // KEY_BLOCK, GROUP, KEY_BLOCK), lambda i: (i, 0, 0)),
              jax.ShapeDtypeStruct((rows // KEY_BLOCK, GROUP, KEY_BLOCK), bf))
    meta_out = lambda shape: (whole(shape), jax.ShapeDtypeStruct(shape, bf))
    outs = [row_out(bf), row_out(bf), vt_out, row_out(f32),
            row_out(bf), row_out(bf), vt_out, row_out(f32),
            meta_out((N_META, GROUP)), meta_out((GROUP, LANES_V7X)),
            meta_out((N_META, GROUP)), meta_out((GROUP, LANES_V7X))]
    return pl.pallas_call(
        _proj_kernel,
        grid=(rows // tm,),
        in_specs=[row_spec(d), whole(meta_tokens.shape), whole((1, d)),
                  pl.BlockSpec(wb.shape, lambda i: (0, 0), pipeline_mode=pl.Buffered(1)),
                  tab_spec, tab_spec,
                  whole(meta_tables[0].shape), whole(meta_tables[1].shape)],
        out_specs=[o[0] for o in outs],
        out_shape=[o[1] for o in outs],
        compiler_params=pltpu.CompilerParams(
            dimension_semantics=("arbitrary",), vmem_limit_bytes=VMEM_LIMIT_BYTES_V7X),
        name="proj",
    )(x2d, meta_tokens, gain, wb, *tables, *meta_tables)


_NT = (((1,), (1,)), ((), ()))


def _masked_queries(q_ref, p, keep):
    q = q_ref[0, :, p * LANES_V7X:(p + 1) * LANES_V7X].astype(jnp.float32)
    return jnp.where(keep, q, 0.0).astype(jnp.bfloat16)


def _pad_rows(a, rows):
    return jnp.concatenate([a, jnp.zeros((rows - a.shape[0], a.shape[1]), a.dtype)], axis=0)


class _Phases:
    pass


def _sb_program(q_ref, k_ref, vt_ref, km_ref, vmt_ref, g_ref, tri_ref, trim_ref, o_ref,
                acc_ref, car_ref, z_ref, *, qb, tiles):
    i = pl.program_id(2)
    lane = lax.broadcasted_iota(jnp.int32, (1, LANES_V7X), 1)
    key = lax.broadcasted_iota(jnp.int32, (KEY_BLOCK, qb), 0)
    qry = lax.broadcasted_iota(jnp.int32, (KEY_BLOCK, qb), 1)
    strict = key < qry
    chains = [(p, hh) for p in range(tiles) for hh in range(2)]
    qms = [_masked_queries(q_ref, p, (lane // HEAD_DIM) == hh) for p, hh in chains]

    def log_sigmoids(z, mask):
        log_beta = jnp.minimum(z, 0.0) - jnp.log(1.0 + jnp.exp2(jnp.abs(z) * (-LOG2_E)))
        log_keep = log_beta - z
        if mask is not None:
            log_keep = jnp.where(mask, log_keep, 0.0)
        return log_beta, log_keep.astype(jnp.bfloat16)

    def scores(n, p, start):
        kblk = k_ref[0, pl.ds(start, KEY_BLOCK), p * LANES_V7X:(p + 1) * LANES_V7X]
        z_ref[n] = lax.dot_general(kblk, qms[n], _NT, preferred_element_type=jnp.float32)

    half = KEY_BLOCK // 2
    strict_half = strict[:half, :half]

    def split(a, diagonal):
        return [a[:half, :half], a[:half, half:], a[half:, half:]] if diagonal else [a]

    def join(parts, diagonal):
        if not diagonal:
            return parts[0]
        older = jnp.concatenate(parts[:2], axis=1)
        newer = jnp.concatenate([jnp.zeros_like(parts[2]), parts[2]], axis=1)
        return jnp.concatenate([older, newer], axis=0)

    def stages(this_start, next_start, vtblk, diagonal, exists=None):
        masks = [strict_half, None, strict_half] if diagonal else [None]
        log_betas, sums = {}, {}

        def stage1(n, p):
            if this_start is not None:
                scores(n, p, this_start)
            parts = [log_sigmoids(z, m) for z, m in zip(split(z_ref[n], diagonal), masks)]
            log_betas[n] = [lb for lb, _ in parts]
            log_keep = join([lk for _, lk in parts], diagonal)
            sums[n] = jnp.dot(tri_ref[...], log_keep, preferred_element_type=jnp.float32)
            if next_start is not None:
                scores(n, p, next_start)

        def stage2(n, p, hh):
            ws = []
            for lb, st, m in zip(log_betas.pop(n), split(sums[n][:KEY_BLOCK], diagonal), masks):
                w = jnp.exp(lb + st)
                ws.append((w if m is None else jnp.where(m, w, 0.0)).astype(jnp.bfloat16))
            tot = sums.pop(n)[KEY_BLOCK:KEY_BLOCK + 1]
            pv = jnp.dot(vtblk(p, hh), join(ws, diagonal), preferred_element_type=jnp.float32)
            if diagonal:
                acc_ref[n] = pv
                car_ref[n] = tot
            else:
                factor = jnp.exp(car_ref[n])
                if exists is not None:
                    factor, tot = factor * exists, tot * exists
                acc_ref[n] += pv * factor
                car_ref[n] += tot

        return ([functools.partial(stage1, n, p) for n, (p, hh) in enumerate(chains)],
                [functools.partial(stage2, n, p, hh) for n, (p, hh) in enumerate(chains)])

    def vt_at(j):
        return lambda p, hh: vt_ref[0, j, pl.ds(p * LANES_V7X + hh * HEAD_DIM, HEAD_DIM), :]

    def block_start(j):
        return pl.multiple_of(jnp.maximum(j, 0) * KEY_BLOCK, KEY_BLOCK)

    def prologue():
        for n, (p, hh) in enumerate(chains):
            scores(n, p, block_start(i))

    def older_blocks():
        def largest_carry():
            return jnp.max(functools.reduce(jnp.maximum,
                                            [car_ref[n] for n in range(len(chains))]))

        def live(state):
            t, carry_max = state
            return jnp.logical_and(t < i, carry_max > EXP_UNDERFLOW)

        def body(state):
            t, _ = state
            j = i - 1 - t
            first, second = stages(block_start(j), None, vt_at(j), False)
            for f in first + second:
                f()
            return t + 1, largest_carry()

        _, carry_max = lax.while_loop(live, body, (jnp.int32(1), largest_carry()))

        @pl.when(carry_max > EXP_UNDERFLOW)
        def _meta():
            zs = [lax.dot_general(km_ref[:, p * LANES_V7X:(p + 1) * LANES_V7X], qms[n], _NT,
                                  preferred_element_type=jnp.float32)
                  for n, (p, hh) in enumerate(chains)]
            log_betas, sticks = [], []
            for z in zs:
                log_beta, log_keep = log_sigmoids(z, None)
                sticks.append(jnp.dot(trim_ref[...], _pad_rows(log_keep, LANES_V7X),
                                      preferred_element_type=jnp.float32))
                log_betas.append(log_beta)
            ws = [_pad_rows(jnp.exp(log_betas[n] + sticks[n] + car_ref[n]).astype(jnp.bfloat16),
                            LANES_V7X) for n in range(len(chains))]
            for n, (p, hh) in enumerate(chains):
                acc_ref[n] += jnp.dot(vmt_ref[pl.ds(p * LANES_V7X + hh * HEAD_DIM, HEAD_DIM), :],
                                      ws[n], preferred_element_type=jnp.float32)

    def finalize():
        for p in range(tiles):
            o = jnp.concatenate([acc_ref[2 * p], acc_ref[2 * p + 1]], axis=0).T
            o_ref[0, :, p * LANES_V7X:(p + 1) * LANES_V7X] = (
                o * g_ref[0, :, p * LANES_V7X:(p + 1) * LANES_V7X]).astype(o_ref.dtype)

    ph = _Phases()
    ph.prologue = prologue
    ph.diagonal = stages(None, block_start(i - 1), vt_at(i), True)
    ph.older = stages(None, None, vt_at(jnp.maximum(i - 1, 0)), False,
                      exists=jnp.where(i > 0, 1.0, 0.0).astype(jnp.float32))
    ph.older_blocks = older_blocks
    ph.finalize = finalize
    return ph


def _diff_program(q_ref, k_ref, vt_ref, km_ref, vmt_ref, g_ref,
                  lq1_ref, lk1_ref, lq2_ref, lk2_ref, sub_ref, o_ref, acc_ref, m_ref, l_ref, z_ref,
                  *, qb, tiles):
    i = pl.program_id(2)
    lane = lax.broadcasted_iota(jnp.int32, (1, LANES_V7X), 1)
    key = lax.broadcasted_iota(jnp.int32, (KEY_BLOCK, qb), 0)
    qry = lax.broadcasted_iota(jnp.int32, (KEY_BLOCK, qb), 1)
    chunk_ok = (qry // CHUNK) >= (key // CHUNK)
    chains = [(p, c) for p in range(tiles) for c in range(2)]
    qms = [_masked_queries(q_ref, p, (lane // HEAD_DIM) == c) for p, c in chains]
    ones = jnp.ones((SUBLANES_BF16_V7X, KEY_BLOCK), jnp.bfloat16)

    def update(n, s, vt_rows, pad_to, first, exists=None):
        if first:
            m_new, pr = diagonal_probabilities(s)
        else:
            m_new = jnp.maximum(m_ref[n], jnp.max(s, axis=0, keepdims=True))
            pr = jnp.exp(s - m_new).astype(jnp.bfloat16)
        if pad_to is not None:
            pr = _pad_rows(pr, pad_to)
        pv = jnp.dot(vt_rows, pr, preferred_element_type=jnp.float32)
        num, den = pv[:LANES_V7X], pv[LANES_V7X:LANES_V7X + 1]
        if exists is not None:
            num, den = num * exists, den * exists
        if first:
            l_ref[n] = den
            acc_ref[n] = num
        else:
            alpha = jnp.exp(m_ref[n] - m_new)
            l_ref[n] = l_ref[n] * alpha + den
            acc_ref[n] = acc_ref[n] * alpha + num
        m_ref[n] = m_new

    def diagonal_probabilities(z):
        half = KEY_BLOCK // 2
        chunk_half = chunk_ok[:half, :half]
        older_early = jnp.where(chunk_half, z[:half, :half], NEG_BIG)
        older_late = z[:half, half:]
        newer_late = jnp.where(chunk_half, z[half:, half:], NEG_BIG)
        m_early = jnp.max(older_early, axis=0, keepdims=True)
        m_late = jnp.maximum(jnp.max(older_late, axis=0, keepdims=True),
                             jnp.max(newer_late, axis=0, keepdims=True))
        pe = lambda part, m: jnp.exp(part - m).astype(jnp.bfloat16)
        newer = jnp.concatenate([jnp.zeros((half, half), jnp.bfloat16), pe(newer_late, m_late)],
                                axis=1)
        older = jnp.concatenate([pe(older_early, m_early), pe(older_late, m_late)], axis=1)
        return (jnp.concatenate([m_early, m_late], axis=1),
                jnp.concatenate([older, newer], axis=0))

    def scores(n, p, start):
        kblk = k_ref[0, pl.ds(start, KEY_BLOCK), p * LANES_V7X:(p + 1) * LANES_V7X]
        z_ref[n] = lax.dot_general(kblk, qms[n], _NT, preferred_element_type=jnp.float32)

    def block_start(j):
        return pl.multiple_of(j * KEY_BLOCK, KEY_BLOCK)

    def block(j, next_start, first, exists=None):
        ahead = {}

        def next_scores(n):
            p = chains[n][0]
            kblk = k_ref[0, pl.ds(next_start, KEY_BLOCK), p * LANES_V7X:(p + 1) * LANES_V7X]
            ahead[n] = lax.dot_general(kblk, qms[n], _NT, preferred_element_type=jnp.float32)

        def chain(n, p):
            if next_start is not None:
                for m in (range(LOOKAHEAD) if n == 0 else ()):
                    next_scores(m)
                if n + LOOKAHEAD < len(chains):
                    next_scores(n + LOOKAHEAD)
            s = z_ref[n]
            vt_rows = jnp.concatenate([vt_ref[0, j, p * LANES_V7X:(p + 1) * LANES_V7X, :], ones],
                                      axis=0)
            update(n, s, vt_rows, None, first, exists)
            if next_start is not None:
                z_ref[n] = ahead.pop(n)

        return [functools.partial(chain, n, p) for n, (p, c) in enumerate(chains)]

    meta_scores = {}

    def diagonal_scores(n, p):
        scores(n, p, pl.multiple_of(i * qb, qb))

    def score_meta():
        for n, (p, c) in enumerate(chains):
            meta_scores[n] = lax.dot_general(km_ref[:, p * LANES_V7X:(p + 1) * LANES_V7X], qms[n],
                                             _NT, preferred_element_type=jnp.float32)

    def meta(n, p):
        vt_rows = jnp.concatenate([vmt_ref[p * LANES_V7X:(p + 1) * LANES_V7X, :],
                                   ones[:, :LANES_V7X]], axis=0)
        update(n, meta_scores.pop(n), vt_rows, LANES_V7X, False)

    def older_blocks():
        def body(t, c):
            for f in (block(2 * t, block_start(2 * t + 1), False)
                      + block(2 * t + 1, block_start(2 * t + 2), False)):
                f()
            return c

        prefetching = jnp.maximum(i - 1, 0)
        lax.fori_loop(0, prefetching // 2, body, 0)

        @pl.when(prefetching % 2 == 1)
        def _odd():
            for f in block(i - 2, block_start(i - 1), False):
                f()


    def finalize():
        lam = (jnp.exp(jnp.sum(lq1_ref[...] * lk1_ref[...], axis=-1, keepdims=True))
               - jnp.exp(jnp.sum(lq2_ref[...] * lk2_ref[...], axis=-1, keepdims=True))
               + LAMBDA_INIT)
        for p in range(tiles):
            ot = (acc_ref[2 * p] * (1.0 / l_ref[2 * p])
                  - acc_ref[2 * p + 1] * (lam / l_ref[2 * p + 1]))
            ot = ot * lax.rsqrt(jnp.mean(ot * ot, axis=0, keepdims=True) + RMS_EPS)
            o = ot.T * (sub_ref[...] * (1.0 - LAMBDA_INIT))
            o_ref[0, :, p * LANES_V7X:(p + 1) * LANES_V7X] = (
                o * g_ref[0, :, p * LANES_V7X:(p + 1) * LANES_V7X]).astype(o_ref.dtype)

    ph = _Phases()
    ph.diagonal_scores = [functools.partial(diagonal_scores, n, p)
                          for n, (p, c) in enumerate(chains)]
    ph.score_meta = score_meta
    ph.diagonal = block(i, block_start(0), True)
    ph.meta = [functools.partial(meta, n, p) for n, (p, c) in enumerate(chains)]
    ph.older_blocks = older_blocks
    ph.last = block(jnp.maximum(i - 1, 0), None, False,
                    exists=jnp.where(i > 0, 1.0, 0.0).astype(jnp.float32))
    ph.finalize = finalize
    return ph


N_SB_IN, N_DF_IN = 8, 11


def _attn_kernel(*refs, qb, tiles):
    sb_in, df_in = refs[:N_SB_IN], refs[N_SB_IN:N_SB_IN + N_DF_IN]
    wo_ref, x_ref, fg_ref, y_ref = refs[N_SB_IN + N_DF_IN:N_SB_IN + N_DF_IN + 4]
    scratch = refs[N_SB_IN + N_DF_IN + 4:]
    mix_sb, mix_df = scratch[7:]
    sb = _sb_program(*sb_in, mix_sb, *scratch[:3], qb=qb, tiles=tiles)
    df = _diff_program(*df_in, mix_df, *scratch[3:7], qb=qb, tiles=tiles)
    sb.prologue()
    for first in sb.diagonal[0]:
        first()
    for second, other in zip(sb.diagonal[1], df.diagonal_scores):
        second()
        other()
    df.score_meta()
    for first, other in zip(sb.older[0], df.diagonal):
        first()
        other()
    for second, other in zip(sb.older[1], df.meta):
        second()
        other()
    sb.older_blocks()
    df.older_blocks()
    for other in df.last:
        other()
    df.finalize()
    sb.finalize()
    half = mix_sb.shape[-1]
    h = (x_ref[0]
         + jnp.dot(mix_sb[0], wo_ref[:half, :], preferred_element_type=jnp.float32)
         + jnp.dot(mix_df[0], wo_ref[half:, :], preferred_element_type=jnp.float32))
    y_ref[0] = _rms(h, fg_ref[...])


def _attention(sb_args, df_args, wo, x, final_gain, qb, tiles):
    q = sb_args[0]
    b, s, width = q.shape
    tw = tiles * LANES_V7X
    assert tw == width, "the fused kernel handles every head of a batch element per grid step"
    qspec = pl.BlockSpec((1, qb, tw), lambda bi, p, i: (bi, i, p))
    kspec = pl.BlockSpec((1, s, tw), lambda bi, p, i: (bi, 0, p))
    vtspec = pl.BlockSpec((1, s // KEY_BLOCK, tw, KEY_BLOCK), lambda bi, p, i: (bi, 0, p, 0))
    kmspec = pl.BlockSpec((N_META, tw), lambda bi, p, i: (0, p))
    vmtspec = pl.BlockSpec((tw, LANES_V7X), lambda bi, p, i: (p, 0))
    full = lambda a: pl.BlockSpec(a.shape, lambda bi, p, i: (0,) * a.ndim)
    common = [qspec, kspec, vtspec, kmspec, vmtspec, qspec]
    xspec = pl.BlockSpec((1, qb, x.shape[-1]), lambda bi, p, i: (bi, i, 0))
    in_specs = (common + [full(a) for a in sb_args[6:]] + common + [full(a) for a in df_args[6:]]
                + [full(wo), xspec, full(final_gain)])
    assert len(sb_args) == N_SB_IN and len(df_args) == N_DF_IN
    stat = pltpu.VMEM((2 * tiles, 1, qb), jnp.float32)
    score = pltpu.VMEM((2 * tiles, KEY_BLOCK, qb), jnp.float32)
    mix = pltpu.VMEM((1, qb, width), jnp.bfloat16)
    return pl.pallas_call(
        functools.partial(_attn_kernel, qb=qb, tiles=tiles),
        grid=(b, width // tw, s // qb),
        in_specs=in_specs,
        out_specs=xspec,
        out_shape=jax.ShapeDtypeStruct(x.shape, jnp.float32),
        scratch_shapes=[pltpu.VMEM((2 * tiles, HEAD_DIM, qb), jnp.float32), stat, score,
                        pltpu.VMEM((2 * tiles, LANES_V7X, qb), jnp.float32), stat, stat, score,
                        mix, mix],
        compiler_params=pltpu.CompilerParams(
            dimension_semantics=("parallel", "parallel", "arbitrary"),
            vmem_limit_bytes=VMEM_LIMIT_BYTES_V7X),
        name="attn",
    )(*sb_args, *df_args, wo, x, final_gain)


def _rope_tables(first, length):
    f32 = np.float32
    inv = f32(1.0) / (f32(ROPE_THETA) ** (np.arange(0, HEAD_DIM, 2, dtype=f32) / f32(HEAD_DIM)))
    ang = np.arange(first, first + length, dtype=f32)[:, None] * inv[None, :]
    cos, sin = np.cos(ang).astype(f32), np.sin(ang).astype(f32)
    return (jnp.asarray(np.concatenate([cos, cos, cos, cos], axis=1)),
            jnp.asarray(np.concatenate([-sin, sin, -sin, sin], axis=1)))


def _forward(x, meta_tokens, norm_gain, w_in, w_out, lambda_q1, lambda_k1, lambda_q2, lambda_k2,
             subln_gain, final_norm_gain, *, qb, tm, tiles):
    assert qb == KEY_BLOCK and tm % KEY_BLOCK == 0
    b, s, d = x.shape
    wb = w_in[0]
    wo = w_out[0].astype(jnp.bfloat16)
    gain = norm_gain[0][None, :]
    x2d = x.reshape(b * s, d)

    (sbq, sbk, sbvt, sbg, dq, dk, dvt, dg, sbk_m, sbvt_m, dk_m, dvt_m) = _project(
        x2d, meta_tokens, gain, wb, _rope_tables(N_META, s), _rope_tables(0, N_META), tm)
    rows3 = lambda a: a.reshape(b, s, GROUP)
    blocks = lambda a: a.reshape(b, s // KEY_BLOCK, GROUP, KEY_BLOCK)

    kb = np.arange(KEY_BLOCK)
    later = (kb[None, :] > kb[:, None]).astype(np.float32)
    tri = np.concatenate([later, np.ones((SUBLANES_BF16_V7X, KEY_BLOCK), np.float32)], axis=0)
    trim = np.zeros((N_META, LANES_V7X), np.float32)
    trim[:, :N_META] = later[:N_META, :N_META]
    tri, trim = jnp.asarray(tri, jnp.bfloat16), jnp.asarray(trim, jnp.bfloat16)

    return _attention(
        (rows3(sbq), rows3(sbk), blocks(sbvt), sbk_m, sbvt_m, rows3(sbg), tri, trim),
        (rows3(dq), rows3(dk), blocks(dvt), dk_m, dvt_m, rows3(dg),
         lambda_q1, lambda_k1, lambda_q2, lambda_k2, subln_gain),
        wo, x, final_norm_gain[None, :], qb, tiles)


def kernel(x, meta_tokens, norm_gain, w_in, w_out, lambda_q1, lambda_k1, lambda_q2, lambda_k2,
           subln_gain, final_norm_gain):
    return _forward(x, meta_tokens, norm_gain, w_in, w_out, lambda_q1, lambda_k1, lambda_q2,
                    lambda_k2, subln_gain, final_norm_gain, qb=KEY_BLOCK, tm=1024, tiles=4)
```

```python
import functools

import numpy as np
import jax
import jax.numpy as jnp
from jax import lax
from jax.experimental import pallas as pl
from jax.experimental.pallas import tpu as pltpu

LANES_V7X = 128
SUBLANES_BF16_V7X = 16
VMEM_LIMIT_BYTES_V7X = 56 * 1024 * 1024
PROJ_ROW_TILES = (1024, 512, 256)

N_META = 16
CHUNK = 64
ROPE_THETA = 10000.0
RMS_EPS = 1e-6
HEAD_DIM = 64
GROUP = 512
KEY_BLOCK = 256
NEG_BIG = -1e30
LOG2_E = 1.4426950408889634
LOOKAHEAD = 1
EXP_UNDERFLOW = -104.0
LAMBDA_INIT = 0.8 - 0.6 * float(np.exp(-0.3 * 0))


def _rms(x, g):
    return x * lax.rsqrt(jnp.mean(x * x, axis=-1, keepdims=True) + RMS_EPS) * g


def _proj_kernel(x_ref, xm_ref, g_ref, w_ref, cos_ref, sin_ref, cosm_ref, sinm_ref, *out_refs):
    scale = HEAD_DIM ** -0.5

    def silu(g):
        return g / (1.0 + jnp.exp(-g))

    def projector(rows_ref, cos_ref, sin_ref):
        u = _rms(rows_ref[...], g_ref[...]).astype(jnp.bfloat16)

        def col(c):
            w = w_ref[:, c * GROUP:(c + 1) * GROUP].astype(jnp.bfloat16)
            return jnp.dot(u, w, preferred_element_type=jnp.float32)

        def col_t(c, rows):
            v = col(c)
            if rows > v.shape[0]:
                v = _pad_rows(v, rows)
            return v.astype(jnp.bfloat16).T

        def rope(t):
            half = HEAD_DIM // 2
            lane = lax.broadcasted_iota(jnp.int32, (1, LANES_V7X), 1)
            first_half = (lane // half) % 2 == 0
            cos = jnp.concatenate([cos_ref[...]] * (GROUP // LANES_V7X), axis=1)
            sin = jnp.concatenate([sin_ref[...]] * (GROUP // LANES_V7X), axis=1)
            parts = []
            for j in range(GROUP // LANES_V7X):
                tile = t[:, j * LANES_V7X:(j + 1) * LANES_V7X]
                parts.append(jnp.where(first_half, pltpu.roll(tile, LANES_V7X - half, 1),
                                       pltpu.roll(tile, half, 1)))
            return t * cos + jnp.concatenate(parts, axis=1) * sin

        return col, col_t, rope

    (sbq_ref, sbk_ref, sbvt_ref, sbg_ref, dq_ref, dk_ref, dvt_ref, dg_ref,
     sbkm_ref, sbvtm_ref, dkm_ref, dvtm_ref) = out_refs

    @pl.when(pl.program_id(0) == 0)
    def _meta():
        col, col_t, rope = projector(xm_ref, cosm_ref, sinm_ref)
        sbkm_ref[...] = col(1).astype(jnp.bfloat16)
        sbvtm_ref[...] = col_t(2, LANES_V7X)
        dkm_ref[...] = rope(col(5)).astype(jnp.bfloat16)
        dvtm_ref[...] = col_t(6, LANES_V7X)

    col, col_t, rope = projector(x_ref, cos_ref, sin_ref)
    sbq_ref[...] = (col(0) * scale).astype(jnp.bfloat16)
    sbk_ref[...] = col(1).astype(jnp.bfloat16)
    sbg_ref[...] = silu(col(3))
    dq_ref[...] = (rope(col(4)) * scale).astype(jnp.bfloat16)
    dk_ref[...] = rope(col(5)).astype(jnp.bfloat16)
    dg_ref[...] = silu(col(7))
    for c, ref in ((2, sbvt_ref), (6, dvt_ref)):
        vt = col_t(c, 0)
        for j in range(ref.shape[0]):
            ref[j] = vt[:, j * KEY_BLOCK:(j + 1) * KEY_BLOCK]


def _project(x2d, meta_tokens, gain, wb, tables, meta_tables, tm):
    rows, d = x2d.shape
    n_pos_tiles = tables[0].shape[0] // tm
    row_spec = lambda width: pl.BlockSpec((tm, width), lambda i: (i, 0))
    tab_spec = pl.BlockSpec((tm, LANES_V7X), lambda i: (i % n_pos_tiles, 0))
    whole = lambda shape: pl.BlockSpec(shape, lambda i: (0,) * len(shape))
    bf, f32 = jnp.bfloat16, jnp.float32
    row_out = lambda dt: (row_spec(GROUP), jax.ShapeDtypeStruct((rows, GROUP), dt))
    vt_out = (pl.BlockSpec((tm // KEY_BLOCK, GROUP, KEY_BLOCK), lambda i: (i, 0, 0)),
              jax.ShapeDtypeStruct((rows // KEY_BLOCK, GROUP, KEY_BLOCK), bf))
    meta_out = lambda shape: (whole(shape), jax.ShapeDtypeStruct(shape, bf))
    outs = [row_out(bf), row_out(bf), vt_out, row_out(f32),
            row_out(bf), row_out(bf), vt_out, row_out(f32),
            meta_out((N_META, GROUP)), meta_out((GROUP, LANES_V7X)),
            meta_out((N_META, GROUP)), meta_out((GROUP, LANES_V7X))]
    return pl.pallas_call(
        _proj_kernel,
        grid=(rows // tm,),
        in_specs=[row_spec(d), whole(meta_tokens.shape), whole((1, d)),
                  pl.BlockSpec(wb.shape, lambda i: (0, 0), pipeline_mode=pl.Buffered(1)),
                  tab_spec, tab_spec,
                  whole(meta_tables[0].shape), whole(meta_tables[1].shape)],
        out_specs=[o[0] for o in outs],
        out_shape=[o[1] for o in outs],
        compiler_params=pltpu.CompilerParams(
            dimension_semantics=("arbitrary",), vmem_limit_bytes=VMEM_LIMIT_BYTES_V7X),
        name="proj",
    )(x2d, meta_tokens, gain, wb, *tables, *meta_tables)


_NT = (((1,), (1,)), ((), ()))


def _masked_queries(q_ref, p, keep):
    q = q_ref[0, :, p * LANES_V7X:(p + 1) * LANES_V7X].astype(jnp.float32)
    return jnp.where(keep, q, 0.0).astype(jnp.bfloat16)


def _pad_rows(a, rows):
    return jnp.concatenate([a, jnp.zeros((rows - a.shape[0], a.shape[1]), a.dtype)], axis=0)


class _Phases:
    pass


def _sb_program(q_ref, k_ref, vt_ref, km_ref, vmt_ref, g_ref, tri_ref, trim_ref, o_ref,
                acc_ref, car_ref, z_ref, *, qb, tiles):
    i = pl.program_id(2)
    lane = lax.broadcasted_iota(jnp.int32, (1, LANES_V7X), 1)
    key = lax.broadcasted_iota(jnp.int32, (KEY_BLOCK, qb), 0)
    qry = lax.broadcasted_iota(jnp.int32, (KEY_BLOCK, qb), 1)
    strict = key < qry
    chains = [(p, hh) for p in range(tiles) for hh in range(2)]
    qms = [_masked_queries(q_ref, p, (lane // HEAD_DIM) == hh) for p, hh in chains]

    def log_sigmoids(z, mask):
        log_beta = jnp.minimum(z, 0.0) - jnp.log(1.0 + jnp.exp2(jnp.abs(z) * (-LOG2_E)))
        log_keep = log_beta - z
        if mask is not None:
            log_keep = jnp.where(mask, log_keep, 0.0)
        return log_beta, log_keep.astype(jnp.bfloat16)

    def scores(n, p, start):
        kblk = k_ref[0, pl.ds(start, KEY_BLOCK), p * LANES_V7X:(p + 1) * LANES_V7X]
        z_ref[n] = lax.dot_general(kblk, qms[n], _NT, preferred_element_type=jnp.float32)

    half = KEY_BLOCK // 2
    strict_half = strict[:half, :half]

    def split(a, diagonal):
        return [a[:half, :half], a[:half, half:], a[half:, half:]] if diagonal else [a]

    def join(parts, diagonal):
        if not diagonal:
            return parts[0]
        older = jnp.concatenate(parts[:2], axis=1)
        newer = jnp.concatenate([jnp.zeros_like(parts[2]), parts[2]], axis=1)
        return jnp.concatenate([older, newer], axis=0)

    def stages(this_start, next_start, vtblk, diagonal, exists=None):
        masks = [strict_half, None, strict_half] if diagonal else [None]
        log_betas, sums = {}, {}

        def stage1(n, p):
            if this_start is not None:
                scores(n, p, this_start)
            parts = [log_sigmoids(z, m) for z, m in zip(split(z_ref[n], diagonal), masks)]
            log_betas[n] = [lb for lb, _ in parts]
            log_keep = join([lk for _, lk in parts], diagonal)
            sums[n] = jnp.dot(tri_ref[...], log_keep, preferred_element_type=jnp.float32)
            if next_start is not None:
                scores(n, p, next_start)

        def stage2(n, p, hh):
            ws = []
            for lb, st, m in zip(log_betas.pop(n), split(sums[n][:KEY_BLOCK], diagonal), masks):
                w = jnp.exp(lb + st)
                ws.append((w if m is None else jnp.where(m, w, 0.0)).astype(jnp.bfloat16))
            tot = sums.pop(n)[KEY_BLOCK:KEY_BLOCK + 1]
            pv = jnp.dot(vtblk(p, hh), join(ws, diagonal), preferred_element_type=jnp.float32)
            if diagonal:
                acc_ref[n] = pv
                car_ref[n] = tot
            else:
                factor = jnp.exp(car_ref[n])
                if exists is not None:
                    factor, tot = factor * exists, tot * exists
                acc_ref[n] += pv * factor
                car_ref[n] += tot

        return ([functools.partial(stage1, n, p) for n, (p, hh) in enumerate(chains)],
                [functools.partial(stage2, n, p, hh) for n, (p, hh) in enumerate(chains)])

    def vt_at(j):
        return lambda p, hh: vt_ref[0, j, pl.ds(p * LANES_V7X + hh * HEAD_DIM, HEAD_DIM), :]

    def block_start(j):
        return pl.multiple_of(jnp.maximum(j, 0) * KEY_BLOCK, KEY_BLOCK)

    def prologue():
        for n, (p, hh) in enumerate(chains):
            scores(n, p, block_start(i))

    def older_blocks():
        def largest_carry():
            return jnp.max(functools.reduce(jnp.maximum,
                                            [car_ref[n] for n in range(len(chains))]))

        def live(state):
            t, carry_max = state
            return jnp.logical_and(t < i, carry_max > EXP_UNDERFLOW)

        def body(state):
            t, _ = state
            j = i - 1 - t
            first, second = stages(block_start(j), None, vt_at(j), False)
            for f in first + second:
                f()
            return t + 1, largest_carry()

        _, carry_max = lax.while_loop(live, body, (jnp.int32(1), largest_carry()))

        @pl.when(carry_max > EXP_UNDERFLOW)
        def _meta():
            zs = [lax.dot_general(km_ref[:, p * LANES_V7X:(p + 1) * LANES_V7X], qms[n], _NT,
                                  preferred_element_type=jnp.float32)
                  for n, (p, hh) in enumerate(chains)]
            log_betas, sticks = [], []
            for z in zs:
                log_beta, log_keep = log_sigmoids(z, None)
                sticks.append(jnp.dot(trim_ref[...], _pad_rows(log_keep, LANES_V7X),
                                      preferred_element_type=jnp.float32))
                log_betas.append(log_beta)
            ws = [_pad_rows(jnp.exp(log_betas[n] + sticks[n] + car_ref[n]).astype(jnp.bfloat16),
                            LANES_V7X) for n in range(len(chains))]
            for n, (p, hh) in enumerate(chains):
                acc_ref[n] += jnp.dot(vmt_ref[pl.ds(p * LANES_V7X + hh * HEAD_DIM, HEAD_DIM), :],
                                      ws[n], preferred_element_type=jnp.float32)

    def finalize():
        for p in range(tiles):
            o = jnp.concatenate([acc_ref[2 * p], acc_ref[2 * p + 1]], axis=0).T
            o_ref[0, :, p * LANES_V7X:(p + 1) * LANES_V7X] = (
                o * g_ref[0, :, p * LANES_V7X:(p + 1) * LANES_V7X]).astype(o_ref.dtype)

    ph = _Phases()
    ph.prologue = prologue
    ph.diagonal = stages(None, block_start(i - 1), vt_at(i), True)
    ph.older = stages(None, None, vt_at(jnp.maximum(i - 1, 0)), False,
                      exists=jnp.where(i > 0, 1.0, 0.0).astype(jnp.float32))
    ph.older_blocks = older_blocks
    ph.finalize = finalize
    return ph


def _diff_program(q_ref, k_ref, vt_ref, km_ref, vmt_ref, g_ref,
                  lq1_ref, lk1_ref, lq2_ref, lk2_ref, sub_ref, o_ref, acc_ref, m_ref, l_ref, z_ref,
                  *, qb, tiles):
    i = pl.program_id(2)
    lane = lax.broadcasted_iota(jnp.int32, (1, LANES_V7X), 1)
    key = lax.broadcasted_iota(jnp.int32, (KEY_BLOCK, qb), 0)
    qry = lax.broadcasted_iota(jnp.int32, (KEY_BLOCK, qb), 1)
    chunk_ok = (qry // CHUNK) >= (key // CHUNK)
    chains = [(p, c) for p in range(tiles) for c in range(2)]
    qms = [_masked_queries(q_ref, p, (lane // HEAD_DIM) == c) for p, c in chains]
    ones = jnp.ones((SUBLANES_BF16_V7X, KEY_BLOCK), jnp.bfloat16)

    def update(n, s, vt_rows, pad_to, first, exists=None):
        if first:
            m_new, pr = diagonal_probabilities(s)
        else:
            m_new = jnp.maximum(m_ref[n], jnp.max(s, axis=0, keepdims=True))
            pr = jnp.exp(s - m_new).astype(jnp.bfloat16)
        if pad_to is not None:
            pr = _pad_rows(pr, pad_to)
        pv = jnp.dot(vt_rows, pr, preferred_element_type=jnp.float32)
        num, den = pv[:LANES_V7X], pv[LANES_V7X:LANES_V7X + 1]
        if exists is not None:
            num, den = num * exists, den * exists
        if first:
            l_ref[n] = den
            acc_ref[n] = num
        else:
            alpha = jnp.exp(m_ref[n] - m_new)
            l_ref[n] = l_ref[n] * alpha + den
            acc_ref[n] = acc_ref[n] * alpha + num
        m_ref[n] = m_new

    def diagonal_probabilities(z):
        half = KEY_BLOCK // 2
        chunk_half = chunk_ok[:half, :half]
        older_early = jnp.where(chunk_half, z[:half, :half], NEG_BIG)
        older_late = z[:half, half:]
        newer_late = jnp.where(chunk_half, z[half:, half:], NEG_BIG)
        m_early = jnp.max(older_early, axis=0, keepdims=True)
        m_late = jnp.maximum(jnp.max(older_late, axis=0, keepdims=True),
                             jnp.max(newer_late, axis=0, keepdims=True))
        pe = lambda part, m: jnp.exp(part - m).astype(jnp.bfloat16)
        newer = jnp.concatenate([jnp.zeros((half, half), jnp.bfloat16), pe(newer_late, m_late)],
                                axis=1)
        older = jnp.concatenate([pe(older_early, m_early), pe(older_late, m_late)], axis=1)
        return (jnp.concatenate([m_early, m_late], axis=1),
                jnp.concatenate([older, newer], axis=0))

    def scores(n, p, start):
        kblk = k_ref[0, pl.ds(start, KEY_BLOCK), p * LANES_V7X:(p + 1) * LANES_V7X]
        z_ref[n] = lax.dot_general(kblk, qms[n], _NT, preferred_element_type=jnp.float32)

    def block_start(j):
        return pl.multiple_of(j * KEY_BLOCK, KEY_BLOCK)

    def block(j, next_start, first, exists=None):
        ahead = {}

        def next_scores(n):
            p = chains[n][0]
            kblk = k_ref[0, pl.ds(next_start, KEY_BLOCK), p * LANES_V7X:(p + 1) * LANES_V7X]
            ahead[n] = lax.dot_general(kblk, qms[n], _NT, preferred_element_type=jnp.float32)

        def chain(n, p):
            if next_start is not None:
                for m in (range(LOOKAHEAD) if n == 0 else ()):
                    next_scores(m)
                if n + LOOKAHEAD < len(chains):
                    next_scores(n + LOOKAHEAD)
            s = z_ref[n]
            vt_rows = jnp.concatenate([vt_ref[0, j, p * LANES_V7X:(p + 1) * LANES_V7X, :], ones],
                                      axis=0)
            update(n, s, vt_rows, None, first, exists)
            if next_start is not None:
                z_ref[n] = ahead.pop(n)

        return [functools.partial(chain, n, p) for n, (p, c) in enumerate(chains)]

    meta_scores = {}

    def diagonal_scores(n, p):
        scores(n, p, pl.multiple_of(i * qb, qb))

    def score_meta():
        for n, (p, c) in enumerate(chains):
            meta_scores[n] = lax.dot_general(km_ref[:, p * LANES_V7X:(p + 1) * LANES_V7X], qms[n],
                                             _NT, preferred_element_type=jnp.float32)

    def meta(n, p):
        vt_rows = jnp.concatenate([vmt_ref[p * LANES_V7X:(p + 1) * LANES_V7X, :],
                                   ones[:, :LANES_V7X]], axis=0)
        update(n, meta_scores.pop(n), vt_rows, LANES_V7X, False)

    def older_blocks():
        def body(t, c):
            for f in (block(2 * t, block_start(2 * t + 1), False)
                      + block(2 * t + 1, block_start(2 * t + 2), False)):
                f()
            return c

        prefetching = jnp.maximum(i - 1, 0)
        lax.fori_loop(0, prefetching // 2, body, 0)

        @pl.when(prefetching % 2 == 1)
        def _odd():
            for f in block(i - 2, block_start(i - 1), False):
                f()


    def finalize():
        lam = (jnp.exp(jnp.sum(lq1_ref[...] * lk1_ref[...], axis=-1, keepdims=True))
               - jnp.exp(jnp.sum(lq2_ref[...] * lk2_ref[...], axis=-1, keepdims=True))
               + LAMBDA_INIT)
        for p in range(tiles):
            ot = (acc_ref[2 * p] * (1.0 / l_ref[2 * p])
                  - acc_ref[2 * p + 1] * (lam / l_ref[2 * p + 1]))
            ot = ot * lax.rsqrt(jnp.mean(ot * ot, axis=0, keepdims=True) + RMS_EPS)
            o = ot.T * (sub_ref[...] * (1.0 - LAMBDA_INIT))
            o_ref[0, :, p * LANES_V7X:(p + 1) * LANES_V7X] = (
                o * g_ref[0, :, p * LANES_V7X:(p + 1) * LANES_V7X]).astype(o_ref.dtype)

    ph = _Phases()
    ph.diagonal_scores = [functools.partial(diagonal_scores, n, p)
                          for n, (p, c) in enumerate(chains)]
    ph.score_meta = score_meta
    ph.diagonal = block(i, block_start(0), True)
    ph.meta = [functools.partial(meta, n, p) for n, (p, c) in enumerate(chains)]
    ph.older_blocks = older_blocks
    ph.last = block(jnp.maximum(i - 1, 0), None, False,
                    exists=jnp.where(i > 0, 1.0, 0.0).astype(jnp.float32))
    ph.finalize = finalize
    return ph


N_SB_IN, N_DF_IN = 8, 11


def _attn_kernel(*refs, qb, tiles):
    sb_in, df_in = refs[:N_SB_IN], refs[N_SB_IN:N_SB_IN + N_DF_IN]
    wo_ref, x_ref, fg_ref, y_ref = refs[N_SB_IN + N_DF_IN:N_SB_IN + N_DF_IN + 4]
    scratch = refs[N_SB_IN + N_DF_IN + 4:]
    mix_sb, mix_df = scratch[7:]
    sb = _sb_program(*sb_in, mix_sb, *scratch[:3], qb=qb, tiles=tiles)
    df = _diff_program(*df_in, mix_df, *scratch[3:7], qb=qb, tiles=tiles)
    sb.prologue()
    for first in sb.diagonal[0]:
        first()
    for second, other in zip(sb.diagonal[1], df.diagonal_scores):
        second()
        other()
    df.score_meta()
    for first, other in zip(sb.older[0], df.diagonal):
        first()
        other()
    for second, other in zip(sb.older[1], df.meta):
        second()
        other()
    sb.older_blocks()
    df.older_blocks()
    for other in df.last:
        other()
    df.finalize()
    sb.finalize()
    half = mix_sb.shape[-1]
    h = (x_ref[0]
         + jnp.dot(mix_sb[0], wo_ref[:half, :], preferred_element_type=jnp.float32)
         + jnp.dot(mix_df[0], wo_ref[half:, :], preferred_element_type=jnp.float32))
    y_ref[0] = _rms(h, fg_ref[...])


def _attention(sb_args, df_args, wo, x, final_gain, qb, tiles):
    q = sb_args[0]
    b, s, width = q.shape
    tw = tiles * LANES_V7X
    assert tw == width, "the fused kernel handles every head of a batch element per grid step"
    qspec = pl.BlockSpec((1, qb, tw), lambda bi, p, i: (bi, i, p))
    kspec = pl.BlockSpec((1, s, tw), lambda bi, p, i: (bi, 0, p))
    vtspec = pl.BlockSpec((1, s // KEY_BLOCK, tw, KEY_BLOCK), lambda bi, p, i: (bi, 0, p, 0))
    kmspec = pl.BlockSpec((N_META, tw), lambda bi, p, i: (0, p))
    vmtspec = pl.BlockSpec((tw, LANES_V7X), lambda bi, p, i: (p, 0))
    full = lambda a: pl.BlockSpec(a.shape, lambda bi, p, i: (0,) * a.ndim)
    common = [qspec, kspec, vtspec, kmspec, vmtspec, qspec]
    xspec = pl.BlockSpec((1, qb, x.shape[-1]), lambda bi, p, i: (bi, i, 0))
    in_specs = (common + [full(a) for a in sb_args[6:]] + common + [full(a) for a in df_args[6:]]
                + [full(wo), xspec, full(final_gain)])
    assert len(sb_args) == N_SB_IN and len(df_args) == N_DF_IN
    stat = pltpu.VMEM((2 * tiles, 1, qb), jnp.float32)
    score = pltpu.VMEM((2 * tiles, KEY_BLOCK, qb), jnp.float32)
    mix = pltpu.VMEM((1, qb, width), jnp.bfloat16)
    return pl.pallas_call(
        functools.partial(_attn_kernel, qb=qb, tiles=tiles),
        grid=(b, width // tw, s // qb),
        in_specs=in_specs,
        out_specs=xspec,
        out_shape=jax.ShapeDtypeStruct(x.shape, jnp.float32),
        scratch_shapes=[pltpu.VMEM((2 * tiles, HEAD_DIM, qb), jnp.float32), stat, score,
                        pltpu.VMEM((2 * tiles, LANES_V7X, qb), jnp.float32), stat, stat, score,
                        mix, mix],
        compiler_params=pltpu.CompilerParams(
            dimension_semantics=("parallel", "parallel", "arbitrary"),
            vmem_limit_bytes=VMEM_LIMIT_BYTES_V7X),
        name="attn",
    )(*sb_args, *df_args, wo, x, final_gain)


def _rope_tables(first, length):
    f32 = np.float32
    inv = f32(1.0) / (f32(ROPE_THETA) ** (np.arange(0, HEAD_DIM, 2, dtype=f32) / f32(HEAD_DIM)))
    ang = np.arange(first, first + length, dtype=f32)[:, None] * inv[None, :]
    cos, sin = np.cos(ang).astype(f32), np.sin(ang).astype(f32)
    return (jnp.asarray(np.concatenate([cos, cos, cos, cos], axis=1)),
            jnp.asarray(np.concatenate([-sin, sin, -sin, sin], axis=1)))


def _forward(x, meta_tokens, norm_gain, w_in, w_out, lambda_q1, lambda_k1, lambda_q2, lambda_k2,
             subln_gain, final_norm_gain, *, qb, tm, tiles):
    assert qb == KEY_BLOCK and tm % KEY_BLOCK == 0
    b, s, d = x.shape
    wb = w_in[0]
    wo = w_out[0].astype(jnp.bfloat16)
    gain = norm_gain[0][None, :]
    x2d = x.reshape(b * s, d)

    (sbq, sbk, sbvt, sbg, dq, dk, dvt, dg, sbk_m, sbvt_m, dk_m, dvt_m) = _project(
        x2d, meta_tokens, gain, wb, _rope_tables(N_META, s), _rope_tables(0, N_META), tm)
    rows3 = lambda a: a.reshape(b, s, GROUP)
    blocks = lambda a: a.reshape(b, s // KEY_BLOCK, GROUP, KEY_BLOCK)

    kb = np.arange(KEY_BLOCK)
    later = (kb[None, :] > kb[:, None]).astype(np.float32)
    tri = np.concatenate([later, np.ones((SUBLANES_BF16_V7X, KEY_BLOCK), np.float32)], axis=0)
    trim = np.zeros((N_META, LANES_V7X), np.float32)
    trim[:, :N_META] = later[:N_META, :N_META]
    tri, trim = jnp.asarray(tri, jnp.bfloat16), jnp.asarray(trim, jnp.bfloat16)

    return _attention(
        (rows3(sbq), rows3(sbk), blocks(sbvt), sbk_m, sbvt_m, rows3(sbg), tri, trim),
        (rows3(dq), rows3(dk), blocks(dvt), dk_m, dvt_m, rows3(dg),
         lambda_q1, lambda_k1, lambda_q2, lambda_k2, subln_gain),
        wo, x, final_norm_gain[None, :], qb, tiles)


def _tiling(batch, seq):
    rows = batch * seq
    tm = next(t for t in PROJ_ROW_TILES if rows % t == 0 and t % KEY_BLOCK == 0)
    return dict(qb=KEY_BLOCK, tm=tm, tiles=GROUP // LANES_V7X)


def kernel(x, meta_tokens, norm_gain, w_in, w_out, lambda_q1, lambda_k1, lambda_q2, lambda_k2,
           subln_gain, final_norm_gain):
    return _forward(x, meta_tokens, norm_gain, w_in, w_out, lambda_q1, lambda_k1, lambda_q2,
                    lambda_k2, subln_gain, final_norm_gain, **_tiling(x.shape[0], x.shape[1]))
```

```python
import functools

import numpy as np
import jax
import jax.numpy as jnp
from jax import lax
from jax.experimental import pallas as pl
from jax.experimental.pallas import tpu as pltpu

LANES_V7X = 128
SUBLANES_BF16_V7X = 16
VMEM_LIMIT_BYTES_V7X = 56 * 1024 * 1024
PROJ_ROW_TILES = (1024, 512, 256)

N_META = 16
CHUNK = 64
ROPE_THETA = 10000.0
RMS_EPS = 1e-6
HEAD_DIM = 64
GROUP = 512
KEY_BLOCK = 256
NEG_BIG = -1e30
LOG2_E = 1.4426950408889634
LOOKAHEAD = 1
EXP_UNDERFLOW = -104.0
LAMBDA_INIT = 0.8 - 0.6 * float(np.exp(-0.3 * 0))


def _rms(x, g):
    return x * lax.rsqrt(jnp.mean(x * x, axis=-1, keepdims=True) + RMS_EPS) * g


def _proj_kernel(x_ref, xm_ref, g_ref, w_ref, cos_ref, sin_ref, cosm_ref, sinm_ref, *out_refs):
    scale = HEAD_DIM ** -0.5

    def silu(g):
        return g / (1.0 + jnp.exp(-g))

    def projector(rows_ref, cos_ref, sin_ref):
        u = _rms(rows_ref[...], g_ref[...]).astype(jnp.bfloat16)

        def col(c):
            w = w_ref[:, c * GROUP:(c + 1) * GROUP].astype(jnp.bfloat16)
            return jnp.dot(u, w, preferred_element_type=jnp.float32)

        def col_t(c, rows):
            v = col(c)
            if rows > v.shape[0]:
                v = _pad_rows(v, rows)
            return v.astype(jnp.bfloat16).T

        def rope(t):
            half = HEAD_DIM // 2
            lane = lax.broadcasted_iota(jnp.int32, (1, LANES_V7X), 1)
            first_half = (lane // half) % 2 == 0
            cos = jnp.concatenate([cos_ref[...]] * (GROUP // LANES_V7X), axis=1)
            sin = jnp.concatenate([sin_ref[...]] * (GROUP // LANES_V7X), axis=1)
            parts = []
            for j in range(GROUP // LANES_V7X):
                tile = t[:, j * LANES_V7X:(j + 1) * LANES_V7X]
                parts.append(jnp.where(first_half, pltpu.roll(tile, LANES_V7X - half, 1),
                                       pltpu.roll(tile, half, 1)))
            return t * cos + jnp.concatenate(parts, axis=1) * sin

        return col, col_t, rope

    (sbq_ref, sbk_ref, sbvt_ref, sbg_ref, dq_ref, dk_ref, dvt_ref, dg_ref,
     sbkm_ref, sbvtm_ref, dkm_ref, dvtm_ref) = out_refs

    @pl.when(pl.program_id(0) == 0)
    def _meta():
        col, col_t, rope = projector(xm_ref, cosm_ref, sinm_ref)
        sbkm_ref[...] = col(1).astype(jnp.bfloat16)
        sbvtm_ref[...] = col_t(2, LANES_V7X)
        dkm_ref[...] = rope(col(5)).astype(jnp.bfloat16)
        dvtm_ref[...] = col_t(6, LANES_V7X)

    col, col_t, rope = projector(x_ref, cos_ref, sin_ref)
    sbq_ref[...] = (col(0) * scale).astype(jnp.bfloat16)
    sbk_ref[...] = col(1).astype(jnp.bfloat16)
    sbg_ref[...] = silu(col(3))
    dq_ref[...] = (rope(col(4)) * scale).astype(jnp.bfloat16)
    dk_ref[...] = rope(col(5)).astype(jnp.bfloat16)
    dg_ref[...] = silu(col(7))
    for c, ref in ((2, sbvt_ref), (6, dvt_ref)):
        vt = col_t(c, 0)
        for j in range(ref.shape[0]):
            ref[j] = vt[:, j * KEY_BLOCK:(j + 1) * KEY_BLOCK]


def _project(x2d, meta_tokens, gain, wb, tables, meta_tables, tm):
    rows, d = x2d.shape
    n_pos_tiles = tables[0].shape[0] // tm
    row_spec = lambda width: pl.BlockSpec((tm, width), lambda i: (i, 0))
    tab_spec = pl.BlockSpec((tm, LANES_V7X), lambda i: (i % n_pos_tiles, 0))
    whole = lambda shape: pl.BlockSpec(shape, lambda i: (0,) * len(shape))
    bf, f32 = jnp.bfloat16, jnp.float32
    row_out = lambda dt: (row_spec(GROUP), jax.ShapeDtypeStruct((rows, GROUP), dt))
    vt_out = (pl.BlockSpec((tm // KEY_BLOCK, GROUP, KEY_BLOCK), lambda i: (i, 0, 0)),
              jax.ShapeDtypeStruct((rows // KEY_BLOCK, GROUP, KEY_BLOCK), bf))
    meta_out = lambda shape: (whole(shape), jax.ShapeDtypeStruct(shape, bf))
    outs = [row_out(bf), row_out(bf), vt_out, row_out(f32),
            row_out(bf), row_out(bf), vt_out, row_out(f32),
            meta_out((N_META, GROUP)), meta_out((GROUP, LANES_V7X)),
            meta_out((N_META, GROUP)), meta_out((GROUP, LANES_V7X))]
    return pl.pallas_call(
        _proj_kernel,
        grid=(rows // tm,),
        in_specs=[row_spec(d), whole(meta_tokens.shape), whole((1, d)),
                  pl.BlockSpec(wb.shape, lambda i: (0, 0), pipeline_mode=pl.Buffered(1)),
                  tab_spec, tab_spec,
                  whole(meta_tables[0].shape), whole(meta_tables[1].shape)],
        out_specs=[o[0] for o in outs],
        out_shape=[o[1] for o in outs],
        compiler_params=pltpu.CompilerParams(
            dimension_semantics=("arbitrary",), vmem_limit_bytes=VMEM_LIMIT_BYTES_V7X),
        name="proj",
    )(x2d, meta_tokens, gain, wb, *tables, *meta_tables)


_NT = (((1,), (1,)), ((), ()))


def _masked_queries(q_ref, p, keep):
    q = q_ref[0, :, p * LANES_V7X:(p + 1) * LANES_V7X].astype(jnp.float32)
    return jnp.where(keep, q, 0.0).astype(jnp.bfloat16)


def _pad_rows(a, rows):
    return jnp.concatenate([a, jnp.zeros((rows - a.shape[0], a.shape[1]), a.dtype)], axis=0)


class _Phases:
    pass


def _sb_program(q_ref, k_ref, vt_ref, km_ref, vmt_ref, g_ref, tri_ref, trim_ref, o_ref,
                acc_ref, car_ref, z_ref, *, qb, tiles, first_block):
    i = pl.program_id(2)
    lane = lax.broadcasted_iota(jnp.int32, (1, LANES_V7X), 1)
    key = lax.broadcasted_iota(jnp.int32, (KEY_BLOCK, qb), 0)
    qry = lax.broadcasted_iota(jnp.int32, (KEY_BLOCK, qb), 1)
    strict = key < qry
    chains = [(p, hh) for p in range(tiles) for hh in range(2)]
    qms = [_masked_queries(q_ref, p, (lane // HEAD_DIM) == hh) for p, hh in chains]

    def log_sigmoids(z, mask):
        log_beta = jnp.minimum(z, 0.0) - jnp.log(1.0 + jnp.exp2(jnp.abs(z) * (-LOG2_E)))
        log_keep = log_beta - z
        if mask is not None:
            log_keep = jnp.where(mask, log_keep, 0.0)
        return log_beta, log_keep.astype(jnp.bfloat16)

    def scores(n, p, start):
        kblk = k_ref[0, pl.ds(start, KEY_BLOCK), p * LANES_V7X:(p + 1) * LANES_V7X]
        z_ref[n] = lax.dot_general(kblk, qms[n], _NT, preferred_element_type=jnp.float32)

    half = KEY_BLOCK // 2
    strict_half = strict[:half, :half]

    def split(a, diagonal):
        return [a[:half, :half], a[:half, half:], a[half:, half:]] if diagonal else [a]

    def join(parts, diagonal):
        if not diagonal:
            return parts[0]
        older = jnp.concatenate(parts[:2], axis=1)
        newer = jnp.concatenate([jnp.zeros_like(parts[2]), parts[2]], axis=1)
        return jnp.concatenate([older, newer], axis=0)

    def stages(this_start, next_start, vtblk, diagonal):
        masks = [strict_half, None, strict_half] if diagonal else [None]
        log_betas, sums = {}, {}

        def stage1(n, p):
            if this_start is not None:
                scores(n, p, this_start)
            parts = [log_sigmoids(z, m) for z, m in zip(split(z_ref[n], diagonal), masks)]
            log_betas[n] = [lb for lb, _ in parts]
            log_keep = join([lk for _, lk in parts], diagonal)
            sums[n] = jnp.dot(tri_ref[...], log_keep, preferred_element_type=jnp.float32)
            if next_start is not None:
                scores(n, p, next_start)

        def stage2(n, p, hh):
            ws = []
            for lb, st, m in zip(log_betas.pop(n), split(sums[n][:KEY_BLOCK], diagonal), masks):
                w = jnp.exp(lb + st)
                ws.append((w if m is None else jnp.where(m, w, 0.0)).astype(jnp.bfloat16))
            tot = sums.pop(n)[KEY_BLOCK:KEY_BLOCK + 1]
            pv = jnp.dot(vtblk(p, hh), join(ws, diagonal), preferred_element_type=jnp.float32)
            if diagonal:
                acc_ref[n] = pv
                car_ref[n] = tot
            else:
                acc_ref[n] += pv * jnp.exp(car_ref[n])
                car_ref[n] += tot

        return ([functools.partial(stage1, n, p) for n, (p, hh) in enumerate(chains)],
                [functools.partial(stage2, n, p, hh) for n, (p, hh) in enumerate(chains)])

    def vt_at(j):
        return lambda p, hh: vt_ref[0, j, pl.ds(p * LANES_V7X + hh * HEAD_DIM, HEAD_DIM), :]

    def block_start(j):
        return pl.multiple_of(j * KEY_BLOCK, KEY_BLOCK)

    def prologue():
        for n, (p, hh) in enumerate(chains):
            scores(n, p, block_start(i))

    def older_blocks():
        def largest_carry():
            return jnp.max(functools.reduce(jnp.maximum,
                                            [car_ref[n] for n in range(len(chains))]))

        def live(state):
            t, carry_max = state
            return jnp.logical_and(t < i, carry_max > EXP_UNDERFLOW)

        def body(state):
            t, _ = state
            j = i - 1 - t
            first, second = stages(block_start(j), None, vt_at(j), False)
            for f in first + second:
                f()
            return t + 1, largest_carry()

        _, carry_max = lax.while_loop(live, body, (jnp.int32(1), largest_carry()))

        @pl.when(carry_max > EXP_UNDERFLOW)
        def _meta():
            zs = [lax.dot_general(km_ref[:, p * LANES_V7X:(p + 1) * LANES_V7X], qms[n], _NT,
                                  preferred_element_type=jnp.float32)
                  for n, (p, hh) in enumerate(chains)]
            log_betas, sticks = [], []
            for z in zs:
                log_beta, log_keep = log_sigmoids(z, None)
                sticks.append(jnp.dot(trim_ref[...], _pad_rows(log_keep, LANES_V7X),
                                      preferred_element_type=jnp.float32))
                log_betas.append(log_beta)
            ws = [_pad_rows(jnp.exp(log_betas[n] + sticks[n] + car_ref[n]).astype(jnp.bfloat16),
                            LANES_V7X) for n in range(len(chains))]
            for n, (p, hh) in enumerate(chains):
                acc_ref[n] += jnp.dot(vmt_ref[pl.ds(p * LANES_V7X + hh * HEAD_DIM, HEAD_DIM), :],
                                      ws[n], preferred_element_type=jnp.float32)

    def finalize():
        for p in range(tiles):
            o = jnp.concatenate([acc_ref[2 * p], acc_ref[2 * p + 1]], axis=0).T
            o_ref[0, :, p * LANES_V7X:(p + 1) * LANES_V7X] = (
                o * g_ref[0, :, p * LANES_V7X:(p + 1) * LANES_V7X]).astype(o_ref.dtype)

    ph = _Phases()
    ph.prologue = prologue
    ph.diagonal = stages(None, None if first_block else block_start(i - 1), vt_at(i), True)
    ph.older = None if first_block else stages(None, None, vt_at(i - 1), False)
    ph.older_blocks = older_blocks
    ph.finalize = finalize
    return ph


def _diff_program(q_ref, k_ref, vt_ref, km_ref, vmt_ref, g_ref,
                  lq1_ref, lk1_ref, lq2_ref, lk2_ref, sub_ref, o_ref, acc_ref, m_ref, l_ref, z_ref,
                  *, qb, tiles, first_block):
    i = pl.program_id(2)
    lane = lax.broadcasted_iota(jnp.int32, (1, LANES_V7X), 1)
    key = lax.broadcasted_iota(jnp.int32, (KEY_BLOCK, qb), 0)
    qry = lax.broadcasted_iota(jnp.int32, (KEY_BLOCK, qb), 1)
    chunk_ok = (qry // CHUNK) >= (key // CHUNK)
    chains = [(p, c) for p in range(tiles) for c in range(2)]
    qms = [_masked_queries(q_ref, p, (lane // HEAD_DIM) == c) for p, c in chains]
    ones = jnp.ones((SUBLANES_BF16_V7X, KEY_BLOCK), jnp.bfloat16)

    def update(n, s, vt_rows, pad_to, first):
        if first:
            m_new, pr = diagonal_probabilities(s)
        else:
            m_new = jnp.maximum(m_ref[n], jnp.max(s, axis=0, keepdims=True))
            pr = jnp.exp(s - m_new).astype(jnp.bfloat16)
        if pad_to is not None:
            pr = _pad_rows(pr, pad_to)
        pv = jnp.dot(vt_rows, pr, preferred_element_type=jnp.float32)
        num, den = pv[:LANES_V7X], pv[LANES_V7X:LANES_V7X + 1]
        if first:
            l_ref[n] = den
            acc_ref[n] = num
        else:
            alpha = jnp.exp(m_ref[n] - m_new)
            l_ref[n] = l_ref[n] * alpha + den
            acc_ref[n] = acc_ref[n] * alpha + num
        m_ref[n] = m_new

    def diagonal_probabilities(z):
        half = KEY_BLOCK // 2
        chunk_half = chunk_ok[:half, :half]
        older_early = jnp.where(chunk_half, z[:half, :half], NEG_BIG)
        older_late = z[:half, half:]
        newer_late = jnp.where(chunk_half, z[half:, half:], NEG_BIG)
        m_early = jnp.max(older_early, axis=0, keepdims=True)
        m_late = jnp.maximum(jnp.max(older_late, axis=0, keepdims=True),
                             jnp.max(newer_late, axis=0, keepdims=True))
        pe = lambda part, m: jnp.exp(part - m).astype(jnp.bfloat16)
        newer = jnp.concatenate([jnp.zeros((half, half), jnp.bfloat16), pe(newer_late, m_late)],
                                axis=1)
        older = jnp.concatenate([pe(older_early, m_early), pe(older_late, m_late)], axis=1)
        return (jnp.concatenate([m_early, m_late], axis=1),
                jnp.concatenate([older, newer], axis=0))

    def scores(n, p, start):
        kblk = k_ref[0, pl.ds(start, KEY_BLOCK), p * LANES_V7X:(p + 1) * LANES_V7X]
        z_ref[n] = lax.dot_general(kblk, qms[n], _NT, preferred_element_type=jnp.float32)

    def block_start(j):
        return pl.multiple_of(j * KEY_BLOCK, KEY_BLOCK)

    def block(j, next_start, first):
        ahead = {}

        def next_scores(n):
            p = chains[n][0]
            kblk = k_ref[0, pl.ds(next_start, KEY_BLOCK), p * LANES_V7X:(p + 1) * LANES_V7X]
            ahead[n] = lax.dot_general(kblk, qms[n], _NT, preferred_element_type=jnp.float32)

        def chain(n, p):
            if next_start is not None:
                for m in (range(LOOKAHEAD) if n == 0 else ()):
                    next_scores(m)
                if n + LOOKAHEAD < len(chains):
                    next_scores(n + LOOKAHEAD)
            s = z_ref[n]
            vt_rows = jnp.concatenate([vt_ref[0, j, p * LANES_V7X:(p + 1) * LANES_V7X, :], ones],
                                      axis=0)
            update(n, s, vt_rows, None, first)
            if next_start is not None:
                z_ref[n] = ahead.pop(n)

        return [functools.partial(chain, n, p) for n, (p, c) in enumerate(chains)]

    meta_scores = {}

    def diagonal_scores(n, p):
        scores(n, p, pl.multiple_of(i * qb, qb))

    def score_meta():
        for n, (p, c) in enumerate(chains):
            meta_scores[n] = lax.dot_general(km_ref[:, p * LANES_V7X:(p + 1) * LANES_V7X], qms[n],
                                             _NT, preferred_element_type=jnp.float32)

    def meta(n, p):
        vt_rows = jnp.concatenate([vmt_ref[p * LANES_V7X:(p + 1) * LANES_V7X, :],
                                   ones[:, :LANES_V7X]], axis=0)
        update(n, meta_scores.pop(n), vt_rows, LANES_V7X, False)

    def older_blocks():
        def body(t, c):
            for f in (block(2 * t, block_start(2 * t + 1), False)
                      + block(2 * t + 1, block_start(2 * t + 2), False)):
                f()
            return c

        prefetching = jnp.maximum(i - 1, 0)
        lax.fori_loop(0, prefetching // 2, body, 0)

        @pl.when(prefetching % 2 == 1)
        def _odd():
            for f in block(i - 2, block_start(i - 1), False):
                f()


    def finalize():
        lam = (jnp.exp(jnp.sum(lq1_ref[...] * lk1_ref[...], axis=-1, keepdims=True))
               - jnp.exp(jnp.sum(lq2_ref[...] * lk2_ref[...], axis=-1, keepdims=True))
               + LAMBDA_INIT)
        for p in range(tiles):
            ot = (acc_ref[2 * p] * (1.0 / l_ref[2 * p])
                  - acc_ref[2 * p + 1] * (lam / l_ref[2 * p + 1]))
            ot = ot * lax.rsqrt(jnp.mean(ot * ot, axis=0, keepdims=True) + RMS_EPS)
            o = ot.T * (sub_ref[...] * (1.0 - LAMBDA_INIT))
            o_ref[0, :, p * LANES_V7X:(p + 1) * LANES_V7X] = (
                o * g_ref[0, :, p * LANES_V7X:(p + 1) * LANES_V7X]).astype(o_ref.dtype)

    ph = _Phases()
    ph.diagonal_scores = [functools.partial(diagonal_scores, n, p)
                          for n, (p, c) in enumerate(chains)]
    ph.score_meta = score_meta
    ph.diagonal = block(i, None if first_block else block_start(0), True)
    ph.meta = [functools.partial(meta, n, p) for n, (p, c) in enumerate(chains)]
    ph.older_blocks = older_blocks
    ph.last = [] if first_block else block(i - 1, None, False)
    ph.finalize = finalize
    return ph


N_SB_IN, N_DF_IN = 8, 11


def _attn_kernel(*refs, qb, tiles):
    sb_in, df_in = refs[:N_SB_IN], refs[N_SB_IN:N_SB_IN + N_DF_IN]
    wo_ref, x_ref, fg_ref, y_ref = refs[N_SB_IN + N_DF_IN:N_SB_IN + N_DF_IN + 4]
    scratch = refs[N_SB_IN + N_DF_IN + 4:]
    mix_sb, mix_df = scratch[7:]

    def step(first_block):
        sb = _sb_program(*sb_in, mix_sb, *scratch[:3], qb=qb, tiles=tiles, first_block=first_block)
        df = _diff_program(*df_in, mix_df, *scratch[3:7], qb=qb, tiles=tiles,
                           first_block=first_block)
        sb.prologue()
        for first in sb.diagonal[0]:
            first()
        for second, other in zip(sb.diagonal[1], df.diagonal_scores):
            second()
            other()
        df.score_meta()
        if first_block:
            for other in df.diagonal + df.meta:
                other()
        else:
            for first, other in zip(sb.older[0], df.diagonal):
                first()
                other()
            for second, other in zip(sb.older[1], df.meta):
                second()
                other()
        sb.older_blocks()
        if not first_block:
            df.older_blocks()
        for other in df.last:
            other()
        df.finalize()
        sb.finalize()
        half = mix_sb.shape[-1]
        h = (x_ref[0]
             + jnp.dot(mix_sb[0], wo_ref[:half, :], preferred_element_type=jnp.float32)
             + jnp.dot(mix_df[0], wo_ref[half:, :], preferred_element_type=jnp.float32))
        y_ref[0] = _rms(h, fg_ref[...])

    pl.when(pl.program_id(2) == 0)(functools.partial(step, True))
    pl.when(pl.program_id(2) > 0)(functools.partial(step, False))


def _attention(sb_args, df_args, wo, x, final_gain, qb, tiles):
    q = sb_args[0]
    b, s, width = q.shape
    tw = tiles * LANES_V7X
    assert tw == width, "the fused kernel handles every head of a batch element per grid step"
    qspec = pl.BlockSpec((1, qb, tw), lambda bi, p, i: (bi, i, p))
    kspec = pl.BlockSpec((1, s, tw), lambda bi, p, i: (bi, 0, p))
    vtspec = pl.BlockSpec((1, s // KEY_BLOCK, tw, KEY_BLOCK), lambda bi, p, i: (bi, 0, p, 0))
    kmspec = pl.BlockSpec((N_META, tw), lambda bi, p, i: (0, p))
    vmtspec = pl.BlockSpec((tw, LANES_V7X), lambda bi, p, i: (p, 0))
    full = lambda a: pl.BlockSpec(a.shape, lambda bi, p, i: (0,) * a.ndim)
    common = [qspec, kspec, vtspec, kmspec, vmtspec, qspec]
    xspec = pl.BlockSpec((1, qb, x.shape[-1]), lambda bi, p, i: (bi, i, 0))
    in_specs = (common + [full(a) for a in sb_args[6:]] + common + [full(a) for a in df_args[6:]]
                + [full(wo), xspec, full(final_gain)])
    assert len(sb_args) == N_SB_IN and len(df_args) == N_DF_IN
    stat = pltpu.VMEM((2 * tiles, 1, qb), jnp.float32)
    score = pltpu.VMEM((2 * tiles, KEY_BLOCK, qb), jnp.float32)
    mix = pltpu.VMEM((1, qb, width), jnp.bfloat16)
    return pl.pallas_call(
        functools.partial(_attn_kernel, qb=qb, tiles=tiles),
        grid=(b, width // tw, s // qb),
        in_specs=in_specs,
        out_specs=xspec,
        out_shape=jax.ShapeDtypeStruct(x.shape, jnp.float32),
        scratch_shapes=[pltpu.VMEM((2 * tiles, HEAD_DIM, qb), jnp.float32), stat, score,
                        pltpu.VMEM((2 * tiles, LANES_V7X, qb), jnp.float32), stat, stat, score,
                        mix, mix],
        compiler_params=pltpu.CompilerParams(
            dimension_semantics=("parallel", "parallel", "arbitrary"),
            vmem_limit_bytes=VMEM_LIMIT_BYTES_V7X),
        name="attn",
    )(*sb_args, *df_args, wo, x, final_gain)


def _rope_tables(first, length):
    f32 = np.float32
    inv = f32(1.0) / (f32(ROPE_THETA) ** (np.arange(0, HEAD_DIM, 2, dtype=f32) / f32(HEAD_DIM)))
    ang = np.arange(first, first + length, dtype=f32)[:, None] * inv[None, :]
    cos, sin = np.cos(ang).astype(f32), np.sin(ang).astype(f32)
    return (jnp.asarray(np.concatenate([cos, cos, cos, cos], axis=1)),
            jnp.asarray(np.concatenate([-sin, sin, -sin, sin], axis=1)))


def _forward(x, meta_tokens, norm_gain, w_in, w_out, lambda_q1, lambda_k1, lambda_q2, lambda_k2,
             subln_gain, final_norm_gain, *, qb, tm, tiles):
    assert qb == KEY_BLOCK and tm % KEY_BLOCK == 0
    b, s, d = x.shape
    wb = w_in[0]
    wo = w_out[0].astype(jnp.bfloat16)
    gain = norm_gain[0][None, :]
    x2d = x.reshape(b * s, d)

    (sbq, sbk, sbvt, sbg, dq, dk, dvt, dg, sbk_m, sbvt_m, dk_m, dvt_m) = _project(
        x2d, meta_tokens, gain, wb, _rope_tables(N_META, s), _rope_tables(0, N_META), tm)
    rows3 = lambda a: a.reshape(b, s, GROUP)
    blocks = lambda a: a.reshape(b, s // KEY_BLOCK, GROUP, KEY_BLOCK)

    kb = np.arange(KEY_BLOCK)
    later = (kb[None, :] > kb[:, None]).astype(np.float32)
    tri = np.concatenate([later, np.ones((SUBLANES_BF16_V7X, KEY_BLOCK), np.float32)], axis=0)
    trim = np.zeros((N_META, LANES_V7X), np.float32)
    trim[:, :N_META] = later[:N_META, :N_META]
    tri, trim = jnp.asarray(tri, jnp.bfloat16), jnp.asarray(trim, jnp.bfloat16)

    return _attention(
        (rows3(sbq), rows3(sbk), blocks(sbvt), sbk_m, sbvt_m, rows3(sbg), tri, trim),
        (rows3(dq), rows3(dk), blocks(dvt), dk_m, dvt_m, rows3(dg),
         lambda_q1, lambda_k1, lambda_q2, lambda_k2, subln_gain),
        wo, x, final_norm_gain[None, :], qb, tiles)


def _tiling(batch, seq):
    rows = batch * seq
    tm = next(t for t in PROJ_ROW_TILES if rows % t == 0 and t % KEY_BLOCK == 0)
    return dict(qb=KEY_BLOCK, tm=tm, tiles=GROUP // LANES_V7X)


def kernel(x, meta_tokens, norm_gain, w_in, w_out, lambda_q1, lambda_k1, lambda_q2, lambda_k2,
           subln_gain, final_norm_gain):
    return _forward(x, meta_tokens, norm_gain, w_in, w_out, lambda_q1, lambda_k1, lambda_q2,
                    lambda_k2, subln_gain, final_norm_gain, **_tiling(x.shape[0], x.shape[1]))
```

```python
import functools

import numpy as np
import jax
import jax.numpy as jnp
from jax import lax
from jax.experimental import pallas as pl
from jax.experimental.pallas import tpu as pltpu

LANES_V7X = 128
SUBLANES_BF16_V7X = 16
VMEM_LIMIT_BYTES_V7X = 56 * 1024 * 1024
PROJ_ROW_TILES = (1024, 512, 256)

N_META = 16
CHUNK = 64
ROPE_THETA = 10000.0
RMS_EPS = 1e-6
HEAD_DIM = 64
GROUP = 512
KEY_BLOCK = 256
NEG_BIG = -1e30
LOG2_E = 1.4426950408889634
LOOKAHEAD = 1
EXP_UNDERFLOW = -104.0
LAMBDA_INIT = 0.8 - 0.6 * float(np.exp(-0.3 * 0))


def _rms(x, g):
    return x * lax.rsqrt(jnp.mean(x * x, axis=-1, keepdims=True) + RMS_EPS) * g


def _proj_kernel(x_ref, xm_ref, g_ref, w_ref, cos_ref, sin_ref, cosm_ref, sinm_ref, *out_refs):
    scale = HEAD_DIM ** -0.5

    def silu(g):
        return g / (1.0 + jnp.exp(-g))

    def projector(rows_ref, cos_ref, sin_ref):
        u = _rms(rows_ref[...], g_ref[...]).astype(jnp.bfloat16)

        def col(c):
            w = w_ref[:, c * GROUP:(c + 1) * GROUP].astype(jnp.bfloat16)
            return jnp.dot(u, w, preferred_element_type=jnp.float32)

        def col_t(c, rows):
            v = col(c)
            if rows > v.shape[0]:
                v = _pad_rows(v, rows)
            return v.astype(jnp.bfloat16).T

        def rope(t):
            half = HEAD_DIM // 2
            lane = lax.broadcasted_iota(jnp.int32, (1, LANES_V7X), 1)
            first_half = (lane // half) % 2 == 0
            cos = jnp.concatenate([cos_ref[...]] * (GROUP // LANES_V7X), axis=1)
            sin = jnp.concatenate([sin_ref[...]] * (GROUP // LANES_V7X), axis=1)
            parts = []
            for j in range(GROUP // LANES_V7X):
                tile = t[:, j * LANES_V7X:(j + 1) * LANES_V7X]
                parts.append(jnp.where(first_half, pltpu.roll(tile, LANES_V7X - half, 1),
                                       pltpu.roll(tile, half, 1)))
            return t * cos + jnp.concatenate(parts, axis=1) * sin

        return col, col_t, rope

    (sbq_ref, sbk_ref, sbvt_ref, sbg_ref, dq_ref, dk_ref, dvt_ref, dg_ref,
     sbkm_ref, sbvtm_ref, dkm_ref, dvtm_ref) = out_refs

    @pl.when(pl.program_id(0) == 0)
    def _meta():
        col, col_t, rope = projector(xm_ref, cosm_ref, sinm_ref)
        sbkm_ref[...] = col(1).astype(jnp.bfloat16)
        sbvtm_ref[...] = col_t(2, LANES_V7X)
        dkm_ref[...] = rope(col(5)).astype(jnp.bfloat16)
        dvtm_ref[...] = col_t(6, LANES_V7X)

    col, col_t, rope = projector(x_ref, cos_ref, sin_ref)
    sbq_ref[...] = (col(0) * scale).astype(jnp.bfloat16)
    sbk_ref[...] = col(1).astype(jnp.bfloat16)
    sbg_ref[...] = silu(col(3))
    dq_ref[...] = (rope(col(4)) * scale).astype(jnp.bfloat16)
    dk_ref[...] = rope(col(5)).astype(jnp.bfloat16)
    dg_ref[...] = silu(col(7))
    for c, ref in ((2, sbvt_ref), (6, dvt_ref)):
        vt = col_t(c, 0)
        for j in range(ref.shape[0]):
            ref[j] = vt[:, j * KEY_BLOCK:(j + 1) * KEY_BLOCK]


def _project(x2d, meta_tokens, gain, wb, tables, meta_tables, tm):
    rows, d = x2d.shape
    n_pos_tiles = tables[0].shape[0] // tm
    row_spec = lambda width: pl.BlockSpec((tm, width), lambda i: (i, 0))
    tab_spec = pl.BlockSpec((tm, LANES_V7X), lambda i: (i % n_pos_tiles, 0))
    whole = lambda shape: pl.BlockSpec(shape, lambda i: (0,) * len(shape))
    bf, f32 = jnp.bfloat16, jnp.float32
    row_out = lambda dt: (row_spec(GROUP), jax.ShapeDtypeStruct((rows, GROUP), dt))
    vt_out = (pl.BlockSpec((tm // KEY_BLOCK, GROUP, KEY_BLOCK), lambda i: (i, 0, 0)),
              jax.ShapeDtypeStruct((rows // KEY_BLOCK, GROUP, KEY_BLOCK), bf))
    meta_out = lambda shape: (whole(shape), jax.ShapeDtypeStruct(shape, bf))
    outs = [row_out(bf), row_out(bf), vt_out, row_out(f32),
            row_out(bf), row_out(bf), vt_out, row_out(f32),
            meta_out((N_META, GROUP)), meta_out((GROUP, LANES_V7X)),
            meta_out((N_META, GROUP)), meta_out((GROUP, LANES_V7X))]
    return pl.pallas_call(
        _proj_kernel,
        grid=(rows // tm,),
        in_specs=[row_spec(d), whole(meta_tokens.shape), whole((1, d)),
                  pl.BlockSpec(wb.shape, lambda i: (0, 0), pipeline_mode=pl.Buffered(1)),
                  tab_spec, tab_spec,
                  whole(meta_tables[0].shape), whole(meta_tables[1].shape)],
        out_specs=[o[0] for o in outs],
        out_shape=[o[1] for o in outs],
        compiler_params=pltpu.CompilerParams(
            dimension_semantics=("arbitrary",), vmem_limit_bytes=VMEM_LIMIT_BYTES_V7X),
        name="proj",
    )(x2d, meta_tokens, gain, wb, *tables, *meta_tables)


_NT = (((1,), (1,)), ((), ()))


def _masked_queries(q_ref, p, keep):
    q = q_ref[0, :, p * LANES_V7X:(p + 1) * LANES_V7X].astype(jnp.float32)
    return jnp.where(keep, q, 0.0).astype(jnp.bfloat16)


def _pad_rows(a, rows):
    return jnp.concatenate([a, jnp.zeros((rows - a.shape[0], a.shape[1]), a.dtype)], axis=0)


class _Phases:
    pass


def _sb_program(q_ref, k_ref, vt_ref, km_ref, vmt_ref, g_ref, tri_ref, trim_ref, o_ref,
                acc_ref, car_ref, z_ref, *, qb, tiles, first_block):
    i = pl.program_id(2)
    lane = lax.broadcasted_iota(jnp.int32, (1, LANES_V7X), 1)
    key = lax.broadcasted_iota(jnp.int32, (KEY_BLOCK, qb), 0)
    qry = lax.broadcasted_iota(jnp.int32, (KEY_BLOCK, qb), 1)
    strict = key < qry
    chains = [(p, hh) for p in range(tiles) for hh in range(2)]
    qms = [_masked_queries(q_ref, p, (lane // HEAD_DIM) == hh) for p, hh in chains]

    def log_sigmoids(z, mask):
        log_beta = jnp.minimum(z, 0.0) - jnp.log(1.0 + jnp.exp2(jnp.abs(z) * (-LOG2_E)))
        log_keep = log_beta - z
        if mask is not None:
            log_keep = jnp.where(mask, log_keep, 0.0)
        return log_beta, log_keep.astype(jnp.bfloat16)

    def scores(n, p, start):
        kblk = k_ref[0, pl.ds(start, KEY_BLOCK), p * LANES_V7X:(p + 1) * LANES_V7X]
        z_ref[n] = lax.dot_general(kblk, qms[n], _NT, preferred_element_type=jnp.float32)

    half = KEY_BLOCK // 2
    strict_half = strict[:half, :half]

    def split(a, diagonal):
        return [a[:half, :half], a[:half, half:], a[half:, half:]] if diagonal else [a]

    def join(parts, diagonal):
        if not diagonal:
            return parts[0]
        older = jnp.concatenate(parts[:2], axis=1)
        newer = jnp.concatenate([jnp.zeros_like(parts[2]), parts[2]], axis=1)
        return jnp.concatenate([older, newer], axis=0)

    def stages(this_start, next_start, vtblk, diagonal):
        masks = [strict_half, None, strict_half] if diagonal else [None]
        log_betas, sums = {}, {}

        def stage1(n, p):
            if this_start is not None:
                scores(n, p, this_start)
            parts = [log_sigmoids(z, m) for z, m in zip(split(z_ref[n], diagonal), masks)]
            log_betas[n] = [lb for lb, _ in parts]
            log_keep = join([lk for _, lk in parts], diagonal)
            sums[n] = jnp.dot(tri_ref[...], log_keep, preferred_element_type=jnp.float32)
            if next_start is not None:
                scores(n, p, next_start)

        def stage2(n, p, hh):
            ws = []
            for lb, st, m in zip(log_betas.pop(n), split(sums[n][:KEY_BLOCK], diagonal), masks):
                w = jnp.exp(lb + st)
                ws.append((w if m is None else jnp.where(m, w, 0.0)).astype(jnp.bfloat16))
            tot = sums.pop(n)[KEY_BLOCK:KEY_BLOCK + 1]
            pv = jnp.dot(vtblk(p, hh), join(ws, diagonal), preferred_element_type=jnp.float32)
            if diagonal:
                acc_ref[n] = pv
                car_ref[n] = tot
            else:
                acc_ref[n] += pv * jnp.exp(car_ref[n])
                car_ref[n] += tot

        return ([functools.partial(stage1, n, p) for n, (p, hh) in enumerate(chains)],
                [functools.partial(stage2, n, p, hh) for n, (p, hh) in enumerate(chains)])

    def vt_at(j):
        return lambda p, hh: vt_ref[0, j, pl.ds(p * LANES_V7X + hh * HEAD_DIM, HEAD_DIM), :]

    def block_start(j):
        return pl.multiple_of(j * KEY_BLOCK, KEY_BLOCK)

    def prologue():
        for n, (p, hh) in enumerate(chains):
            scores(n, p, block_start(i))

    def older_blocks():
        def largest_carry():
            return jnp.max(functools.reduce(jnp.maximum,
                                            [car_ref[n] for n in range(len(chains))]))

        def live(state):
            t, carry_max = state
            return jnp.logical_and(t < i, carry_max > EXP_UNDERFLOW)

        def body(state):
            t, _ = state
            j = i - 1 - t
            first, second = stages(block_start(j), None, vt_at(j), False)
            for f in first + second:
                f()
            return t + 1, largest_carry()

        if first_block:
            carry_max = largest_carry()
        else:
            _, carry_max = lax.while_loop(live, body, (jnp.int32(1), largest_carry()))

        @pl.when(carry_max > EXP_UNDERFLOW)
        def _meta():
            zs = [lax.dot_general(km_ref[:, p * LANES_V7X:(p + 1) * LANES_V7X], qms[n], _NT,
                                  preferred_element_type=jnp.float32)
                  for n, (p, hh) in enumerate(chains)]
            log_betas, sticks = [], []
            for z in zs:
                log_beta, log_keep = log_sigmoids(z, None)
                sticks.append(jnp.dot(trim_ref[...], _pad_rows(log_keep, LANES_V7X),
                                      preferred_element_type=jnp.float32))
                log_betas.append(log_beta)
            ws = [_pad_rows(jnp.exp(log_betas[n] + sticks[n] + car_ref[n]).astype(jnp.bfloat16),
                            LANES_V7X) for n in range(len(chains))]
            for n, (p, hh) in enumerate(chains):
                acc_ref[n] += jnp.dot(vmt_ref[pl.ds(p * LANES_V7X + hh * HEAD_DIM, HEAD_DIM), :],
                                      ws[n], preferred_element_type=jnp.float32)

    def finalize():
        for p in range(tiles):
            o = jnp.concatenate([acc_ref[2 * p], acc_ref[2 * p + 1]], axis=0).T
            o_ref[0, :, p * LANES_V7X:(p + 1) * LANES_V7X] = (
                o * g_ref[0, :, p * LANES_V7X:(p + 1) * LANES_V7X]).astype(o_ref.dtype)

    ph = _Phases()
    ph.prologue = prologue
    ph.diagonal = stages(None, None if first_block else block_start(i - 1), vt_at(i), True)
    ph.older = None if first_block else stages(None, None, vt_at(i - 1), False)
    ph.older_blocks = older_blocks
    ph.finalize = finalize
    return ph


def _diff_program(q_ref, k_ref, vt_ref, km_ref, vmt_ref, g_ref,
                  lq1_ref, lk1_ref, lq2_ref, lk2_ref, sub_ref, o_ref, acc_ref, m_ref, l_ref, z_ref,
                  *, qb, tiles, first_block):
    i = pl.program_id(2)
    lane = lax.broadcasted_iota(jnp.int32, (1, LANES_V7X), 1)
    key = lax.broadcasted_iota(jnp.int32, (KEY_BLOCK, qb), 0)
    qry = lax.broadcasted_iota(jnp.int32, (KEY_BLOCK, qb), 1)
    chunk_ok = (qry // CHUNK) >= (key // CHUNK)
    chains = [(p, c) for p in range(tiles) for c in range(2)]
    qms = [_masked_queries(q_ref, p, (lane // HEAD_DIM) == c) for p, c in chains]
    ones = jnp.ones((SUBLANES_BF16_V7X, KEY_BLOCK), jnp.bfloat16)

    def update(n, s, vt_rows, pad_to, first):
        if first:
            m_new, pr = diagonal_probabilities(s)
        else:
            m_new = jnp.maximum(m_ref[n], jnp.max(s, axis=0, keepdims=True))
            pr = jnp.exp(s - m_new).astype(jnp.bfloat16)
        if pad_to is not None:
            pr = _pad_rows(pr, pad_to)
        pv = jnp.dot(vt_rows, pr, preferred_element_type=jnp.float32)
        num, den = pv[:LANES_V7X], pv[LANES_V7X:LANES_V7X + 1]
        if first:
            l_ref[n] = den
            acc_ref[n] = num
        else:
            alpha = jnp.exp(m_ref[n] - m_new)
            l_ref[n] = l_ref[n] * alpha + den
            acc_ref[n] = acc_ref[n] * alpha + num
        m_ref[n] = m_new

    def diagonal_probabilities(z):
        half = KEY_BLOCK // 2
        chunk_half = chunk_ok[:half, :half]
        older_early = jnp.where(chunk_half, z[:half, :half], NEG_BIG)
        older_late = z[:half, half:]
        newer_late = jnp.where(chunk_half, z[half:, half:], NEG_BIG)
        m_early = jnp.max(older_early, axis=0, keepdims=True)
        m_late = jnp.maximum(jnp.max(older_late, axis=0, keepdims=True),
                             jnp.max(newer_late, axis=0, keepdims=True))
        pe = lambda part, m: jnp.exp(part - m).astype(jnp.bfloat16)
        newer = jnp.concatenate([jnp.zeros((half, half), jnp.bfloat16), pe(newer_late, m_late)],
                                axis=1)
        older = jnp.concatenate([pe(older_early, m_early), pe(older_late, m_late)], axis=1)
        return (jnp.concatenate([m_early, m_late], axis=1),
                jnp.concatenate([older, newer], axis=0))

    def scores(n, p, start):
        kblk = k_ref[0, pl.ds(start, KEY_BLOCK), p * LANES_V7X:(p + 1) * LANES_V7X]
        z_ref[n] = lax.dot_general(kblk, qms[n], _NT, preferred_element_type=jnp.float32)

    def block_start(j):
        return pl.multiple_of(j * KEY_BLOCK, KEY_BLOCK)

    def block(j, next_start, first):
        ahead = {}

        def next_scores(n):
            p = chains[n][0]
            kblk = k_ref[0, pl.ds(next_start, KEY_BLOCK), p * LANES_V7X:(p + 1) * LANES_V7X]
            ahead[n] = lax.dot_general(kblk, qms[n], _NT, preferred_element_type=jnp.float32)

        def chain(n, p):
            if next_start is not None:
                for m in (range(LOOKAHEAD) if n == 0 else ()):
                    next_scores(m)
                if n + LOOKAHEAD < len(chains):
                    next_scores(n + LOOKAHEAD)
            s = z_ref[n]
            vt_rows = jnp.concatenate([vt_ref[0, j, p * LANES_V7X:(p + 1) * LANES_V7X, :], ones],
                                      axis=0)
            update(n, s, vt_rows, None, first)
            if next_start is not None:
                z_ref[n] = ahead.pop(n)

        return [functools.partial(chain, n, p) for n, (p, c) in enumerate(chains)]

    meta_scores = {}

    def diagonal_scores(n, p):
        scores(n, p, pl.multiple_of(i * qb, qb))

    def score_meta():
        for n, (p, c) in enumerate(chains):
            meta_scores[n] = lax.dot_general(km_ref[:, p * LANES_V7X:(p + 1) * LANES_V7X], qms[n],
                                             _NT, preferred_element_type=jnp.float32)

    def meta(n, p):
        vt_rows = jnp.concatenate([vmt_ref[p * LANES_V7X:(p + 1) * LANES_V7X, :],
                                   ones[:, :LANES_V7X]], axis=0)
        update(n, meta_scores.pop(n), vt_rows, LANES_V7X, False)

    def older_blocks():
        def body(t, c):
            for f in (block(2 * t, block_start(2 * t + 1), False)
                      + block(2 * t + 1, block_start(2 * t + 2), False)):
                f()
            return c

        prefetching = jnp.maximum(i - 1, 0)
        lax.fori_loop(0, prefetching // 2, body, 0)

        @pl.when(prefetching % 2 == 1)
        def _odd():
            for f in block(i - 2, block_start(i - 1), False):
                f()


    def finalize():
        lam = (jnp.exp(jnp.sum(lq1_ref[...] * lk1_ref[...], axis=-1, keepdims=True))
               - jnp.exp(jnp.sum(lq2_ref[...] * lk2_ref[...], axis=-1, keepdims=True))
               + LAMBDA_INIT)
        for p in range(tiles):
            ot = (acc_ref[2 * p] * (1.0 / l_ref[2 * p])
                  - acc_ref[2 * p + 1] * (lam / l_ref[2 * p + 1]))
            ot = ot * lax.rsqrt(jnp.mean(ot * ot, axis=0, keepdims=True) + RMS_EPS)
            o = ot.T * (sub_ref[...] * (1.0 - LAMBDA_INIT))
            o_ref[0, :, p * LANES_V7X:(p + 1) * LANES_V7X] = (
                o * g_ref[0, :, p * LANES_V7X:(p + 1) * LANES_V7X]).astype(o_ref.dtype)

    ph = _Phases()
    ph.diagonal_scores = [functools.partial(diagonal_scores, n, p)
                          for n, (p, c) in enumerate(chains)]
    ph.score_meta = score_meta
    ph.diagonal = block(i, None if first_block else block_start(0), True)
    ph.meta = [functools.partial(meta, n, p) for n, (p, c) in enumerate(chains)]
    ph.older_blocks = older_blocks
    ph.last = [] if first_block else block(i - 1, None, False)
    ph.finalize = finalize
    return ph


N_SB_IN, N_DF_IN = 8, 11


def _attn_kernel(*refs, qb, tiles):
    sb_in, df_in = refs[:N_SB_IN], refs[N_SB_IN:N_SB_IN + N_DF_IN]
    wo_ref, x_ref, fg_ref, y_ref = refs[N_SB_IN + N_DF_IN:N_SB_IN + N_DF_IN + 4]
    scratch = refs[N_SB_IN + N_DF_IN + 4:]
    mix_sb, mix_df = scratch[7:]

    def step(first_block):
        sb = _sb_program(*sb_in, mix_sb, *scratch[:3], qb=qb, tiles=tiles, first_block=first_block)
        df = _diff_program(*df_in, mix_df, *scratch[3:7], qb=qb, tiles=tiles,
                           first_block=first_block)
        sb.prologue()
        for first in sb.diagonal[0]:
            first()
        for second, other in zip(sb.diagonal[1], df.diagonal_scores):
            second()
            other()
        df.score_meta()
        if first_block:
            for other in df.diagonal + df.meta:
                other()
        else:
            for first, other in zip(sb.older[0], df.diagonal):
                first()
                other()
            for second, other in zip(sb.older[1], df.meta):
                second()
                other()
        sb.older_blocks()
        if not first_block:
            df.older_blocks()
        for other in df.last:
            other()
        df.finalize()
        sb.finalize()
        half = mix_sb.shape[-1]
        h = (x_ref[0]
             + jnp.dot(mix_sb[0], wo_ref[:half, :], preferred_element_type=jnp.float32)
             + jnp.dot(mix_df[0], wo_ref[half:, :], preferred_element_type=jnp.float32))
        y_ref[0] = _rms(h, fg_ref[...])

    pl.when(pl.program_id(2) == 0)(functools.partial(step, True))
    pl.when(pl.program_id(2) > 0)(functools.partial(step, False))


def _attention(sb_args, df_args, wo, x, final_gain, qb, tiles):
    q = sb_args[0]
    b, s, width = q.shape
    tw = tiles * LANES_V7X
    assert tw == width, "the fused kernel handles every head of a batch element per grid step"
    qspec = pl.BlockSpec((1, qb, tw), lambda bi, p, i: (bi, i, p))
    kspec = pl.BlockSpec((1, s, tw), lambda bi, p, i: (bi, 0, p))
    vtspec = pl.BlockSpec((1, s // KEY_BLOCK, tw, KEY_BLOCK), lambda bi, p, i: (bi, 0, p, 0))
    kmspec = pl.BlockSpec((N_META, tw), lambda bi, p, i: (0, p))
    vmtspec = pl.BlockSpec((tw, LANES_V7X), lambda bi, p, i: (p, 0))
    full = lambda a: pl.BlockSpec(a.shape, lambda bi, p, i: (0,) * a.ndim)
    common = [qspec, kspec, vtspec, kmspec, vmtspec, qspec]
    xspec = pl.BlockSpec((1, qb, x.shape[-1]), lambda bi, p, i: (bi, i, 0))
    in_specs = (common + [full(a) for a in sb_args[6:]] + common + [full(a) for a in df_args[6:]]
                + [full(wo), xspec, full(final_gain)])
    assert len(sb_args) == N_SB_IN and len(df_args) == N_DF_IN
    stat = pltpu.VMEM((2 * tiles, 1, qb), jnp.float32)
    score = pltpu.VMEM((2 * tiles, KEY_BLOCK, qb), jnp.float32)
    mix = pltpu.VMEM((1, qb, width), jnp.bfloat16)
    return pl.pallas_call(
        functools.partial(_attn_kernel, qb=qb, tiles=tiles),
        grid=(b, width // tw, s // qb),
        in_specs=in_specs,
        out_specs=xspec,
        out_shape=jax.ShapeDtypeStruct(x.shape, jnp.float32),
        scratch_shapes=[pltpu.VMEM((2 * tiles, HEAD_DIM, qb), jnp.float32), stat, score,
                        pltpu.VMEM((2 * tiles, LANES_V7X, qb), jnp.float32), stat, stat, score,
                        mix, mix],
        compiler_params=pltpu.CompilerParams(
            dimension_semantics=("parallel", "parallel", "arbitrary"),
            vmem_limit_bytes=VMEM_LIMIT_BYTES_V7X),
        name="attn",
    )(*sb_args, *df_args, wo, x, final_gain)


def _rope_tables(first, length):
    f32 = np.float32
    inv = f32(1.0) / (f32(ROPE_THETA) ** (np.arange(0, HEAD_DIM, 2, dtype=f32) / f32(HEAD_DIM)))
    ang = np.arange(first, first + length, dtype=f32)[:, None] * inv[None, :]
    cos, sin = np.cos(ang).astype(f32), np.sin(ang).astype(f32)
    return (jnp.asarray(np.concatenate([cos, cos, cos, cos], axis=1)),
            jnp.asarray(np.concatenate([-sin, sin, -sin, sin], axis=1)))


def _forward(x, meta_tokens, norm_gain, w_in, w_out, lambda_q1, lambda_k1, lambda_q2, lambda_k2,
             subln_gain, final_norm_gain, *, qb, tm, tiles):
    assert qb == KEY_BLOCK and tm % KEY_BLOCK == 0
    b, s, d = x.shape
    wb = w_in[0]
    wo = w_out[0].astype(jnp.bfloat16)
    gain = norm_gain[0][None, :]
    x2d = x.reshape(b * s, d)

    (sbq, sbk, sbvt, sbg, dq, dk, dvt, dg, sbk_m, sbvt_m, dk_m, dvt_m) = _project(
        x2d, meta_tokens, gain, wb, _rope_tables(N_META, s), _rope_tables(0, N_META), tm)
    rows3 = lambda a: a.reshape(b, s, GROUP)
    blocks = lambda a: a.reshape(b, s // KEY_BLOCK, GROUP, KEY_BLOCK)

    kb = np.arange(KEY_BLOCK)
    later = (kb[None, :] > kb[:, None]).astype(np.float32)
    tri = np.concatenate([later, np.ones((SUBLANES_BF16_V7X, KEY_BLOCK), np.float32)], axis=0)
    trim = np.zeros((N_META, LANES_V7X), np.float32)
    trim[:, :N_META] = later[:N_META, :N_META]
    tri, trim = jnp.asarray(tri, jnp.bfloat16), jnp.asarray(trim, jnp.bfloat16)

    return _attention(
        (rows3(sbq), rows3(sbk), blocks(sbvt), sbk_m, sbvt_m, rows3(sbg), tri, trim),
        (rows3(dq), rows3(dk), blocks(dvt), dk_m, dvt_m, rows3(dg),
         lambda_q1, lambda_k1, lambda_q2, lambda_k2, subln_gain),
        wo, x, final_norm_gain[None, :], qb, tiles)


def _tiling(batch, seq):
    rows = batch * seq
    tm = next(t for t in PROJ_ROW_TILES if rows % t == 0 and t % KEY_BLOCK == 0)
    return dict(qb=KEY_BLOCK, tm=tm, tiles=GROUP // LANES_V7X)


def kernel(x, meta_tokens, norm_gain, w_in, w_out, lambda_q1, lambda_k1, lambda_q2, lambda_k2,
           subln_gain, final_norm_gain):
    return _forward(x, meta_tokens, norm_gain, w_in, w_out, lambda_q1, lambda_k1, lambda_q2,
                    lambda_k2, subln_gain, final_norm_gain, **_tiling(x.shape[0], x.shape[1]))
```

```python
import functools

import numpy as np
import jax
import jax.numpy as jnp
from jax import lax
from jax.experimental import pallas as pl
from jax.experimental.pallas import tpu as pltpu

LANES_V7X = 128
SUBLANES_BF16_V7X = 16
VMEM_LIMIT_BYTES_V7X = 56 * 1024 * 1024
PROJ_ROW_TILES = (1024, 512, 256)

N_META = 16
CHUNK = 64
ROPE_THETA = 10000.0
RMS_EPS = 1e-6
HEAD_DIM = 64
GROUP = 512
KEY_BLOCK = 256
NEG_BIG = -1e30
LOG2_E = 1.4426950408889634
LOOKAHEAD = 1
EXP_UNDERFLOW = -104.0
LAMBDA_INIT = 0.8 - 0.6 * float(np.exp(-0.3 * 0))


def _rms(x, g):
    return x * lax.rsqrt(jnp.mean(x * x, axis=-1, keepdims=True) + RMS_EPS) * g


def _proj_kernel(x_ref, xm_ref, g_ref, w_ref, cos_ref, sin_ref, cosm_ref, sinm_ref, *out_refs):
    scale = HEAD_DIM ** -0.5

    def silu(g):
        return g / (1.0 + jnp.exp(-g))

    def projector(rows_ref, cos_ref, sin_ref):
        u = _rms(rows_ref[...], g_ref[...]).astype(jnp.bfloat16)

        def col(c):
            w = w_ref[:, c * GROUP:(c + 1) * GROUP].astype(jnp.bfloat16)
            return jnp.dot(u, w, preferred_element_type=jnp.float32)

        def col_t(c, rows):
            v = col(c)
            if rows > v.shape[0]:
                v = _pad_rows(v, rows)
            return v.astype(jnp.bfloat16).T

        def rope(t):
            half = HEAD_DIM // 2
            lane = lax.broadcasted_iota(jnp.int32, (1, LANES_V7X), 1)
            first_half = (lane // half) % 2 == 0
            cos = jnp.concatenate([cos_ref[...]] * (GROUP // LANES_V7X), axis=1)
            sin = jnp.concatenate([sin_ref[...]] * (GROUP // LANES_V7X), axis=1)
            parts = []
            for j in range(GROUP // LANES_V7X):
                tile = t[:, j * LANES_V7X:(j + 1) * LANES_V7X]
                parts.append(jnp.where(first_half, pltpu.roll(tile, LANES_V7X - half, 1),
                                       pltpu.roll(tile, half, 1)))
            return t * cos + jnp.concatenate(parts, axis=1) * sin

        return col, col_t, rope

    (sbq_ref, sbk_ref, sbvt_ref, sbg_ref, dq_ref, dk_ref, dvt_ref, dg_ref,
     sbkm_ref, sbvtm_ref, dkm_ref, dvtm_ref) = out_refs

    @pl.when(pl.program_id(0) == 0)
    def _meta():
        col, col_t, rope = projector(xm_ref, cosm_ref, sinm_ref)
        sbkm_ref[...] = col(1).astype(jnp.bfloat16)
        sbvtm_ref[...] = col_t(2, LANES_V7X)
        dkm_ref[...] = rope(col(5)).astype(jnp.bfloat16)
        dvtm_ref[...] = col_t(6, LANES_V7X)

    col, col_t, rope = projector(x_ref, cos_ref, sin_ref)
    sbq_ref[...] = (col(0) * scale).astype(jnp.bfloat16)
    sbk_ref[...] = col(1).astype(jnp.bfloat16)
    sbg_ref[...] = silu(col(3))
    dq_ref[...] = (rope(col(4)) * scale).astype(jnp.bfloat16)
    dk_ref[...] = rope(col(5)).astype(jnp.bfloat16)
    dg_ref[...] = silu(col(7))
    for c, ref in ((2, sbvt_ref), (6, dvt_ref)):
        vt = col_t(c, 0)
        for j in range(ref.shape[0]):
            ref[j] = vt[:, j * KEY_BLOCK:(j + 1) * KEY_BLOCK]


def _project(x2d, meta_tokens, gain, wb, tables, meta_tables, tm):
    rows, d = x2d.shape
    n_pos_tiles = tables[0].shape[0] // tm
    row_spec = lambda width: pl.BlockSpec((tm, width), lambda i: (i, 0))
    tab_spec = pl.BlockSpec((tm, LANES_V7X), lambda i: (i % n_pos_tiles, 0))
    whole = lambda shape: pl.BlockSpec(shape, lambda i: (0,) * len(shape))
    bf, f32 = jnp.bfloat16, jnp.float32
    row_out = lambda dt: (row_spec(GROUP), jax.ShapeDtypeStruct((rows, GROUP), dt))
    vt_out = (pl.BlockSpec((tm // KEY_BLOCK, GROUP, KEY_BLOCK), lambda i: (i, 0, 0)),
              jax.ShapeDtypeStruct((rows // KEY_BLOCK, GROUP, KEY_BLOCK), bf))
    meta_out = lambda shape: (whole(shape), jax.ShapeDtypeStruct(shape, bf))
    outs = [row_out(bf), row_out(bf), vt_out, row_out(f32),
            row_out(bf), row_out(bf), vt_out, row_out(f32),
            meta_out((N_META, GROUP)), meta_out((GROUP, LANES_V7X)),
            meta_out((N_META, GROUP)), meta_out((GROUP, LANES_V7X))]
    return pl.pallas_call(
        _proj_kernel,
        grid=(rows // tm,),
        in_specs=[row_spec(d), whole(meta_tokens.shape), whole((1, d)),
                  pl.BlockSpec(wb.shape, lambda i: (0, 0), pipeline_mode=pl.Buffered(1)),
                  tab_spec, tab_spec,
                  whole(meta_tables[0].shape), whole(meta_tables[1].shape)],
        out_specs=[o[0] for o in outs],
        out_shape=[o[1] for o in outs],
        compiler_params=pltpu.CompilerParams(
            dimension_semantics=("arbitrary",), vmem_limit_bytes=VMEM_LIMIT_BYTES_V7X),
        name="proj",
    )(x2d, meta_tokens, gain, wb, *tables, *meta_tables)


_NT = (((1,), (1,)), ((), ()))


def _masked_queries(q_ref, p, keep):
    q = q_ref[0, :, p * LANES_V7X:(p + 1) * LANES_V7X].astype(jnp.float32)
    return jnp.where(keep, q, 0.0).astype(jnp.bfloat16)


def _pad_rows(a, rows):
    return jnp.concatenate([a, jnp.zeros((rows - a.shape[0], a.shape[1]), a.dtype)], axis=0)


class _Phases:
    pass


def _sb_program(q_ref, k_ref, vt_ref, km_ref, vmt_ref, g_ref, tri_ref, trim_ref, o_ref,
                acc_ref, car_ref, z_ref, *, qb, tiles, first_block):
    i = pl.program_id(2)
    lane = lax.broadcasted_iota(jnp.int32, (1, LANES_V7X), 1)
    key = lax.broadcasted_iota(jnp.int32, (KEY_BLOCK, qb), 0)
    qry = lax.broadcasted_iota(jnp.int32, (KEY_BLOCK, qb), 1)
    strict = key < qry
    chains = [(p, hh) for p in range(tiles) for hh in range(2)]
    qms = [_masked_queries(q_ref, p, (lane // HEAD_DIM) == hh) for p, hh in chains]

    def log_sigmoids(z, mask):
        log_beta = jnp.minimum(z, 0.0) - jnp.log(1.0 + jnp.exp2(jnp.abs(z) * (-LOG2_E)))
        log_keep = log_beta - z
        if mask is not None:
            log_keep = jnp.where(mask, log_keep, 0.0)
        return log_beta, log_keep.astype(jnp.bfloat16)

    def scores(n, p, start):
        kblk = k_ref[0, pl.ds(start, KEY_BLOCK), p * LANES_V7X:(p + 1) * LANES_V7X]
        z_ref[n] = lax.dot_general(kblk, qms[n], _NT, preferred_element_type=jnp.float32)

    half = KEY_BLOCK // 2
    strict_half = strict[:half, :half]

    def split(a, diagonal):
        return [a[:half, :half], a[:half, half:], a[half:, half:]] if diagonal else [a]

    def join(parts, diagonal):
        if not diagonal:
            return parts[0]
        older = jnp.concatenate(parts[:2], axis=1)
        newer = jnp.concatenate([jnp.zeros_like(parts[2]), parts[2]], axis=1)
        return jnp.concatenate([older, newer], axis=0)

    def stages(this_start, next_start, vtblk, diagonal):
        masks = [strict_half, None, strict_half] if diagonal else [None]
        log_betas, sums = {}, {}

        def stage1(n, p):
            if this_start is not None:
                scores(n, p, this_start)
            parts = [log_sigmoids(z, m) for z, m in zip(split(z_ref[n], diagonal), masks)]
            log_betas[n] = [lb for lb, _ in parts]
            log_keep = join([lk for _, lk in parts], diagonal)
            sums[n] = jnp.dot(tri_ref[...], log_keep, preferred_element_type=jnp.float32)
            if next_start is not None:
                scores(n, p, next_start)

        def stage2(n, p, hh):
            ws = []
            for lb, st, m in zip(log_betas.pop(n), split(sums[n][:KEY_BLOCK], diagonal), masks):
                w = jnp.exp(lb + st)
                ws.append((w if m is None else jnp.where(m, w, 0.0)).astype(jnp.bfloat16))
            tot = sums.pop(n)[KEY_BLOCK:KEY_BLOCK + 1]
            pv = jnp.dot(vtblk(p, hh), join(ws, diagonal), preferred_element_type=jnp.float32)
            if diagonal:
                acc_ref[n] = pv
                car_ref[n] = tot
            else:
                acc_ref[n] += pv * jnp.exp(car_ref[n])
                car_ref[n] += tot

        return ([functools.partial(stage1, n, p) for n, (p, hh) in enumerate(chains)],
                [functools.partial(stage2, n, p, hh) for n, (p, hh) in enumerate(chains)])

    def vt_at(j):
        return lambda p, hh: vt_ref[0, j, pl.ds(p * LANES_V7X + hh * HEAD_DIM, HEAD_DIM), :]

    def block_start(j):
        return pl.multiple_of(j * KEY_BLOCK, KEY_BLOCK)

    def prologue():
        for n, (p, hh) in enumerate(chains):
            scores(n, p, block_start(i))

    def older_blocks():
        def largest_carry():
            return jnp.max(functools.reduce(jnp.maximum,
                                            [car_ref[n] for n in range(len(chains))]))

        def live(state):
            t, carry_max = state
            return jnp.logical_and(t < i, carry_max > EXP_UNDERFLOW)

        def body(state):
            t, _ = state
            j = i - 1 - t
            first, second = stages(block_start(j), None, vt_at(j), False)
            for f in first + second:
                f()
            return t + 1, largest_carry()

        if first_block:
            carry_max = largest_carry()
        else:
            _, carry_max = lax.while_loop(live, body, (jnp.int32(1), largest_carry()))

        @pl.when(carry_max > EXP_UNDERFLOW)
        def _meta():
            zs = [lax.dot_general(km_ref[:, p * LANES_V7X:(p + 1) * LANES_V7X], qms[n], _NT,
                                  preferred_element_type=jnp.float32)
                  for n, (p, hh) in enumerate(chains)]
            log_betas, sticks = [], []
            for z in zs:
                log_beta, log_keep = log_sigmoids(z, None)
                sticks.append(jnp.dot(trim_ref[...], _pad_rows(log_keep, LANES_V7X),
                                      preferred_element_type=jnp.float32))
                log_betas.append(log_beta)
            ws = [_pad_rows(jnp.exp(log_betas[n] + sticks[n] + car_ref[n]).astype(jnp.bfloat16),
                            LANES_V7X) for n in range(len(chains))]
            for n, (p, hh) in enumerate(chains):
                acc_ref[n] += jnp.dot(vmt_ref[pl.ds(p * LANES_V7X + hh * HEAD_DIM, HEAD_DIM), :],
                                      ws[n], preferred_element_type=jnp.float32)

    def finalize():
        for p in range(tiles):
            o = jnp.concatenate([acc_ref[2 * p], acc_ref[2 * p + 1]], axis=0).T
            o_ref[0, :, p * LANES_V7X:(p + 1) * LANES_V7X] = (
                o * g_ref[0, :, p * LANES_V7X:(p + 1) * LANES_V7X]).astype(o_ref.dtype)

    ph = _Phases()
    ph.prologue = prologue
    ph.diagonal = stages(None, None if first_block else block_start(i - 1), vt_at(i), True)
    ph.older = None if first_block else stages(None, None, vt_at(i - 1), False)
    ph.older_blocks = older_blocks
    ph.finalize = finalize
    return ph


def _diff_program(q_ref, k_ref, vt_ref, km_ref, vmt_ref, g_ref,
                  lq1_ref, lk1_ref, lq2_ref, lk2_ref, sub_ref, o_ref, acc_ref, m_ref, l_ref, z_ref,
                  *, qb, tiles, first_block):
    i = pl.program_id(2)
    lane = lax.broadcasted_iota(jnp.int32, (1, LANES_V7X), 1)
    key = lax.broadcasted_iota(jnp.int32, (KEY_BLOCK, qb), 0)
    qry = lax.broadcasted_iota(jnp.int32, (KEY_BLOCK, qb), 1)
    chunk_ok = (qry // CHUNK) >= (key // CHUNK)
    chains = [(p, c) for p in range(tiles) for c in range(2)]
    qms = [_masked_queries(q_ref, p, (lane // HEAD_DIM) == c) for p, c in chains]
    ones = jnp.ones((SUBLANES_BF16_V7X, KEY_BLOCK), jnp.bfloat16)

    def update(n, s, vt_rows, pad_to, first):
        if first:
            m_new, pr = diagonal_probabilities(s)
        else:
            m_new = jnp.maximum(m_ref[n], jnp.max(s, axis=0, keepdims=True))
            pr = jnp.exp(s - m_new).astype(jnp.bfloat16)
        if pad_to is not None:
            pr = _pad_rows(pr, pad_to)
        pv = jnp.dot(vt_rows, pr, preferred_element_type=jnp.float32)
        num, den = pv[:LANES_V7X], pv[LANES_V7X:LANES_V7X + 1]
        if first:
            l_ref[n] = den
            acc_ref[n] = num
        else:
            alpha = jnp.exp(m_ref[n] - m_new)
            l_ref[n] = l_ref[n] * alpha + den
            acc_ref[n] = acc_ref[n] * alpha + num
        m_ref[n] = m_new

    def diagonal_probabilities(z):
        half = KEY_BLOCK // 2
        chunk_half = chunk_ok[:half, :half]
        older_early = jnp.where(chunk_half, z[:half, :half], NEG_BIG)
        older_late = z[:half, half:]
        newer_late = jnp.where(chunk_half, z[half:, half:], NEG_BIG)
        m_early = jnp.max(older_early, axis=0, keepdims=True)
        m_late = jnp.maximum(jnp.max(older_late, axis=0, keepdims=True),
                             jnp.max(newer_late, axis=0, keepdims=True))
        pe = lambda part, m: jnp.exp(part - m).astype(jnp.bfloat16)
        newer = jnp.concatenate([jnp.zeros((half, half), jnp.bfloat16), pe(newer_late, m_late)],
                                axis=1)
        older = jnp.concatenate([pe(older_early, m_early), pe(older_late, m_late)], axis=1)
        return (jnp.concatenate([m_early, m_late], axis=1),
                jnp.concatenate([older, newer], axis=0))

    def scores(n, p, start):
        kblk = k_ref[0, pl.ds(start, KEY_BLOCK), p * LANES_V7X:(p + 1) * LANES_V7X]
        z_ref[n] = lax.dot_general(kblk, qms[n], _NT, preferred_element_type=jnp.float32)

    def block_start(j):
        return pl.multiple_of(j * KEY_BLOCK, KEY_BLOCK)

    def block(j, next_start, first):
        ahead = {}

        def next_scores(n):
            p = chains[n][0]
            kblk = k_ref[0, pl.ds(next_start, KEY_BLOCK), p * LANES_V7X:(p + 1) * LANES_V7X]
            ahead[n] = lax.dot_general(kblk, qms[n], _NT, preferred_element_type=jnp.float32)

        def chain(n, p):
            if next_start is not None:
                for m in (range(LOOKAHEAD) if n == 0 else ()):
                    next_scores(m)
                if n + LOOKAHEAD < len(chains):
                    next_scores(n + LOOKAHEAD)
            s = z_ref[n]
            vt_rows = jnp.concatenate([vt_ref[0, j, p * LANES_V7X:(p + 1) * LANES_V7X, :], ones],
                                      axis=0)
            update(n, s, vt_rows, None, first)
            if next_start is not None:
                z_ref[n] = ahead.pop(n)

        return [functools.partial(chain, n, p) for n, (p, c) in enumerate(chains)]

    meta_scores = {}

    def diagonal_scores(n, p):
        scores(n, p, pl.multiple_of(i * qb, qb))

    def score_meta():
        for n, (p, c) in enumerate(chains):
            meta_scores[n] = lax.dot_general(km_ref[:, p * LANES_V7X:(p + 1) * LANES_V7X], qms[n],
                                             _NT, preferred_element_type=jnp.float32)

    def meta(n, p):
        vt_rows = jnp.concatenate([vmt_ref[p * LANES_V7X:(p + 1) * LANES_V7X, :],
                                   ones[:, :LANES_V7X]], axis=0)
        update(n, meta_scores.pop(n), vt_rows, LANES_V7X, False)

    def older_blocks():
        def body(t, c):
            for f in (block(2 * t, block_start(2 * t + 1), False)
                      + block(2 * t + 1, block_start(2 * t + 2), False)):
                f()
            return c

        prefetching = jnp.maximum(i - 1, 0)
        lax.fori_loop(0, prefetching // 2, body, 0)

        @pl.when(prefetching % 2 == 1)
        def _odd():
            for f in block(i - 2, block_start(i - 1), False):
                f()


    def finalize():
        lam = (jnp.exp(jnp.sum(lq1_ref[...] * lk1_ref[...], axis=-1, keepdims=True))
               - jnp.exp(jnp.sum(lq2_ref[...] * lk2_ref[...], axis=-1, keepdims=True))
               + LAMBDA_INIT)
        for p in range(tiles):
            ot = (acc_ref[2 * p] * (1.0 / l_ref[2 * p])
                  - acc_ref[2 * p + 1] * (lam / l_ref[2 * p + 1]))
            ot = ot * lax.rsqrt(jnp.mean(ot * ot, axis=0, keepdims=True) + RMS_EPS)
            o = ot.T * (sub_ref[...] * (1.0 - LAMBDA_INIT))
            o_ref[0, :, p * LANES_V7X:(p + 1) * LANES_V7X] = (
                o * g_ref[0, :, p * LANES_V7X:(p + 1) * LANES_V7X]).astype(o_ref.dtype)

    ph = _Phases()
    ph.diagonal_scores = [functools.partial(diagonal_scores, n, p)
                          for n, (p, c) in enumerate(chains)]
    ph.score_meta = score_meta
    ph.diagonal = block(i, None if first_block else block_start(0), True)
    ph.meta = [functools.partial(meta, n, p) for n, (p, c) in enumerate(chains)]
    ph.older_blocks = older_blocks
    ph.last = [] if first_block else block(i - 1, None, False)
    ph.finalize = finalize
    return ph


N_SB_IN, N_DF_IN = 8, 11


def _attn_kernel(*refs, qb, tiles):
    sb_in, df_in = refs[:N_SB_IN], refs[N_SB_IN:N_SB_IN + N_DF_IN]
    wo_ref, x_ref, fg_ref, y_ref = refs[N_SB_IN + N_DF_IN:N_SB_IN + N_DF_IN + 4]
    scratch = refs[N_SB_IN + N_DF_IN + 4:]
    mix_sb, mix_df = scratch[7:]

    def step(first_block):
        sb = _sb_program(*sb_in, mix_sb, *scratch[:3], qb=qb, tiles=tiles, first_block=first_block)
        df = _diff_program(*df_in, mix_df, *scratch[3:7], qb=qb, tiles=tiles,
                           first_block=first_block)
        sb.prologue()
        if first_block:
            for first, other in zip(sb.diagonal[0], df.diagonal_scores):
                first()
                other()
            df.score_meta()
            for second, other in zip(sb.diagonal[1], df.diagonal):
                second()
                other()
            for other in df.meta:
                other()
        else:
            for first in sb.diagonal[0]:
                first()
            for second, other in zip(sb.diagonal[1], df.diagonal_scores):
                second()
                other()
            df.score_meta()
            for first, other in zip(sb.older[0], df.diagonal):
                first()
                other()
            for second, other in zip(sb.older[1], df.meta):
                second()
                other()
        sb.older_blocks()
        if not first_block:
            df.older_blocks()
        for other in df.last:
            other()
        df.finalize()
        sb.finalize()
        half = mix_sb.shape[-1]
        h = (x_ref[0]
             + jnp.dot(mix_sb[0], wo_ref[:half, :], preferred_element_type=jnp.float32)
             + jnp.dot(mix_df[0], wo_ref[half:, :], preferred_element_type=jnp.float32))
        y_ref[0] = _rms(h, fg_ref[...])

    pl.when(pl.program_id(2) == 0)(functools.partial(step, True))
    pl.when(pl.program_id(2) > 0)(functools.partial(step, False))


def _attention(sb_args, df_args, wo, x, final_gain, qb, tiles):
    q = sb_args[0]
    b, s, width = q.shape
    tw = tiles * LANES_V7X
    assert tw == width, "the fused kernel handles every head of a batch element per grid step"
    qspec = pl.BlockSpec((1, qb, tw), lambda bi, p, i: (bi, i, p))
    kspec = pl.BlockSpec((1, s, tw), lambda bi, p, i: (bi, 0, p))
    vtspec = pl.BlockSpec((1, s // KEY_BLOCK, tw, KEY_BLOCK), lambda bi, p, i: (bi, 0, p, 0))
    kmspec = pl.BlockSpec((N_META, tw), lambda bi, p, i: (0, p))
    vmtspec = pl.BlockSpec((tw, LANES_V7X), lambda bi, p, i: (p, 0))
    full = lambda a: pl.BlockSpec(a.shape, lambda bi, p, i: (0,) * a.ndim)
    common = [qspec, kspec, vtspec, kmspec, vmtspec, qspec]
    xspec = pl.BlockSpec((1, qb, x.shape[-1]), lambda bi, p, i: (bi, i, 0))
    in_specs = (common + [full(a) for a in sb_args[6:]] + common + [full(a) for a in df_args[6:]]
                + [full(wo), xspec, full(final_gain)])
    assert len(sb_args) == N_SB_IN and len(df_args) == N_DF_IN
    stat = pltpu.VMEM((2 * tiles, 1, qb), jnp.float32)
    score = pltpu.VMEM((2 * tiles, KEY_BLOCK, qb), jnp.float32)
    mix = pltpu.VMEM((1, qb, width), jnp.bfloat16)
    return pl.pallas_call(
        functools.partial(_attn_kernel, qb=qb, tiles=tiles),
        grid=(b, width // tw, s // qb),
        in_specs=in_specs,
        out_specs=xspec,
        out_shape=jax.ShapeDtypeStruct(x.shape, jnp.float32),
        scratch_shapes=[pltpu.VMEM((2 * tiles, HEAD_DIM, qb), jnp.float32), stat, score,
                        pltpu.VMEM((2 * tiles, LANES_V7X, qb), jnp.float32), stat, stat, score,
                        mix, mix],
        compiler_params=pltpu.CompilerParams(
            dimension_semantics=("parallel", "parallel", "arbitrary"),
            vmem_limit_bytes=VMEM_LIMIT_BYTES_V7X),
        name="attn",
    )(*sb_args, *df_args, wo, x, final_gain)


def _rope_tables(first, length):
    f32 = np.float32
    inv = f32(1.0) / (f32(ROPE_THETA) ** (np.arange(0, HEAD_DIM, 2, dtype=f32) / f32(HEAD_DIM)))
    ang = np.arange(first, first + length, dtype=f32)[:, None] * inv[None, :]
    cos, sin = np.cos(ang).astype(f32), np.sin(ang).astype(f32)
    return (jnp.asarray(np.concatenate([cos, cos, cos, cos], axis=1)),
            jnp.asarray(np.concatenate([-sin, sin, -sin, sin], axis=1)))


def _forward(x, meta_tokens, norm_gain, w_in, w_out, lambda_q1, lambda_k1, lambda_q2, lambda_k2,
             subln_gain, final_norm_gain, *, qb, tm, tiles):
    assert qb == KEY_BLOCK and tm % KEY_BLOCK == 0
    b, s, d = x.shape
    wb = w_in[0]
    wo = w_out[0].astype(jnp.bfloat16)
    gain = norm_gain[0][None, :]
    x2d = x.reshape(b * s, d)

    (sbq, sbk, sbvt, sbg, dq, dk, dvt, dg, sbk_m, sbvt_m, dk_m, dvt_m) = _project(
        x2d, meta_tokens, gain, wb, _rope_tables(N_META, s), _rope_tables(0, N_META), tm)
    rows3 = lambda a: a.reshape(b, s, GROUP)
    blocks = lambda a: a.reshape(b, s // KEY_BLOCK, GROUP, KEY_BLOCK)

    kb = np.arange(KEY_BLOCK)
    later = (kb[None, :] > kb[:, None]).astype(np.float32)
    tri = np.concatenate([later, np.ones((SUBLANES_BF16_V7X, KEY_BLOCK), np.float32)], axis=0)
    trim = np.zeros((N_META, LANES_V7X), np.float32)
    trim[:, :N_META] = later[:N_META, :N_META]
    tri, trim = jnp.asarray(tri, jnp.bfloat16), jnp.asarray(trim, jnp.bfloat16)

    return _attention(
        (rows3(sbq), rows3(sbk), blocks(sbvt), sbk_m, sbvt_m, rows3(sbg), tri, trim),
        (rows3(dq), rows3(dk), blocks(dvt), dk_m, dvt_m, rows3(dg),
         lambda_q1, lambda_k1, lambda_q2, lambda_k2, subln_gain),
        wo, x, final_norm_gain[None, :], qb, tiles)


def _tiling(batch, seq):
    rows = batch * seq
    tm = next(t for t in PROJ_ROW_TILES if rows % t == 0 and t % KEY_BLOCK == 0)
    return dict(qb=KEY_BLOCK, tm=tm, tiles=GROUP // LANES_V7X)


def kernel(x, meta_tokens, norm_gain, w_in, w_out, lambda_q1, lambda_k1, lambda_q2, lambda_k2,
           subln_gain, final_norm_gain):
    return _forward(x, meta_tokens, norm_gain, w_in, w_out, lambda_q1, lambda_k1, lambda_q2,
                    lambda_k2, subln_gain, final_norm_gain, **_tiling(x.shape[0], x.shape[1]))
```
